```python
import math
import jax
import jax.numpy as jnp
from jax import lax
import numpy as np

D_MODEL = 1024
BATCH = 8
SEQ = 8192
DEPTH = 4
DEC_BATCH = 2
DEC_SEQ = 8192
PAST_LEN = 128

PLE_DIM = 256
GRID_W = 64
HEAD_DIM = 64
BRANCH_W = 256
N_BRANCH = 4
EPS = 1e-6
NEG_INF = -1e30

MLA_HEADS = 4
MLA_Q_RANK = 256
MLA_KV_RANK = 128
MLA_NOPE = 64
MLA_ROPE = 32
MLA_V = 64
ROPE_THETA = 10000.0
MLA_QBLOCK = 128

DIL_PAIRS = ((128, 1), (512, 4), (2048, 16))
DIL_SLOTS = 4
DIL_HEADS = DIL_SLOTS * len(DIL_PAIRS)
DIL_BLOCK = 64

SWA_Q_HEADS = 4
SWA_KV_HEADS = 2
SWA_WINDOW = 128
SWA_BLOCK = 128

NA_HEADS = 4
NA_ROWS_MAX = 8
NA_COLS = 16
NA_QCOLS = 16
NA_KCOLS = 32

REL_BUCKETS = 32
REL_MAX_DIST = 1024
REL_HEADS = DIL_HEADS + SWA_Q_HEADS

D_FF = 2816
CONV_W = 3

A_COLS = MLA_Q_RANK + MLA_KV_RANK + MLA_ROPE
B_COLS = 3 * DIL_HEADS * HEAD_DIM
C_COLS = (SWA_Q_HEADS + 2 * SWA_KV_HEADS) * HEAD_DIM
D_COLS = 3 * NA_HEADS * HEAD_DIM
IN_COLS = A_COLS + B_COLS + C_COLS + D_COLS

kernel_name = 'hybrid_parallel_encoder'


def rmsnorm(x, g):
    xf = x.astype(jnp.float32)
    y = xf * lax.rsqrt(jnp.mean(xf * xf, -1, keepdims=True) + EPS)
    return (y * g.astype(jnp.float32)).astype(x.dtype)


def t5_bucket(rel):
    nb = REL_BUCKETS // 2
    max_exact = nb // 2
    ret = jnp.where(rel > 0, nb, 0)
    n = jnp.abs(rel)
    nf = jnp.maximum(n, 1).astype(jnp.float32)
    large = max_exact + (jnp.log(nf / max_exact) / math.log(REL_MAX_DIST / max_exact) * (nb - max_exact)).astype(jnp.int32)
    large = jnp.minimum(large, nb - 1)
    return ret + jnp.where(n < max_exact, n, large)


def band_offsets(block):
    return np.arange(3 * block)[None, :] - block - np.arange(block)[:, None]


def banded_attention(q, k, v, bias, window, block, sink=None):
    n, L, hq, dh = q.shape
    hkv = k.shape[2]
    g = hq // hkv
    nb = -(-L // block)
    lp = nb * block
    pad = lp - L
    qb = jnp.pad(q, ((0, 0), (0, pad), (0, 0), (0, 0))).reshape(n, nb, block, hkv, g, dh)

    def key_window(t):
        tb = jnp.pad(t, ((0, 0), (block, block + pad), (0, 0), (0, 0))).reshape(n, nb + 2, block, hkv, dh)
        return jnp.concatenate([tb[:, :-2], tb[:, 1:-1], tb[:, 2:]], axis=2)

    kb = key_window(k)
    vb = key_window(v)
    s = jnp.einsum('nbqhgd,nbkhd->nbhgqk', qb, kb, preferred_element_type=jnp.float32) * (dh ** -0.5)
    s = s + bias.astype(jnp.float32).reshape(hkv, g, block, 3 * block)
    off = band_offsets(block)
    kpos = np.arange(nb)[:, None, None] * block + np.arange(block)[None, :, None] + off[None]
    valid = (np.abs(off)[None] <= window) & (kpos >= 0) & (kpos < L)
    s = jnp.where(valid[None, :, None, None], s, NEG_INF)
    m = jnp.max(s, -1, keepdims=True)
    lse = m + jnp.log(jnp.sum(jnp.exp(s - m), -1, keepdims=True))
    if sink is not None:
        lse = jnp.logaddexp(lse, sink.astype(jnp.float32).reshape(hkv, g, 1, 1))
    p = jnp.exp(s - lse)
    o = jnp.einsum('nbhgqk,nbkhd->nbqhgd', p.astype(v.dtype), vb).reshape(n, lp, hq, dh)[:, :L]
    lse = jnp.transpose(lse[..., 0], (0, 1, 4, 2, 3)).reshape(n, lp, hq)[:, :L]
    return o, lse


def rope_tables(seq):
    inv = ROPE_THETA ** (-jnp.arange(0, MLA_ROPE, 2, dtype=jnp.float32) / MLA_ROPE)
    ang = jnp.arange(seq, dtype=jnp.float32)[:, None] * inv[None]
    return jnp.cos(ang), jnp.sin(ang)


def apply_rope(x, cos, sin):
    x1, x2 = jnp.split(x, 2, -1)
    c = cos.astype(x.dtype)
    s = sin.astype(x.dtype)
    return jnp.concatenate([x1 * c - x2 * s, x1 * s + x2 * c], -1)


def mla_attention(za, q_norm, kv_norm, w_q_up, w_kv_up):
    bsz, seq = za.shape[:2]
    cq = rmsnorm(za[..., :MLA_Q_RANK], q_norm)
    ckv = rmsnorm(za[..., MLA_Q_RANK:MLA_Q_RANK + MLA_KV_RANK], kv_norm)
    k_rope = za[..., MLA_Q_RANK + MLA_KV_RANK:]
    q = (cq @ w_q_up).reshape(bsz, seq, MLA_HEADS, MLA_NOPE + MLA_ROPE)
    kv = (ckv @ w_kv_up).reshape(bsz, seq, MLA_HEADS, MLA_NOPE + MLA_V)
    cos, sin = rope_tables(seq)
    q = jnp.concatenate([q[..., :MLA_NOPE], apply_rope(q[..., MLA_NOPE:], cos[:, None], sin[:, None])], -1)
    k_rope = apply_rope(k_rope, cos, sin)
    k = jnp.concatenate([kv[..., :MLA_NOPE], jnp.broadcast_to(k_rope[:, :, None], (bsz, seq, MLA_HEADS, MLA_ROPE))], -1)
    v = kv[..., MLA_NOPE:]
    scale = (MLA_NOPE + MLA_ROPE) ** -0.5
    nq = seq // MLA_QBLOCK
    qb = jnp.moveaxis(q.reshape(bsz, nq, MLA_QBLOCK, MLA_HEADS, MLA_NOPE + MLA_ROPE), 1, 0)

    def query_block(qi):
        s = jnp.einsum('bqhd,bkhd->bhqk', qi, k, preferred_element_type=jnp.float32) * scale
        p = jax.nn.softmax(s, axis=-1)
        return jnp.einsum('bhqk,bkhd->bqhd', p.astype(v.dtype), v)

    o = lax.map(query_block, qb)
    return jnp.moveaxis(o, 0, 1).reshape(bsz, seq, MLA_HEADS * MLA_V)


def dilated_attention(zb, rel_table):
    bsz, seq = zb.shape[:2]
    t = zb.reshape(bsz, seq, len(DIL_PAIRS), 3, DIL_SLOTS, HEAD_DIM)
    off = band_offsets(DIL_BLOCK)
    outs = []
    lses = []
    for gi, (window, dil) in enumerate(DIL_PAIRS):
        half = window // (2 * dil)
        sub = seq // dil

        def to_sub(a):
            return a.reshape(bsz, sub, dil, DIL_SLOTS, HEAD_DIM).transpose(0, 2, 1, 3, 4).reshape(bsz * dil, sub, DIL_SLOTS, HEAD_DIM)

        q = to_sub(t[:, :, gi, 0])
        k = to_sub(t[:, :, gi, 1])
        v = to_sub(t[:, :, gi, 2])
        bias = rel_table[t5_bucket(jnp.asarray(off * dil))][..., gi * DIL_SLOTS:(gi + 1) * DIL_SLOTS]
        o, lse = banded_attention(q, k, v, jnp.moveaxis(bias, -1, 0), half, DIL_BLOCK)
        outs.append(o.reshape(bsz, dil, sub, DIL_SLOTS, HEAD_DIM).transpose(0, 2, 1, 3, 4).reshape(bsz, seq, DIL_SLOTS, HEAD_DIM))
        lses.append(lse.reshape(bsz, dil, sub, DIL_SLOTS).transpose(0, 2, 1, 3).reshape(bsz, seq, DIL_SLOTS))
    w = jax.nn.softmax(jnp.stack(lses, 0), axis=0)
    o = w[0][..., None].astype(outs[0].dtype) * outs[0]
    for gi in range(1, len(DIL_PAIRS)):
        o = o + w[gi][..., None].astype(outs[gi].dtype) * outs[gi]
    return o.reshape(bsz, seq, DIL_SLOTS * HEAD_DIM)


def window_gqa(zc, rel_table, sink):
    bsz, seq = zc.shape[:2]
    nq = SWA_Q_HEADS * HEAD_DIM
    nk = SWA_KV_HEADS * HEAD_DIM
    q = zc[..., :nq].reshape(bsz, seq, SWA_Q_HEADS, HEAD_DIM)
    k = zc[..., nq:nq + nk].reshape(bsz, seq, SWA_KV_HEADS, HEAD_DIM)
    v = zc[..., nq + nk:].reshape(bsz, seq, SWA_KV_HEADS, HEAD_DIM)
    bias = rel_table[t5_bucket(jnp.asarray(band_offsets(SWA_BLOCK)))][..., DIL_HEADS:]
    o, _ = banded_attention(q, k, v, jnp.moveaxis(bias, -1, 0), SWA_WINDOW, SWA_BLOCK, sink)
    return o.reshape(bsz, seq, SWA_Q_HEADS * HEAD_DIM)


def na_col_tables():
    ncb = GRID_W // NA_QCOLS
    starts = np.clip(np.arange(ncb) * NA_QCOLS - NA_COLS // 2, 0, GRID_W - NA_KCOLS)
    qc = np.arange(GRID_W).reshape(ncb, NA_QCOLS)
    sc = np.clip(qc - NA_COLS // 2, 0, GRID_W - NA_COLS)
    kc = starts[:, None] + np.arange(NA_KCOLS)
    valid = (kc[:, None, :] >= sc[..., None]) & (kc[:, None, :] < sc[..., None] + NA_COLS)
    dc = np.clip(kc[:, None, :] - qc[..., None], -(NA_COLS - 1), NA_COLS - 1)
    return kc, valid, dc


def neighborhood_attention(zd, rpb):
    bsz, seq = zd.shape[:2]
    rows = seq // GRID_W
    kr = min(NA_ROWS_MAX, rows)
    t = zd.reshape(bsz, rows, GRID_W, 3, NA_HEADS, HEAD_DIM)
    q, k, v = t[:, :, :, 0], t[:, :, :, 1], t[:, :, :, 2]
    col_idx, valid, dc = na_col_tables()
    ncb = GRID_W // NA_QCOLS
    bias_c = rpb[:, :, dc + NA_COLS - 1]
    mask = valid[:, :, None, :]
    scale = HEAD_DIM ** -0.5

    def grid_row(r):
        sr = jnp.clip(r - kr // 2, 0, rows - kr)
        kw = lax.dynamic_slice_in_dim(k, sr, kr, axis=1)[:, :, col_idx]
        vw = lax.dynamic_slice_in_dim(v, sr, kr, axis=1)[:, :, col_idx]
        qr = lax.dynamic_index_in_dim(q, r, axis=1, keepdims=False).reshape(bsz, ncb, NA_QCOLS, NA_HEADS, HEAD_DIM)
        s = jnp.einsum('bcqhd,bickhd->bhcqik', qr, kw, preferred_element_type=jnp.float32) * scale
        dr = sr + jnp.arange(kr) - r + NA_ROWS_MAX - 1
        s = s + jnp.moveaxis(bias_c[:, dr], 1, 3).astype(jnp.float32)[None]
        s = jnp.where(mask, s, NEG_INF)
        p = jax.nn.softmax(s, axis=(-2, -1))
        o = jnp.einsum('bhcqik,bickhd->bcqhd', p.astype(vw.dtype), vw)
        return o.reshape(bsz, GRID_W, NA_HEADS * HEAD_DIM)

    o = lax.map(grid_row, jnp.arange(rows))
    return jnp.moveaxis(o, 0, 1).reshape(bsz, seq, NA_HEADS * HEAD_DIM)


def conv_ffn(h, w_up, conv_w, conv_b, w_down):
    u = h @ w_up
    up = jnp.pad(u, ((0, 0), (1, 1), (0, 0)))
    u = up[:, :-2] * conv_w[0] + up[:, 1:-1] * conv_w[1] + up[:, 2:] * conv_w[2] + conv_b
    a, b = jnp.split(u, 2, -1)
    return (jax.nn.gelu(a) * b) @ w_down


def trunk(x, p, ln_attn, w_in, q_norm, kv_norm, w_q_up, w_kv_up, attn_sink, na_rpb, rel_table,
          w_gate, w_branch, w_out, ln_ffn, w_ffn_up, ffn_conv_w, ffn_conv_b, w_ffn_down,
          ln_ple, w_ple_gate, w_ple_proj, ln_final):
    cuts = [A_COLS, A_COLS + B_COLS, A_COLS + B_COLS + C_COLS]
    for i in range(DEPTH):
        h = rmsnorm(x, ln_attn[i])
        za, zb, zc, zd = jnp.split(h @ w_in[i], cuts, axis=-1)
        branches = (
            mla_attention(za, q_norm[i], kv_norm[i], w_q_up[i], w_kv_up[i]),
            dilated_attention(zb, rel_table),
            window_gqa(zc, rel_table, attn_sink[i]),
            neighborhood_attention(zd, na_rpb[i]),
        )
        merged = jnp.zeros_like(x)
        for j, o in enumerate(branches):
            merged = merged + jax.nn.sigmoid(h @ w_gate[i, j]) * (o @ w_branch[i, j])
        x = x + merged @ w_out[i]
        x = x + conv_ffn(rmsnorm(x, ln_ffn[i]), w_ffn_up[i], ffn_conv_w[i], ffn_conv_b[i], w_ffn_down[i])
        x = x + jax.nn.sigmoid(rmsnorm(x, ln_ple[i]) @ w_ple_gate[i]) * (p[i] @ w_ple_proj[i])
    return rmsnorm(x, ln_final)


def setup_inputs(seed: int = 0) -> dict:
    key = jax.random.key(seed)
    ks = iter(jax.random.split(key, 32))

    def nrm(shape, scale):
        return jax.random.normal(next(ks), shape, jnp.float32) * scale

    def gain(shape):
        return 1.0 + nrm(shape, 0.01)

    return {
        'x_prompt': nrm((BATCH, SEQ, D_MODEL), 1.0),
        'x_sample': nrm((DEC_BATCH, DEC_SEQ, D_MODEL), 1.0),
        'p_prompt': nrm((DEPTH, BATCH, SEQ, PLE_DIM), 1.0),
        'p_sample': nrm((DEPTH, DEC_BATCH, DEC_SEQ, PLE_DIM), 1.0),
        'ln_attn': gain((DEPTH, D_MODEL)),
        'w_in': nrm((DEPTH, D_MODEL, IN_COLS), D_MODEL ** -0.5),
        'q_norm': gain((DEPTH, MLA_Q_RANK)),
        'kv_norm': gain((DEPTH, MLA_KV_RANK)),
        'w_q_up': nrm((DEPTH, MLA_Q_RANK, MLA_HEADS * (MLA_NOPE + MLA_ROPE)), MLA_Q_RANK ** -0.5),
        'w_kv_up': nrm((DEPTH, MLA_KV_RANK, MLA_HEADS * (MLA_NOPE + MLA_V)), MLA_KV_RANK ** -0.5),
        'attn_sink': nrm((DEPTH, SWA_Q_HEADS), 1.0),
        'na_rpb': nrm((DEPTH, NA_HEADS, 2 * NA_ROWS_MAX - 1, 2 * NA_COLS - 1), 0.5),
        'rel_table': nrm((REL_BUCKETS, REL_HEADS), 0.5),
        'w_gate': nrm((DEPTH, N_BRANCH, D_MODEL, D_MODEL), D_MODEL ** -0.5),
        'w_branch': nrm((DEPTH, N_BRANCH, BRANCH_W, D_MODEL), BRANCH_W ** -0.5),
        'w_out': nrm((DEPTH, D_MODEL, D_MODEL), D_MODEL ** -0.5),
        'ln_ffn': gain((DEPTH, D_MODEL)),
        'w_ffn_up': nrm((DEPTH, D_MODEL, 2 * D_FF), D_MODEL ** -0.5),
        'ffn_conv_w': nrm((DEPTH, CONV_W, 2 * D_FF), CONV_W ** -0.5),
        'ffn_conv_b': nrm((DEPTH, 2 * D_FF), 0.01),
        'w_ffn_down': nrm((DEPTH, D_FF, D_MODEL), D_FF ** -0.5),
        'ln_ple': gain((DEPTH, D_MODEL)),
        'w_ple_gate': nrm((DEPTH, D_MODEL, D_MODEL), D_MODEL ** -0.5),
        'w_ple_proj': nrm((DEPTH, PLE_DIM, D_MODEL), PLE_DIM ** -0.5),
        'ln_final': gain((D_MODEL,)),
    }


def reference(x_prompt, x_sample, p_prompt, p_sample, ln_attn, w_in, q_norm, kv_norm, w_q_up, w_kv_up,
              attn_sink, na_rpb, rel_table, w_gate, w_branch, w_out, ln_ffn, w_ffn_up, ffn_conv_w,
              ffn_conv_b, w_ffn_down, ln_ple, w_ple_gate, w_ple_proj, ln_final):
    weights = (ln_attn, w_in, q_norm, kv_norm, w_q_up, w_kv_up, attn_sink, na_rpb, rel_table,
               w_gate, w_branch, w_out, ln_ffn, w_ffn_up, ffn_conv_w, ffn_conv_b, w_ffn_down,
               ln_ple, w_ple_gate, w_ple_proj, ln_final)
    y_prompt = trunk(x_prompt, p_prompt, *weights)
    y_sample = trunk(x_sample, p_sample, *weights)
    return (y_prompt, y_sample)
```

```python
import functools
import math

import jax
import jax.numpy as jnp
import numpy as np
from jax import lax
from jax.experimental import pallas as pl
from jax.experimental.pallas import tpu as pltpu

D_MODEL = 1024
DEPTH = 4
PLE_DIM = 256
GRID_W = 64
HEAD_DIM = 64
BRANCH_W = 256
N_BRANCH = 4
EPS = 1e-6
NEG_INF = -1e30

MLA_HEADS = 4
MLA_Q_RANK = 256
MLA_KV_RANK = 128
MLA_NOPE = 64
MLA_ROPE = 32
MLA_V = 64
ROPE_THETA = 10000.0
MLA_SLOT = 128

DIL_PAIRS = ((128, 1), (512, 4), (2048, 16))
DIL_SLOTS = 4
DIL_HEADS = DIL_SLOTS * len(DIL_PAIRS)
DIL_BLOCK = 64

SWA_Q_HEADS = 4
SWA_KV_HEADS = 2
SWA_WINDOW = 128
SWA_BLOCK = 128

NA_HEADS = 4
NA_ROWS = 8
NA_COLS = 16

REL_BUCKETS = 32
REL_MAX_DIST = 1024

D_FF = 2816
CONV_W = 3

A_COLS = MLA_Q_RANK + MLA_KV_RANK + MLA_ROPE
B_COLS = 3 * DIL_HEADS * HEAD_DIM
C_COLS = (SWA_Q_HEADS + 2 * SWA_KV_HEADS) * HEAD_DIM
D_COLS = 3 * NA_HEADS * HEAD_DIM
A_PAD = 512
C_EXP = 3 * BRANCH_W
MAIN_COLS = B_COLS + C_EXP + D_COLS
MAIN_CHUNK = 768

BF16 = jnp.bfloat16
F32 = jnp.float32

VMEM_LIMIT_BYTES = 56 * 1024 * 1024


def _params(*semantics):
    return pltpu.CompilerParams(dimension_semantics=semantics, vmem_limit_bytes=VMEM_LIMIT_BYTES)


def _resident(shape):
    return pl.BlockSpec(shape, lambda *_: (0,) * len(shape), pipeline_mode=pl.Buffered(1))


def _dot(a, b):
    return jnp.dot(a, b, preferred_element_type=F32)


def _dot_nt(a, b):
    return lax.dot_general(a, b, (((1,), (1,)), ((), ())), preferred_element_type=F32)


def _rms(x, g):
    return x * lax.rsqrt(jnp.mean(x * x, axis=-1, keepdims=True) + EPS) * g


def _attn_in_kernel(x_ref, g_ref, wmain_ref, wa_ref, qn_ref, kvn_ref, wq_ref, wqs_ref, wk_ref,
                    wv_ref, e_ref, es_ref, cos_ref, sin_ref,
                    qa_ref, ka_ref, va_ref, zb_ref, zc_ref, zd_ref):
    hb = _rms(x_ref[...], g_ref[...]).astype(BF16)
    outs = ((zb_ref, B_COLS), (zc_ref, C_EXP), (zd_ref, D_COLS))
    col = 0
    for ref, width in outs:
        for c in range(width // MAIN_CHUNK):
            ref[:, c * MAIN_CHUNK:(c + 1) * MAIN_CHUNK] = _dot(
                hb, wmain_ref[:, col:col + MAIN_CHUNK]).astype(BF16)
            col += MAIN_CHUNK

    za = _dot(hb, wa_ref[...])
    cq = _rms(za[:, :MLA_Q_RANK], qn_ref[...]).astype(BF16)
    ckv = _rms(za[:, MLA_Q_RANK:MLA_Q_RANK + MLA_KV_RANK], kvn_ref[...]).astype(BF16)
    kr = za[:, MLA_Q_RANK + MLA_KV_RANK:]
    kr_hi = kr.astype(BF16)
    kr_lo = (kr - kr_hi.astype(F32)).astype(BF16)
    cos = jnp.concatenate([cos_ref[...]] * MLA_HEADS, axis=1)
    sin = jnp.concatenate([sin_ref[...]] * MLA_HEADS, axis=1)
    q = _dot(cq, wq_ref[...]) * cos + _dot(cq, wqs_ref[...]) * sin
    qa_ref[...] = (q * ((MLA_NOPE + MLA_ROPE) ** -0.5)).astype(BF16)
    k_rope = _dot(kr_hi, e_ref[...]) + _dot(kr_lo, e_ref[...])
    k_swap = _dot(kr_hi, es_ref[...]) + _dot(kr_lo, es_ref[...])
    ka_ref[...] = ((_dot(ckv, wk_ref[...]) + k_rope) * cos + k_swap * sin).astype(BF16)
    va_ref[...] = _dot(ckv, wv_ref[...]).astype(BF16)


def _attn_in(x, w, cos_t, sin_t, tm):
    bsz, seq, _ = x.shape
    tok = lambda width: pl.BlockSpec((None, tm, width), lambda b, i: (b, i, 0))
    table = pl.BlockSpec((tm, MLA_SLOT), lambda b, i: (i, 0))
    out_widths = (MLA_HEADS * MLA_SLOT, MLA_HEADS * MLA_SLOT, MLA_HEADS * MLA_V, B_COLS, C_EXP, D_COLS)
    return pl.pallas_call(
        _attn_in_kernel,
        grid=(bsz, seq // tm),
        in_specs=[tok(D_MODEL), _resident((1, D_MODEL)), _resident((D_MODEL, MAIN_COLS)),
                  _resident((D_MODEL, A_PAD)), _resident((1, MLA_Q_RANK)), _resident((1, MLA_KV_RANK)),
                  _resident((MLA_Q_RANK, MLA_HEADS * MLA_SLOT)), _resident((MLA_Q_RANK, MLA_HEADS * MLA_SLOT)),
                  _resident((MLA_KV_RANK, MLA_HEADS * MLA_SLOT)), _resident((MLA_KV_RANK, MLA_HEADS * MLA_V)),
                  _resident((MLA_SLOT, MLA_HEADS * MLA_SLOT)), _resident((MLA_SLOT, MLA_HEADS * MLA_SLOT)),
                  table, table],
        out_specs=[tok(n) for n in out_widths],
        out_shape=[jax.ShapeDtypeStruct((bsz, seq, n), BF16) for n in out_widths],
        compiler_params=_params("parallel", "parallel"),
        name="attn_in",
    )(x, w["ln_attn"], w["w_main"], w["w_a"], w["q_norm"], w["kv_norm"], w["wq"], w["wq_s"],
      w["wk"], w["wv"], w["e"], w["e_s"], cos_t, sin_t)


def _mla_kernel(q_ref, k_ref, v_ref, o_ref, m_scr, l_scr, acc_scr):
    kv = pl.program_id(2)

    @pl.when(kv == 0)
    def _():
        m_scr[...] = jnp.full(m_scr.shape, NEG_INF, F32)
        l_scr[...] = jnp.zeros(l_scr.shape, F32)
        acc_scr[...] = jnp.zeros(acc_scr.shape, F32)

    for h in range(MLA_HEADS):
        q = q_ref[:, h * MLA_SLOT:(h + 1) * MLA_SLOT]
        k = k_ref[:, h * MLA_SLOT:(h + 1) * MLA_SLOT]
        pair = h // 2
        v = v_ref[:, pair * 2 * MLA_V:(pair + 1) * 2 * MLA_V]
        s = _dot_nt(q, k)
        m_prev = m_scr[h]
        m_new = jnp.maximum(m_prev, jnp.max(s, axis=1, keepdims=True))
        alpha = jnp.exp(m_prev - m_new)
        p = jnp.exp(s - m_new)
        l_scr[h] = alpha * l_scr[h] + jnp.sum(p, axis=1, keepdims=True)
        acc_scr[h] = alpha * acc_scr[h] + _dot(p.astype(BF16), v)
        m_scr[h] = m_new

    @pl.when(kv == pl.num_programs(2) - 1)
    def _():
        lane = lax.broadcasted_iota(jnp.int32, acc_scr.shape[1:], 1)
        for pair in range(MLA_HEADS // 2):
            lo = acc_scr[2 * pair] / l_scr[2 * pair]
            hi = acc_scr[2 * pair + 1] / l_scr[2 * pair + 1]
            o_ref[:, pair * 2 * MLA_V:(pair + 1) * 2 * MLA_V] = jnp.where(lane < MLA_V, lo, hi).astype(BF16)


def _mla(qa, ka, va, tq, tk):
    bsz, seq, _ = qa.shape
    return pl.pallas_call(
        _mla_kernel,
        grid=(bsz, seq // tq, seq // tk),
        in_specs=[pl.BlockSpec((None, tq, MLA_HEADS * MLA_SLOT), lambda b, i, j: (b, i, 0)),
                  pl.BlockSpec((None, tk, MLA_HEADS * MLA_SLOT), lambda b, i, j: (b, j, 0)),
                  pl.BlockSpec((None, tk, MLA_HEADS * MLA_V), lambda b, i, j: (b, j, 0))],
        out_specs=pl.BlockSpec((None, tq, MLA_HEADS * MLA_V), lambda b, i, j: (b, i, 0)),
        out_shape=jax.ShapeDtypeStruct((bsz, seq, MLA_HEADS * MLA_V), BF16),
        scratch_shapes=[pltpu.VMEM((MLA_HEADS, tq, 1), F32), pltpu.VMEM((MLA_HEADS, tq, 1), F32),
                        pltpu.VMEM((MLA_HEADS, tq, 2 * MLA_V), F32)],
        compiler_params=_params("parallel", "parallel", "arbitrary"),
        name="mla_attention",
    )(qa, ka, va)


def _head_rows(x, nheads):
    head = lax.broadcasted_iota(jnp.int32, x.shape, 1) // HEAD_DIM
    xf = x.astype(F32)
    return jnp.concatenate([jnp.where(head == h, xf, 0.0) for h in range(nheads)], axis=0).astype(x.dtype)


def _head_cols(x, nheads):
    n = x.shape[0] // nheads
    head = lax.broadcasted_iota(jnp.int32, (n, x.shape[1]), 1) // HEAD_DIM
    out = x[:n]
    for h in range(1, nheads):
        out = jnp.where(head == h, x[h * n:(h + 1) * n], out)
    return out


def _banded_kernel(*refs, blk, tm, seq_len, tile_axis, with_sink, with_lse):
    q_ref, kp_ref, km_ref, kn_ref, vp_ref, vm_ref, vn_ref, bias_ref = refs[:8]
    refs = refs[8:]
    sink_ref = None
    if with_sink:
        sink_ref, refs = refs[0], refs[1:]
    o_ref = refs[0]
    lse_ref = refs[1] if with_lse else None

    tile = pl.program_id(tile_axis)
    kcat = jnp.concatenate([kp_ref[...], km_ref[...], kn_ref[...]], axis=0)
    vcat = jnp.concatenate([vp_ref[...], vm_ref[...], vn_ref[...]], axis=0)
    nblk = tm // blk
    for jb in range(nblk):
        qbd = _head_rows(q_ref[jb * blk:(jb + 1) * blk, :], DIL_SLOTS)
        kw = kcat[jb * blk:(jb + 3) * blk]
        vw = vcat[jb * blk:(jb + 3) * blk]
        s = _dot_nt(qbd, kw) * (HEAD_DIM ** -0.5) + bias_ref[...]
        if jb == 0 or jb == nblk - 1:
            kpos = tile * tm + (jb - 1) * blk + lax.broadcasted_iota(jnp.int32, s.shape, 1)
            if jb == 0:
                s = jnp.where(kpos >= 0, s, NEG_INF)
            if jb == nblk - 1:
                s = jnp.where(kpos < seq_len, s, NEG_INF)
        m = jnp.max(s, axis=1, keepdims=True)
        if with_sink:
            m = jnp.maximum(m, sink_ref[...])
        p = jnp.exp(s - m)
        l = jnp.sum(p, axis=1, keepdims=True)
        if with_sink:
            l = l + jnp.exp(sink_ref[...] - m)
        obd = _dot(p.astype(BF16), vw) / l
        o_ref[jb * blk:(jb + 1) * blk, :] = _head_cols(obd, DIL_SLOTS).astype(BF16)
        if with_lse:
            lse = jnp.broadcast_to(m + jnp.log(l), obd.shape)
            lse_ref[jb * blk:(jb + 1) * blk, :] = _head_cols(lse, DIL_SLOTS)


def _banded(z, bias, sink, *, dil, col0, blk, tm, with_lse):
    bsz, seq, ncol = z.shape
    sub = seq // dil
    nblocks_per_pos = ncol // BRANCH_W
    zr = z.reshape(bsz, sub, dil * ncol)
    tm = min(tm, sub)
    per_tile = tm // blk
    last_blk = sub // blk - 1

    def main(j):
        return pl.BlockSpec((None, tm, BRANCH_W), lambda b, r, i: (b, i, r * nblocks_per_pos + col0 + j))

    def prev(j):
        return pl.BlockSpec((None, blk, BRANCH_W),
                            lambda b, r, i: (b, jnp.maximum(i * per_tile - 1, 0), r * nblocks_per_pos + col0 + j))

    def nxt(j):
        return pl.BlockSpec((None, blk, BRANCH_W),
                            lambda b, r, i: (b, jnp.minimum((i + 1) * per_tile, last_blk), r * nblocks_per_pos + col0 + j))

    in_specs = [main(0), prev(1), main(1), nxt(1), prev(2), main(2), nxt(2),
                _resident((DIL_SLOTS * blk, 3 * blk))]
    args = [zr, zr, zr, zr, zr, zr, zr, bias]
    if sink is not None:
        in_specs.append(_resident((DIL_SLOTS * blk, 1)))
        args.append(sink)
    out_spec = pl.BlockSpec((None, tm, BRANCH_W), lambda b, r, i: (b, i, r))
    out_specs = [out_spec]
    out_shape = [jax.ShapeDtypeStruct((bsz, sub, dil * BRANCH_W), BF16)]
    if with_lse:
        out_specs.append(out_spec)
        out_shape.append(jax.ShapeDtypeStruct((bsz, sub, dil * BRANCH_W), F32))
    outs = pl.pallas_call(
        functools.partial(_banded_kernel, blk=blk, tm=tm, seq_len=sub, tile_axis=2,
                          with_sink=sink is not None, with_lse=with_lse),
        grid=(bsz, dil, sub // tm),
        in_specs=in_specs, out_specs=out_specs, out_shape=out_shape,
        compiler_params=_params("parallel", "parallel", "parallel"),
        name=f"banded_d{dil}_b{blk}",
    )(*args)
    return [o.reshape(bsz, seq, BRANCH_W) for o in outs]


def _na_kernel(q_ref, k_ref, v_ref, bias_ref, o_ref, *, rows, rows_per_step):
    step = pl.program_id(1)
    for rr in range(rows_per_step):
        r = step * rows_per_step + rr
        sr = jnp.clip(r - NA_ROWS // 2, 0, rows - NA_ROWS)
        variant = sr - r + NA_ROWS - 1
        start = pl.multiple_of(sr * GRID_W, GRID_W)
        kw = k_ref[pl.ds(start, NA_ROWS * GRID_W), :]
        vw = v_ref[pl.ds(start, NA_ROWS * GRID_W), :]
        qbd = _head_rows(q_ref[rr * GRID_W:(rr + 1) * GRID_W, :], NA_HEADS)
        s = _dot_nt(qbd, kw) * (HEAD_DIM ** -0.5) + bias_ref[variant]
        m = jnp.max(s, axis=1, keepdims=True)
        p = jnp.exp(s - m)
        l = jnp.sum(p, axis=1, keepdims=True)
        obd = _dot(p.astype(BF16), vw) / l
        o_ref[rr * GRID_W:(rr + 1) * GRID_W, :] = _head_cols(obd, NA_HEADS).astype(BF16)


def _na(zd, bias, rows_per_step):
    bsz, seq, _ = zd.shape
    rows = seq // GRID_W
    assert rows >= NA_ROWS and rows % rows_per_step == 0
    tq = rows_per_step * GRID_W
    whole = lambda j: pl.BlockSpec((None, seq, BRANCH_W), lambda b, i: (b, 0, j))
    return pl.pallas_call(
        functools.partial(_na_kernel, rows=rows, rows_per_step=rows_per_step),
        grid=(bsz, rows // rows_per_step),
        in_specs=[pl.BlockSpec((None, tq, BRANCH_W), lambda b, i: (b, i, 0)), whole(1), whole(2),
                  _resident((NA_ROWS, NA_HEADS * GRID_W, NA_ROWS * GRID_W))],
        out_specs=pl.BlockSpec((None, tq, BRANCH_W), lambda b, i: (b, i, 0)),
        out_shape=jax.ShapeDtypeStruct((bsz, seq, BRANCH_W), BF16),
        compiler_params=_params("parallel", "arbitrary"),
        name="neighborhood_attention",
    )(zd, zd, zd, bias)


def _merge_kernel(x_ref, g_ref, oa_ref, ob0_ref, ob1_ref, ob2_ref, l0_ref, l1_ref, l2_ref,
                  oc_ref, od_ref, wg_ref, wb_ref, wo_ref, out_ref):
    x = x_ref[...]
    hb = _rms(x, g_ref[...]).astype(BF16)
    lses = (l0_ref[...], l1_ref[...], l2_ref[...])
    top = jnp.maximum(jnp.maximum(lses[0], lses[1]), lses[2])
    ws = [jnp.exp(l - top) for l in lses]
    den = ws[0] + ws[1] + ws[2]
    ob = sum((wgt / den) * o[...].astype(F32) for wgt, o in zip(ws, (ob0_ref, ob1_ref, ob2_ref)))
    branches = (oa_ref[...], ob.astype(BF16), oc_ref[...], od_ref[...])
    merged = None
    for j, o in enumerate(branches):
        term = jax.nn.sigmoid(_dot(hb, wg_ref[j])) * _dot(o, wb_ref[j])
        merged = term if merged is None else merged + term
    out_ref[...] = x + _dot(merged.astype(BF16), wo_ref[...])


def _merge(x, w, oa, obs, lses, oc, od, tm):
    bsz, seq, _ = x.shape
    tok = lambda width: pl.BlockSpec((None, tm, width), lambda b, i: (b, i, 0))
    return pl.pallas_call(
        _merge_kernel,
        grid=(bsz, seq // tm),
        in_specs=[tok(D_MODEL), _resident((1, D_MODEL))] + [tok(BRANCH_W)] * 9
                 + [_resident((N_BRANCH, D_MODEL, D_MODEL)), _resident((N_BRANCH, BRANCH_W, D_MODEL)),
                    _resident((D_MODEL, D_MODEL))],
        out_specs=tok(D_MODEL),
        out_shape=jax.ShapeDtypeStruct(x.shape, F32),
        compiler_params=_params("parallel", "parallel"),
        name="merge",
    )(x, w["ln_attn"], oa, *obs, *lses, oc, od, w["w_gate"], w["w_branch"], w["w_out"])


def _ffn_kernel(x_ref, xp_ref, xn_ref, g_ref, wua_ref, wub_ref, cwa_ref, cwb_ref, cba_ref, cbb_ref,
                wd_ref, out_ref, hn_scr, halo_scr, acc_scr, *, tm):
    tile = pl.program_id(1)
    chunk = pl.program_id(2)

    @pl.when(chunk == 0)
    def _():
        g = g_ref[...]
        hn_scr[...] = _rms(x_ref[...], g).astype(BF16)
        keep_prev = jnp.where(tile > 0, 1.0, 0.0)
        keep_next = jnp.where(tile < pl.num_programs(1) - 1, 1.0, 0.0)
        halo = jnp.concatenate([_rms(xp_ref[...], g) * keep_prev, _rms(xn_ref[...], g) * keep_next], axis=0)
        halo_scr[...] = halo.astype(BF16)
        acc_scr[...] = jnp.zeros(acc_scr.shape, F32)

    hn = hn_scr[...]
    halo = halo_scr[...]

    def conv_part(wu_ref, cw_ref, cb_ref):
        wu = wu_ref[...]
        u = _dot(hn, wu)
        uh = _dot(halo, wu)
        row = lax.broadcasted_iota(jnp.int32, u.shape, 0)
        below = jnp.where(row == 0, uh[7:8], pltpu.roll(u, 1, 0))
        above = jnp.where(row == tm - 1, uh[8:9], pltpu.roll(u, tm - 1, 0))
        cw = cw_ref[...]
        return below * cw[0:1] + u * cw[1:2] + above * cw[2:3] + cb_ref[...]

    a = conv_part(wua_ref, cwa_ref, cba_ref)
    b = conv_part(wub_ref, cwb_ref, cbb_ref)
    acc_scr[...] += _dot((jax.nn.gelu(a) * b).astype(BF16), wd_ref[...])

    @pl.when(chunk == pl.num_programs(2) - 1)
    def _():
        out_ref[...] = x_ref[...] + acc_scr[...]


def _ffn(x, w, tm, fc):
    bsz, seq, _ = x.shape
    nchunk = D_FF // fc
    halo_rows = 8
    per_tile = tm // halo_rows
    last = seq // halo_rows - 1
    tok = pl.BlockSpec((None, tm, D_MODEL), lambda b, i, c: (b, i, 0))
    return pl.pallas_call(
        functools.partial(_ffn_kernel, tm=tm),
        grid=(bsz, seq // tm, nchunk),
        in_specs=[tok,
                  pl.BlockSpec((None, halo_rows, D_MODEL), lambda b, i, c: (b, jnp.maximum(i * per_tile - 1, 0), 0)),
                  pl.BlockSpec((None, halo_rows, D_MODEL), lambda b, i, c: (b, jnp.minimum((i + 1) * per_tile, last), 0)),
                  _resident((1, D_MODEL)),
                  pl.BlockSpec((D_MODEL, fc), lambda b, i, c: (0, c)),
                  pl.BlockSpec((D_MODEL, fc), lambda b, i, c: (0, nchunk + c)),
                  pl.BlockSpec((CONV_W, fc), lambda b, i, c: (0, c)),
                  pl.BlockSpec((CONV_W, fc), lambda b, i, c: (0, nchunk + c)),
                  pl.BlockSpec((1, fc), lambda b, i, c: (0, c)),
                  pl.BlockSpec((1, fc), lambda b, i, c: (0, nchunk + c)),
                  pl.BlockSpec((fc, D_MODEL), lambda b, i, c: (c, 0))],
        out_specs=tok,
        out_shape=jax.ShapeDtypeStruct(x.shape, F32),
        scratch_shapes=[pltpu.VMEM((tm, D_MODEL), BF16), pltpu.VMEM((2 * halo_rows, D_MODEL), BF16),
                        pltpu.VMEM((tm, D_MODEL), F32)],
        compiler_params=_params("parallel", "parallel", "arbitrary"),
        name="conv_ffn",
    )(x, x, x, w["ln_ffn"], w["w_ffn_up"], w["w_ffn_up"], w["ffn_conv_w"], w["ffn_conv_w"],
      w["ffn_conv_b"], w["ffn_conv_b"], w["w_ffn_down"])


def _ple_kernel(x_ref, p_ref, g_ref, wg_ref, wp_ref, gf_ref, out_ref, *, final):
    x = x_ref[...]
    gate = jax.nn.sigmoid(_dot(_rms(x, g_ref[...]).astype(BF16), wg_ref[...]))
    y = x + gate * _dot(p_ref[...].astype(BF16), wp_ref[...])
    if final:
        y = _rms(y, gf_ref[...])
    out_ref[...] = y


def _ple(x, p, w, ln_final, tm, final):
    bsz, seq, _ = x.shape
    tok = lambda width: pl.BlockSpec((None, tm, width), lambda b, i: (b, i, 0))
    return pl.pallas_call(
        functools.partial(_ple_kernel, final=final),
        grid=(bsz, seq // tm),
        in_specs=[tok(D_MODEL), tok(PLE_DIM), _resident((1, D_MODEL)), _resident((D_MODEL, D_MODEL)),
                  _resident((PLE_DIM, D_MODEL)), _resident((1, D_MODEL))],
        out_specs=tok(D_MODEL),
        out_shape=jax.ShapeDtypeStruct(x.shape, F32),
        compiler_params=_params("parallel", "parallel"),
        name="ple_final" if final else "ple",
    )(x, p, w["ln_ple"], w["w_ple_gate"], w["w_ple_proj"], ln_final)


def _gather_cols(wmat, cols, sign=None):
    cols = np.asarray(cols)
    picked = jnp.take(wmat, jnp.asarray(np.maximum(cols, 0)), axis=-1)
    scale = (cols >= 0).astype(np.float32) * (1.0 if sign is None else np.asarray(sign, np.float32))
    return picked * jnp.asarray(scale)


def _prep_weights(ln_attn, w_in, q_norm, kv_norm, w_q_up, w_kv_up, w_gate, w_branch, w_out, ln_ffn,
                  w_ffn_up, ffn_conv_w, ffn_conv_b, w_ffn_down, ln_ple, w_ple_gate, w_ple_proj):
    half = MLA_ROPE // 2
    qk_dim = MLA_NOPE + MLA_ROPE
    slot_w = MLA_HEADS * MLA_SLOT

    c0 = A_COLS + B_COLS
    rep = np.repeat(np.arange(SWA_KV_HEADS), SWA_Q_HEADS // SWA_KV_HEADS)[:, None] * HEAD_DIM + np.arange(HEAD_DIM)
    nq = SWA_Q_HEADS * HEAD_DIM
    nk = SWA_KV_HEADS * HEAD_DIM
    main_cols = np.concatenate([np.arange(A_COLS, c0), c0 + np.arange(nq), c0 + nq + rep.reshape(-1),
                                c0 + nq + nk + rep.reshape(-1), np.arange(c0 + C_COLS, c0 + C_COLS + D_COLS)])
    a_cols = np.concatenate([np.arange(A_COLS), np.full((A_PAD - A_COLS,), -1)])

    q_cols = np.full((slot_w,), -1)
    qs_cols = np.full((slot_w,), -1)
    qs_sign = np.ones((slot_w,), np.float32)
    k_cols = np.full((slot_w,), -1)
    e_rows = np.full((slot_w,), -1)
    es_rows = np.full((slot_w,), -1)
    es_sign = np.ones((slot_w,), np.float32)
    for h in range(MLA_HEADS):
        base = h * MLA_SLOT
        q_cols[base:base + qk_dim] = h * qk_dim + np.arange(qk_dim)
        k_cols[base:base + MLA_NOPE] = h * (MLA_NOPE + MLA_V) + np.arange(MLA_NOPE)
        rope0 = base + MLA_NOPE
        qs_cols[rope0:rope0 + half] = h * qk_dim + MLA_NOPE + half + np.arange(half)
        qs_sign[rope0:rope0 + half] = -1.0
        qs_cols[rope0 + half:rope0 + MLA_ROPE] = h * qk_dim + MLA_NOPE + np.arange(half)
        e_rows[rope0:rope0 + MLA_ROPE] = np.arange(MLA_ROPE)
        es_rows[rope0:rope0 + half] = half + np.arange(half)
        es_sign[rope0:rope0 + half] = -1.0
        es_rows[rope0 + half:rope0 + MLA_ROPE] = np.arange(half)
    v_cols = (np.arange(MLA_HEADS)[:, None] * (MLA_NOPE + MLA_V) + MLA_NOPE + np.arange(MLA_V)).reshape(-1)
    eye = jnp.eye(MLA_SLOT, dtype=F32)

    row = lambda a: a[:, None, :]
    stacked = {
        "ln_attn": row(ln_attn), "q_norm": row(q_norm), "kv_norm": row(kv_norm),
        "w_main": _gather_cols(w_in, main_cols).astype(BF16),
        "w_a": _gather_cols(w_in, a_cols).astype(BF16),
        "wq": _gather_cols(w_q_up, q_cols).astype(BF16),
        "wq_s": _gather_cols(w_q_up, qs_cols, qs_sign).astype(BF16),
        "wk": _gather_cols(w_kv_up, k_cols).astype(BF16),
        "wv": _gather_cols(w_kv_up, v_cols).astype(BF16),
        "w_gate": w_gate.astype(BF16), "w_branch": w_branch.astype(BF16), "w_out": w_out.astype(BF16),
        "ln_ffn": row(ln_ffn), "w_ffn_up": w_ffn_up.astype(BF16), "ffn_conv_w": ffn_conv_w,
        "ffn_conv_b": row(ffn_conv_b), "w_ffn_down": w_ffn_down.astype(BF16),
        "ln_ple": row(ln_ple), "w_ple_gate": w_ple_gate.astype(BF16), "w_ple_proj": w_ple_proj.astype(BF16),
    }
    shared = {
        "e": _gather_cols(eye, e_rows).astype(BF16),
        "e_s": _gather_cols(eye, es_rows, es_sign).astype(BF16),
    }
    return [dict({k: v[i] for k, v in stacked.items()}, **shared) for i in range(DEPTH)]


def _rope_tables(seq):
    inv = ROPE_THETA ** (-jnp.arange(0, MLA_ROPE, 2, dtype=F32) / MLA_ROPE)
    ang = jnp.arange(seq, dtype=F32)[:, None] * inv[None]
    cos, sin = jnp.cos(ang), jnp.sin(ang)
    pad = MLA_SLOT - MLA_NOPE - MLA_ROPE
    cos_t = jnp.concatenate([jnp.ones((seq, MLA_NOPE), F32), cos, cos, jnp.zeros((seq, pad), F32)], axis=1)
    sin_t = jnp.concatenate([jnp.zeros((seq, MLA_NOPE), F32), sin, sin, jnp.zeros((seq, pad), F32)], axis=1)
    return cos_t, sin_t


def _t5_bucket(rel):
    nb = REL_BUCKETS // 2
    max_exact = nb // 2
    ret = jnp.where(rel > 0, nb, 0)
    n = jnp.abs(rel)
    nf = jnp.maximum(n, 1).astype(F32)
    large = max_exact + (jnp.log(nf / max_exact) / math.log(REL_MAX_DIST / max_exact) * (nb - max_exact)).astype(jnp.int32)
    large = jnp.minimum(large, nb - 1)
    return ret + jnp.where(n < max_exact, n, large)


def _band_bias(rel_table, block, window, dil, head0):
    off = np.arange(3 * block)[None, :] - block - np.arange(block)[:, None]
    bias = rel_table[_t5_bucket(jnp.asarray(off * dil))][..., head0:head0 + DIL_SLOTS]
    bias = jnp.where(jnp.asarray(np.abs(off) <= window)[..., None], bias.astype(F32), NEG_INF)
    return jnp.moveaxis(bias, -1, 0).reshape(DIL_SLOTS * block, 3 * block)


def _na_bias(rpb):
    qc = np.arange(GRID_W)[:, None]
    kc = np.arange(GRID_W)[None, :]
    sc = np.clip(qc - NA_COLS // 2, 0, GRID_W - NA_COLS)
    valid = (kc >= sc) & (kc < sc + NA_COLS)
    dc = np.clip(kc - qc, -(NA_COLS - 1), NA_COLS - 1) + NA_COLS - 1
    table = rpb.astype(F32)[:, :, jnp.asarray(dc)]
    table = jnp.where(jnp.asarray(valid)[None, None], table, NEG_INF)
    variants = [jnp.transpose(table[:, v:v + NA_ROWS], (0, 2, 1, 3)).reshape(NA_HEADS * GRID_W, NA_ROWS * GRID_W)
                for v in range(NA_ROWS)]
    return jnp.stack(variants)


def _trunk(x, p, weights, attn_sink, na_rpb, rel_table, ln_final):
    bsz, seq, _ = x.shape
    cos_t, sin_t = _rope_tables(seq)
    dil_bias = [_band_bias(rel_table, DIL_BLOCK, window // (2 * dil), dil, gi * DIL_SLOTS)
                for gi, (window, dil) in enumerate(DIL_PAIRS)]
    swa_bias = _band_bias(rel_table, SWA_BLOCK, SWA_WINDOW, 1, DIL_HEADS)
    ln_final = ln_final[None, :]
    tm = 512
    for i in range(DEPTH):
        w = weights[i]
        qa, ka, va, zb, zc, zd = _attn_in(x, w, cos_t, sin_t, tm)
        oa = _mla(qa, ka, va, 512, 1024)
        obs, lses = [], []
        for gi, (_, dil) in enumerate(DIL_PAIRS):
            o, lse = _banded(zb, dil_bias[gi], None, dil=dil, col0=3 * gi, blk=DIL_BLOCK, tm=512, with_lse=True)
            obs.append(o)
            lses.append(lse)
        sink = jnp.repeat(attn_sink[i].astype(F32), SWA_BLOCK)[:, None]
        (oc,) = _banded(zc, swa_bias, sink, dil=1, col0=0, blk=SWA_BLOCK, tm=512, with_lse=False)
        od = _na(zd, _na_bias(na_rpb[i]), rows_per_step=4)
        x = _merge(x, w, oa, obs, lses, oc, od, tm)
        x = _ffn(x, w, tm, 256)
        x = _ple(x, p[i], w, ln_final, tm, final=(i == DEPTH - 1))
    return x


def kernel(x_prompt, x_sample, p_prompt, p_sample, ln_attn, w_in, q_norm, kv_norm, w_q_up, w_kv_up,
           attn_sink, na_rpb, rel_table, w_gate, w_branch, w_out, ln_ffn, w_ffn_up, ffn_conv_w,
           ffn_conv_b, w_ffn_down, ln_ple, w_ple_gate, w_ple_proj, ln_final):
    weights = _prep_weights(ln_attn, w_in, q_norm, kv_norm, w_q_up, w_kv_up, w_gate, w_branch, w_out,
                            ln_ffn, w_ffn_up, ffn_conv_w, ffn_conv_b, w_ffn_down, ln_ple, w_ple_gate,
                            w_ple_proj)
    nprompt = x_prompt.shape[0]
    x = jnp.concatenate([x_prompt, x_sample], axis=0)
    p = jnp.concatenate([p_prompt, p_sample], axis=1)
    y = _trunk(x, p, weights, attn_sink, na_rpb, rel_table, ln_final)
    return (y[:nprompt], y[nprompt:])
```

```python
import functools
import math

import jax
import jax.numpy as jnp
import numpy as np
from jax import lax
from jax.experimental import pallas as pl
from jax.experimental.pallas import tpu as pltpu

D_MODEL = 1024
DEPTH = 4
PLE_DIM = 256
GRID_W = 64
HEAD_DIM = 64
BRANCH_W = 256
N_BRANCH = 4
EPS = 1e-6
NEG_INF = -1e30

MLA_HEADS = 4
MLA_Q_RANK = 256
MLA_KV_RANK = 128
MLA_NOPE = 64
MLA_ROPE = 32
MLA_V = 64
ROPE_THETA = 10000.0
MLA_SLOT = 128
MLA_VROWS = 96
LANES = 128
LOG2_E = math.log2(math.e)

DIL_PAIRS = ((128, 1), (512, 4), (2048, 16))
DIL_SLOTS = 4
DIL_HEADS = DIL_SLOTS * len(DIL_PAIRS)
DIL_BLOCK = 64

SWA_Q_HEADS = 4
SWA_KV_HEADS = 2
SWA_WINDOW = 128
SWA_BLOCK = 128

NA_HEADS = 4
NA_ROWS = 8
NA_COLS = 16

REL_BUCKETS = 32
REL_MAX_DIST = 1024

D_FF = 2816
CONV_W = 3

A_COLS = MLA_Q_RANK + MLA_KV_RANK + MLA_ROPE
B_COLS = 3 * DIL_HEADS * HEAD_DIM
C_COLS = (SWA_Q_HEADS + 2 * SWA_KV_HEADS) * HEAD_DIM
D_COLS = 3 * NA_HEADS * HEAD_DIM
A_PAD = 512
C_EXP = 3 * BRANCH_W
MAIN_COLS = B_COLS + C_EXP + D_COLS
MAIN_CHUNK = 768

BF16 = jnp.bfloat16
F32 = jnp.float32

VMEM_LIMIT_BYTES = 56 * 1024 * 1024


def _params(*semantics):
    return pltpu.CompilerParams(dimension_semantics=semantics, vmem_limit_bytes=VMEM_LIMIT_BYTES)


def _resident(shape):
    return pl.BlockSpec(shape, lambda *_: (0,) * len(shape), pipeline_mode=pl.Buffered(1))


def _dot(a, b):
    return jnp.dot(a, b, preferred_element_type=F32)


def _dot_nt(a, b):
    return lax.dot_general(a, b, (((1,), (1,)), ((), ())), preferred_element_type=F32)


def _rms(x, g):
    return x * lax.rsqrt(jnp.mean(x * x, axis=-1, keepdims=True) + EPS) * g


def _split_residues(val, scr, out_ref, dil):
    tm = val.shape[0]
    nslab = val.shape[1] // LANES
    for j in range(nslab):
        scr[j] = val[:, j * LANES:(j + 1) * LANES]
    for r in range(dil):
        for j in range(nslab):
            out_ref[r, :, j * LANES:(j + 1) * LANES] = scr[j, pl.ds(r, tm // dil, stride=dil), :].astype(BF16)


def _attn_in_kernel(x_ref, g_ref, wmain_ref, wa_ref, qn_ref, kvn_ref, wq_ref, wqs_ref, wk_ref,
                    wvt_ref, e_ref, es_ref, cos_ref, sin_ref,
                    qa_ref, ka_ref, vt_ref, zb0_ref, zb1_ref, zb2_ref, zc_ref, zd_ref, scr1, scr2):
    hb = _rms(x_ref[...], g_ref[...]).astype(BF16)
    chunk = lambda c: _dot(hb, wmain_ref[:, c * MAIN_CHUNK:(c + 1) * MAIN_CHUNK])
    zb0_ref[...] = chunk(0).astype(BF16)
    _split_residues(chunk(1), scr1, zb1_ref, DIL_PAIRS[1][1])
    _split_residues(chunk(2), scr2, zb2_ref, DIL_PAIRS[2][1])
    zc_ref[...] = chunk(3).astype(BF16)
    zd_ref[...] = chunk(4).astype(BF16)

    za = _dot(hb, wa_ref[...])
    cq = _rms(za[:, :MLA_Q_RANK], qn_ref[...]).astype(BF16)
    ckv = _rms(za[:, MLA_Q_RANK:MLA_Q_RANK + MLA_KV_RANK], kvn_ref[...]).astype(BF16)
    kr = za[:, MLA_Q_RANK + MLA_KV_RANK:]
    kr_hi = kr.astype(BF16)
    kr_lo = (kr - kr_hi.astype(F32)).astype(BF16)
    cos = jnp.concatenate([cos_ref[...]] * MLA_HEADS, axis=1)
    sin = jnp.concatenate([sin_ref[...]] * MLA_HEADS, axis=1)
    q = _dot(cq, wq_ref[...]) * cos + _dot(cq, wqs_ref[...]) * sin
    qa_ref[...] = (q * (LOG2_E * (MLA_NOPE + MLA_ROPE) ** -0.5)).astype(BF16)
    k_rope = _dot(kr_hi, e_ref[...]) + _dot(kr_lo, e_ref[...])
    k_swap = _dot(kr_hi, es_ref[...]) + _dot(kr_lo, es_ref[...])
    ka_ref[...] = ((_dot(ckv, wk_ref[...]) + k_rope) * cos + k_swap * sin).astype(BF16)
    vt = _dot_nt(wvt_ref[...], ckv)
    row = lax.broadcasted_iota(jnp.int32, vt.shape, 0)
    vt_ref[...] = jnp.where(row % MLA_VROWS == MLA_V, 1.0, vt).astype(BF16)


def _attn_in(x, w, cos_t, sin_t, tm):
    bsz, seq, _ = x.shape
    tok = lambda width: pl.BlockSpec((None, tm, width), lambda b, i: (b, i, 0))
    table = pl.BlockSpec((tm, MLA_SLOT), lambda b, i: (i, 0))
    group = 3 * BRANCH_W
    slot_w = MLA_HEADS * MLA_SLOT
    out_specs = [tok(slot_w), tok(slot_w),
                 pl.BlockSpec((None, MLA_HEADS * MLA_VROWS, tm), lambda b, i: (b, 0, i)), tok(group)]
    out_shape = [jax.ShapeDtypeStruct((bsz, seq, slot_w), BF16), jax.ShapeDtypeStruct((bsz, seq, slot_w), BF16),
                 jax.ShapeDtypeStruct((bsz, MLA_HEADS * MLA_VROWS, seq), BF16),
                 jax.ShapeDtypeStruct((bsz, seq, group), BF16)]
    scratch = []
    for _, dil in DIL_PAIRS[1:]:
        out_specs.append(pl.BlockSpec((None, dil, tm // dil, group), lambda b, i: (b, 0, i, 0)))
        out_shape.append(jax.ShapeDtypeStruct((bsz, dil, seq // dil, group), BF16))
        scratch.append(pltpu.VMEM((group // LANES, tm, LANES), F32))
    out_specs += [tok(C_EXP), tok(D_COLS)]
    out_shape += [jax.ShapeDtypeStruct((bsz, seq, C_EXP), BF16), jax.ShapeDtypeStruct((bsz, seq, D_COLS), BF16)]
    return pl.pallas_call(
        _attn_in_kernel,
        grid=(bsz, seq // tm),
        in_specs=[tok(D_MODEL), _resident((1, D_MODEL)), _resident((D_MODEL, MAIN_COLS)),
                  _resident((D_MODEL, A_PAD)), _resident((1, MLA_Q_RANK)), _resident((1, MLA_KV_RANK)),
                  _resident((MLA_Q_RANK, slot_w)), _resident((MLA_Q_RANK, slot_w)),
                  _resident((MLA_KV_RANK, slot_w)), _resident((MLA_HEADS * MLA_VROWS, MLA_KV_RANK)),
                  _resident((MLA_SLOT, slot_w)), _resident((MLA_SLOT, slot_w)),
                  table, table],
        out_specs=out_specs, out_shape=out_shape, scratch_shapes=scratch,
        compiler_params=_params("parallel", "parallel"),
        name="attn_in",
    )(x, w["ln_attn"], w["w_main"], w["w_a"], w["q_norm"], w["kv_norm"], w["wq"], w["wq_s"],
      w["wk"], w["wv_t"], w["e"], w["e_s"], cos_t, sin_t)


def _mla_kernel(q_ref, k_ref, vt_ref, o_ref, m_scr, acc_scr):
    kv = pl.program_id(2)

    @pl.when(kv == 0)
    def _():
        m_scr[...] = jnp.full(m_scr.shape, NEG_INF, F32)
        acc_scr[...] = jnp.zeros(acc_scr.shape, F32)

    def scores(h):
        return _dot_nt(k_ref[:, h * MLA_SLOT:(h + 1) * MLA_SLOT], q_ref[:, h * MLA_SLOT:(h + 1) * MLA_SLOT])

    s_next = scores(0)
    for h in range(MLA_HEADS):
        s = s_next
        if h + 1 < MLA_HEADS:
            s_next = scores(h + 1)
        m_prev = m_scr[h]
        m_new = jnp.maximum(m_prev, jnp.max(s, axis=0, keepdims=True))
        p = jnp.exp2(s - m_new).astype(BF16)
        acc_scr[h] = jnp.exp2(m_prev - m_new) * acc_scr[h] + _dot(vt_ref[h * MLA_VROWS:(h + 1) * MLA_VROWS, :], p)
        m_scr[h] = m_new

    @pl.when(kv == pl.num_programs(2) - 1)
    def _():
        o_t = jnp.concatenate([acc_scr[h, :MLA_V, :] / acc_scr[h, MLA_V:MLA_V + 1, :] for h in range(MLA_HEADS)],
                              axis=0)
        o_ref[...] = o_t.T.astype(BF16)


def _mla(qa, ka, vt, tq, tk):
    bsz, seq, _ = qa.shape
    tq, tk = min(tq, seq), min(tk, seq)
    return pl.pallas_call(
        _mla_kernel,
        grid=(bsz, seq // tq, seq // tk),
        in_specs=[pl.BlockSpec((None, tq, MLA_HEADS * MLA_SLOT), lambda b, i, j: (b, i, 0)),
                  pl.BlockSpec((None, tk, MLA_HEADS * MLA_SLOT), lambda b, i, j: (b, j, 0)),
                  pl.BlockSpec((None, MLA_HEADS * MLA_VROWS, tk), lambda b, i, j: (b, 0, j))],
        out_specs=pl.BlockSpec((None, tq, MLA_HEADS * MLA_V), lambda b, i, j: (b, i, 0)),
        out_shape=jax.ShapeDtypeStruct((bsz, seq, MLA_HEADS * MLA_V), BF16),
        scratch_shapes=[pltpu.VMEM((MLA_HEADS, 1, tq), F32), pltpu.VMEM((MLA_HEADS, MLA_VROWS, tq), F32)],
        compiler_params=_params("parallel", "parallel", "arbitrary"),
        name="mla_attention",
    )(qa, ka, vt)


def _head_rows(x, nheads):
    head = lax.broadcasted_iota(jnp.int32, x.shape, 1) // HEAD_DIM
    xf = x.astype(F32)
    return jnp.concatenate([jnp.where(head == h, xf, 0.0) for h in range(nheads)], axis=0).astype(x.dtype)


def _head_cols(x, nheads):
    n = x.shape[0] // nheads
    head = lax.broadcasted_iota(jnp.int32, (n, x.shape[1]), 1) // HEAD_DIM
    out = x[:n]
    for h in range(1, nheads):
        out = jnp.where(head == h, x[h * n:(h + 1) * n], out)
    return out


def _banded_kernel(*refs, blk, tm, seq_len, tile_axis, with_sink, with_lse):
    q_ref, kp_ref, km_ref, kn_ref, vp_ref, vm_ref, vn_ref, bias_ref = refs[:8]
    refs = refs[8:]
    sink_ref = None
    if with_sink:
        sink_ref, refs = refs[0], refs[1:]
    o_ref = refs[0]
    lse_ref = refs[1] if with_lse else None

    tile = pl.program_id(tile_axis)
    kcat = jnp.concatenate([kp_ref[...], km_ref[...], kn_ref[...]], axis=0)
    vcat = jnp.concatenate([vp_ref[...], vm_ref[...], vn_ref[...]], axis=0)
    nblk = tm // blk
    for jb in range(nblk):
        qbd = _head_rows(q_ref[jb * blk:(jb + 1) * blk, :], DIL_SLOTS)
        kw = kcat[jb * blk:(jb + 3) * blk]
        vw = vcat[jb * blk:(jb + 3) * blk]
        s = _dot_nt(qbd, kw) * (HEAD_DIM ** -0.5) + bias_ref[...]
        if jb == 0 or jb == nblk - 1:
            kpos = tile * tm + (jb - 1) * blk + lax.broadcasted_iota(jnp.int32, s.shape, 1)
            if jb == 0:
                s = jnp.where(kpos >= 0, s, NEG_INF)
            if jb == nblk - 1:
                s = jnp.where(kpos < seq_len, s, NEG_INF)
        m = jnp.max(s, axis=1, keepdims=True)
        if with_sink:
            m = jnp.maximum(m, sink_ref[...])
        p = jnp.exp(s - m)
        l = jnp.sum(p, axis=1, keepdims=True)
        if with_sink:
            l = l + jnp.exp(sink_ref[...] - m)
        obd = _dot(p.astype(BF16), vw) / l
        o_ref[jb * blk:(jb + 1) * blk, :] = _head_cols(obd, DIL_SLOTS).astype(BF16)
        if with_lse:
            lse = jnp.broadcast_to(m + jnp.log(l), obd.shape)
            lse_ref[jb * blk:(jb + 1) * blk, :] = _head_cols(lse, DIL_SLOTS)


def _banded(z, bias, sink, *, blk, tm, with_lse):
    bsz, dil, sub, _ = z.shape
    tm = min(tm, sub)
    per_tile = tm // blk
    last_blk = sub // blk - 1

    def main(j):
        return pl.BlockSpec((None, None, tm, BRANCH_W), lambda b, r, i: (b, r, i, j))

    def prev(j):
        return pl.BlockSpec((None, None, blk, BRANCH_W),
                            lambda b, r, i: (b, r, jnp.maximum(i * per_tile - 1, 0), j))

    def nxt(j):
        return pl.BlockSpec((None, None, blk, BRANCH_W),
                            lambda b, r, i: (b, r, jnp.minimum((i + 1) * per_tile, last_blk), j))

    in_specs = [main(0), prev(1), main(1), nxt(1), prev(2), main(2), nxt(2),
                _resident((DIL_SLOTS * blk, 3 * blk))]
    args = [z, z, z, z, z, z, z, bias]
    if sink is not None:
        in_specs.append(_resident((DIL_SLOTS * blk, 1)))
        args.append(sink)
    out_spec = pl.BlockSpec((None, None, tm, BRANCH_W), lambda b, r, i: (b, r, i, 0))
    out_specs = [out_spec]
    out_shape = [jax.ShapeDtypeStruct((bsz, dil, sub, BRANCH_W), BF16)]
    if with_lse:
        out_specs.append(out_spec)
        out_shape.append(jax.ShapeDtypeStruct((bsz, dil, sub, BRANCH_W), F32))
    return pl.pallas_call(
        functools.partial(_banded_kernel, blk=blk, tm=tm, seq_len=sub, tile_axis=2,
                          with_sink=sink is not None, with_lse=with_lse),
        grid=(bsz, dil, sub // tm),
        in_specs=in_specs, out_specs=out_specs, out_shape=out_shape,
        compiler_params=_params("parallel", "parallel", "parallel"),
        name=f"banded_d{dil}_b{blk}",
    )(*args)


def _na_kernel(q_ref, k_ref, v_ref, bias_ref, o_ref, *, rows, rows_per_step):
    step = pl.program_id(1)
    for rr in range(rows_per_step):
        r = step * rows_per_step + rr
        sr = jnp.clip(r - NA_ROWS // 2, 0, rows - NA_ROWS)
        variant = sr - r + NA_ROWS - 1
        start = pl.multiple_of(sr * GRID_W, GRID_W)
        kw = k_ref[pl.ds(start, NA_ROWS * GRID_W), :]
        vw = v_ref[pl.ds(start, NA_ROWS * GRID_W), :]
        qbd = _head_rows(q_ref[rr * GRID_W:(rr + 1) * GRID_W, :], NA_HEADS)
        s = _dot_nt(qbd, kw) * (HEAD_DIM ** -0.5) + bias_ref[variant]
        m = jnp.max(s, axis=1, keepdims=True)
        p = jnp.exp(s - m)
        l = jnp.sum(p, axis=1, keepdims=True)
        obd = _dot(p.astype(BF16), vw) / l
        o_ref[rr * GRID_W:(rr + 1) * GRID_W, :] = _head_cols(obd, NA_HEADS).astype(BF16)


def _na(zd, bias, rows_per_step):
    bsz, seq, _ = zd.shape
    rows = seq // GRID_W
    assert rows >= NA_ROWS and rows % rows_per_step == 0
    tq = rows_per_step * GRID_W
    whole = lambda j: pl.BlockSpec((None, seq, BRANCH_W), lambda b, i: (b, 0, j))
    return pl.pallas_call(
        functools.partial(_na_kernel, rows=rows, rows_per_step=rows_per_step),
        grid=(bsz, rows // rows_per_step),
        in_specs=[pl.BlockSpec((None, tq, BRANCH_W), lambda b, i: (b, i, 0)), whole(1), whole(2),
                  _resident((NA_ROWS, NA_HEADS * GRID_W, NA_ROWS * GRID_W))],
        out_specs=pl.BlockSpec((None, tq, BRANCH_W), lambda b, i: (b, i, 0)),
        out_shape=jax.ShapeDtypeStruct((bsz, seq, BRANCH_W), BF16),
        compiler_params=_params("parallel", "arbitrary"),
        name="neighborhood_attention",
    )(zd, zd, zd, bias)


def _join_residues(ref, scr):
    dil, rows, width = ref.shape
    for j in range(width // LANES):
        for r in range(dil):
            scr[j, pl.ds(r, rows, stride=dil), :] = ref[r, :, j * LANES:(j + 1) * LANES].astype(F32)
    return jnp.concatenate([scr[j] for j in range(width // LANES)], axis=1)


def _merge_kernel(x_ref, g_ref, oa_ref, ob0_ref, ob1_ref, ob2_ref, l0_ref, l1_ref, l2_ref,
                  oc_ref, od_ref, wg_ref, wb_ref, wo_ref, out_ref, *scratch):
    x = x_ref[...]
    hb = _rms(x, g_ref[...]).astype(BF16)
    outs = (ob0_ref[0].astype(F32), _join_residues(ob1_ref, scratch[0]), _join_residues(ob2_ref, scratch[1]))
    lses = (l0_ref[0], _join_residues(l1_ref, scratch[2]), _join_residues(l2_ref, scratch[3]))
    top = jnp.maximum(jnp.maximum(lses[0], lses[1]), lses[2])
    ws = [jnp.exp(l - top) for l in lses]
    den = ws[0] + ws[1] + ws[2]
    ob = sum((wgt / den) * o for wgt, o in zip(ws, outs))
    branches = (oa_ref[...], ob.astype(BF16), oc_ref[...], od_ref[...])
    merged = None
    for j, o in enumerate(branches):
        term = jax.nn.sigmoid(_dot(hb, wg_ref[j])) * _dot(o, wb_ref[j])
        merged = term if merged is None else merged + term
    out_ref[...] = x + _dot(merged.astype(BF16), wo_ref[...])


def _merge(x, w, oa, obs, lses, oc, od, tm):
    bsz, seq, _ = x.shape
    tok = lambda width: pl.BlockSpec((None, tm, width), lambda b, i: (b, i, 0))
    residues = [pl.BlockSpec((None, dil, tm // dil, BRANCH_W), lambda b, i: (b, 0, i, 0)) for _, dil in DIL_PAIRS]
    return pl.pallas_call(
        _merge_kernel,
        grid=(bsz, seq // tm),
        in_specs=[tok(D_MODEL), _resident((1, D_MODEL)), tok(BRANCH_W)] + residues + residues
                 + [tok(BRANCH_W), tok(BRANCH_W),
                    _resident((N_BRANCH, D_MODEL, D_MODEL)), _resident((N_BRANCH, BRANCH_W, D_MODEL)),
                    _resident((D_MODEL, D_MODEL))],
        out_specs=tok(D_MODEL),
        out_shape=jax.ShapeDtypeStruct(x.shape, F32),
        scratch_shapes=[pltpu.VMEM((BRANCH_W // LANES, tm, LANES), F32)] * 4,
        compiler_params=_params("parallel", "parallel"),
        name="merge",
    )(x, w["ln_attn"], oa, *obs, *lses, oc, od, w["w_gate"], w["w_branch"], w["w_out"])


def _ffn_kernel(x_ref, xp_ref, xn_ref, g_ref, wua_ref, wub_ref, cwa_ref, cwb_ref, cba_ref, cbb_ref,
                wd_ref, out_ref, hn_scr, halo_scr, acc_scr, *, tm):
    tile = pl.program_id(1)
    chunk = pl.program_id(2)

    @pl.when(chunk == 0)
    def _():
        g = g_ref[...]
        hn_scr[...] = _rms(x_ref[...], g).astype(BF16)
        keep_prev = jnp.where(tile > 0, 1.0, 0.0)
        keep_next = jnp.where(tile < pl.num_programs(1) - 1, 1.0, 0.0)
        halo = jnp.concatenate([_rms(xp_ref[...], g) * keep_prev, _rms(xn_ref[...], g) * keep_next], axis=0)
        halo_scr[...] = halo.astype(BF16)
        acc_scr[...] = jnp.zeros(acc_scr.shape, F32)

    hn = hn_scr[...]
    halo = halo_scr[...]

    def conv_part(wu_ref, cw_ref, cb_ref):
        wu = wu_ref[...]
        u = _dot(hn, wu)
        uh = _dot(halo, wu)
        row = lax.broadcasted_iota(jnp.int32, u.shape, 0)
        below = jnp.where(row == 0, uh[7:8], pltpu.roll(u, 1, 0))
        above = jnp.where(row == tm - 1, uh[8:9], pltpu.roll(u, tm - 1, 0))
        cw = cw_ref[...]
        return below * cw[0:1] + u * cw[1:2] + above * cw[2:3] + cb_ref[...]

    a = conv_part(wua_ref, cwa_ref, cba_ref)
    b = conv_part(wub_ref, cwb_ref, cbb_ref)
    acc_scr[...] += _dot((jax.nn.gelu(a) * b).astype(BF16), wd_ref[...])

    @pl.when(chunk == pl.num_programs(2) - 1)
    def _():
        out_ref[...] = x_ref[...] + acc_scr[...]


def _ffn(x, w, tm, fc):
    bsz, seq, _ = x.shape
    nchunk = D_FF // fc
    halo_rows = 8
    per_tile = tm // halo_rows
    last = seq // halo_rows - 1
    tok = pl.BlockSpec((None, tm, D_MODEL), lambda b, i, c: (b, i, 0))
    return pl.pallas_call(
        functools.partial(_ffn_kernel, tm=tm),
        grid=(bsz, seq // tm, nchunk),
        in_specs=[tok,
                  pl.BlockSpec((None, halo_rows, D_MODEL), lambda b, i, c: (b, jnp.maximum(i * per_tile - 1, 0), 0)),
                  pl.BlockSpec((None, halo_rows, D_MODEL), lambda b, i, c: (b, jnp.minimum((i + 1) * per_tile, last), 0)),
                  _resident((1, D_MODEL)),
                  pl.BlockSpec((D_MODEL, fc), lambda b, i, c: (0, c)),
                  pl.BlockSpec((D_MODEL, fc), lambda b, i, c: (0, nchunk + c)),
                  pl.BlockSpec((CONV_W, fc), lambda b, i, c: (0, c)),
                  pl.BlockSpec((CONV_W, fc), lambda b, i, c: (0, nchunk + c)),
                  pl.BlockSpec((1, fc), lambda b, i, c: (0, c)),
                  pl.BlockSpec((1, fc), lambda b, i, c: (0, nchunk + c)),
                  pl.BlockSpec((fc, D_MODEL), lambda b, i, c: (c, 0))],
        out_specs=tok,
        out_shape=jax.ShapeDtypeStruct(x.shape, F32),
        scratch_shapes=[pltpu.VMEM((tm, D_MODEL), BF16), pltpu.VMEM((2 * halo_rows, D_MODEL), BF16),
                        pltpu.VMEM((tm, D_MODEL), F32)],
        compiler_params=_params("parallel", "parallel", "arbitrary"),
        name="conv_ffn",
    )(x, x, x, w["ln_ffn"], w["w_ffn_up"], w["w_ffn_up"], w["ffn_conv_w"], w["ffn_conv_w"],
      w["ffn_conv_b"], w["ffn_conv_b"], w["w_ffn_down"])


def _ple_kernel(x_ref, p_ref, g_ref, wg_ref, wp_ref, gf_ref, out_ref, *, final):
    x = x_ref[...]
    gate = jax.nn.sigmoid(_dot(_rms(x, g_ref[...]).astype(BF16), wg_ref[...]))
    y = x + gate * _dot(p_ref[...].astype(BF16), wp_ref[...])
    if final:
        y = _rms(y, gf_ref[...])
    out_ref[...] = y


def _ple(x, p, w, ln_final, tm, final):
    bsz, seq, _ = x.shape
    tok = lambda width: pl.BlockSpec((None, tm, width), lambda b, i: (b, i, 0))
    return pl.pallas_call(
        functools.partial(_ple_kernel, final=final),
        grid=(bsz, seq // tm),
        in_specs=[tok(D_MODEL), tok(PLE_DIM), _resident((1, D_MODEL)), _resident((D_MODEL, D_MODEL)),
                  _resident((PLE_DIM, D_MODEL)), _resident((1, D_MODEL))],
        out_specs=tok(D_MODEL),
        out_shape=jax.ShapeDtypeStruct(x.shape, F32),
        compiler_params=_params("parallel", "parallel"),
        name="ple_final" if final else "ple",
    )(x, p, w["ln_ple"], w["w_ple_gate"], w["w_ple_proj"], ln_final)


def _gather_cols(wmat, cols, sign=None):
    cols = np.asarray(cols)
    picked = jnp.take(wmat, jnp.asarray(np.maximum(cols, 0)), axis=-1)
    scale = (cols >= 0).astype(np.float32) * (1.0 if sign is None else np.asarray(sign, np.float32))
    return picked * jnp.asarray(scale)


def _prep_weights(ln_attn, w_in, q_norm, kv_norm, w_q_up, w_kv_up, w_gate, w_branch, w_out, ln_ffn,
                  w_ffn_up, ffn_conv_w, ffn_conv_b, w_ffn_down, ln_ple, w_ple_gate, w_ple_proj):
    half = MLA_ROPE // 2
    qk_dim = MLA_NOPE + MLA_ROPE
    slot_w = MLA_HEADS * MLA_SLOT

    c0 = A_COLS + B_COLS
    rep = np.repeat(np.arange(SWA_KV_HEADS), SWA_Q_HEADS // SWA_KV_HEADS)[:, None] * HEAD_DIM + np.arange(HEAD_DIM)
    nq = SWA_Q_HEADS * HEAD_DIM
    nk = SWA_KV_HEADS * HEAD_DIM
    main_cols = np.concatenate([np.arange(A_COLS, c0), c0 + np.arange(nq), c0 + nq + rep.reshape(-1),
                                c0 + nq + nk + rep.reshape(-1), np.arange(c0 + C_COLS, c0 + C_COLS + D_COLS)])
    a_cols = np.concatenate([np.arange(A_COLS), np.full((A_PAD - A_COLS,), -1)])

    q_cols = np.full((slot_w,), -1)
    qs_cols = np.full((slot_w,), -1)
    qs_sign = np.ones((slot_w,), np.float32)
    k_cols = np.full((slot_w,), -1)
    e_rows = np.full((slot_w,), -1)
    es_rows = np.full((slot_w,), -1)
    es_sign = np.ones((slot_w,), np.float32)
    for h in range(MLA_HEADS):
        base = h * MLA_SLOT
        q_cols[base:base + qk_dim] = h * qk_dim + np.arange(qk_dim)
        k_cols[base:base + MLA_NOPE] = h * (MLA_NOPE + MLA_V) + np.arange(MLA_NOPE)
        rope0 = base + MLA_NOPE
        qs_cols[rope0:rope0 + half] = h * qk_dim + MLA_NOPE + half + np.arange(half)
        qs_sign[rope0:rope0 + half] = -1.0
        qs_cols[rope0 + half:rope0 + MLA_ROPE] = h * qk_dim + MLA_NOPE + np.arange(half)
        e_rows[rope0:rope0 + MLA_ROPE] = np.arange(MLA_ROPE)
        es_rows[rope0:rope0 + half] = half + np.arange(half)
        es_sign[rope0:rope0 + half] = -1.0
        es_rows[rope0 + half:rope0 + MLA_ROPE] = np.arange(half)
    v_cols = np.full((MLA_HEADS, MLA_VROWS), -1)
    v_cols[:, :MLA_V] = np.arange(MLA_HEADS)[:, None] * (MLA_NOPE + MLA_V) + MLA_NOPE + np.arange(MLA_V)
    v_cols = v_cols.reshape(-1)
    eye = jnp.eye(MLA_SLOT, dtype=F32)

    row = lambda a: a[:, None, :]
    stacked = {
        "ln_attn": row(ln_attn), "q_norm": row(q_norm), "kv_norm": row(kv_norm),
        "w_main": _gather_cols(w_in, main_cols).astype(BF16),
        "w_a": _gather_cols(w_in, a_cols).astype(BF16),
        "wq": _gather_cols(w_q_up, q_cols).astype(BF16),
        "wq_s": _gather_cols(w_q_up, qs_cols, qs_sign).astype(BF16),
        "wk": _gather_cols(w_kv_up, k_cols).astype(BF16),
        "wv_t": jnp.swapaxes(_gather_cols(w_kv_up, v_cols), -1, -2).astype(BF16),
        "w_gate": w_gate.astype(BF16), "w_branch": w_branch.astype(BF16), "w_out": w_out.astype(BF16),
        "ln_ffn": row(ln_ffn), "w_ffn_up": w_ffn_up.astype(BF16), "ffn_conv_w": ffn_conv_w,
        "ffn_conv_b": row(ffn_conv_b), "w_ffn_down": w_ffn_down.astype(BF16),
        "ln_ple": row(ln_ple), "w_ple_gate": w_ple_gate.astype(BF16), "w_ple_proj": w_ple_proj.astype(BF16),
    }
    shared = {
        "e": _gather_cols(eye, e_rows).astype(BF16),
        "e_s": _gather_cols(eye, es_rows, es_sign).astype(BF16),
    }
    return [dict({k: v[i] for k, v in stacked.items()}, **shared) for i in range(DEPTH)]


def _rope_tables(seq):
    inv = ROPE_THETA ** (-jnp.arange(0, MLA_ROPE, 2, dtype=F32) / MLA_ROPE)
    ang = jnp.arange(seq, dtype=F32)[:, None] * inv[None]
    cos, sin = jnp.cos(ang), jnp.sin(ang)
    pad = MLA_SLOT - MLA_NOPE - MLA_ROPE
    cos_t = jnp.concatenate([jnp.ones((seq, MLA_NOPE), F32), cos, cos, jnp.zeros((seq, pad), F32)], axis=1)
    sin_t = jnp.concatenate([jnp.zeros((seq, MLA_NOPE), F32), sin, sin, jnp.zeros((seq, pad), F32)], axis=1)
    return cos_t, sin_t


def _t5_bucket(rel):
    nb = REL_BUCKETS // 2
    max_exact = nb // 2
    ret = jnp.where(rel > 0, nb, 0)
    n = jnp.abs(rel)
    nf = jnp.maximum(n, 1).astype(F32)
    large = max_exact + (jnp.log(nf / max_exact) / math.log(REL_MAX_DIST / max_exact) * (nb - max_exact)).astype(jnp.int32)
    large = jnp.minimum(large, nb - 1)
    return ret + jnp.where(n < max_exact, n, large)


def _band_bias(rel_table, block, window, dil, head0):
    off = np.arange(3 * block)[None, :] - block - np.arange(block)[:, None]
    bias = rel_table[_t5_bucket(jnp.asarray(off * dil))][..., head0:head0 + DIL_SLOTS]
    bias = jnp.where(jnp.asarray(np.abs(off) <= window)[..., None], bias.astype(F32), NEG_INF)
    return jnp.moveaxis(bias, -1, 0).reshape(DIL_SLOTS * block, 3 * block)


def _na_bias(rpb):
    qc = np.arange(GRID_W)[:, None]
    kc = np.arange(GRID_W)[None, :]
    sc = np.clip(qc - NA_COLS // 2, 0, GRID_W - NA_COLS)
    valid = (kc >= sc) & (kc < sc + NA_COLS)
    dc = np.clip(kc - qc, -(NA_COLS - 1), NA_COLS - 1) + NA_COLS - 1
    table = rpb.astype(F32)[:, :, jnp.asarray(dc)]
    table = jnp.where(jnp.asarray(valid)[None, None], table, NEG_INF)
    variants = [jnp.transpose(table[:, v:v + NA_ROWS], (0, 2, 1, 3)).reshape(NA_HEADS * GRID_W, NA_ROWS * GRID_W)
                for v in range(NA_ROWS)]
    return jnp.stack(variants)


def _trunk(x, p, weights, attn_sink, na_rpb, rel_table, ln_final):
    bsz, seq, _ = x.shape
    cos_t, sin_t = _rope_tables(seq)
    dil_bias = [_band_bias(rel_table, DIL_BLOCK, window // (2 * dil), dil, gi * DIL_SLOTS)
                for gi, (window, dil) in enumerate(DIL_PAIRS)]
    swa_bias = _band_bias(rel_table, SWA_BLOCK, SWA_WINDOW, 1, DIL_HEADS)
    ln_final = ln_final[None, :]
    tm = 512
    for i in range(DEPTH):
        w = weights[i]
        qa, ka, vt, zb0, zb1, zb2, zc, zd = _attn_in(x, w, cos_t, sin_t, tm)
        oa = _mla(qa, ka, vt, 1024, 2048)
        obs, lses = [], []
        for gi, zb in enumerate((zb0[:, None], zb1, zb2)):
            o, lse = _banded(zb, dil_bias[gi], None, blk=DIL_BLOCK, tm=512, with_lse=True)
            obs.append(o)
            lses.append(lse)
        sink = jnp.repeat(attn_sink[i].astype(F32), SWA_BLOCK)[:, None]
        (oc,) = _banded(zc[:, None], swa_bias, sink, blk=SWA_BLOCK, tm=512, with_lse=False)
        oc = oc.reshape(bsz, seq, BRANCH_W)
        od = _na(zd, _na_bias(na_rpb[i]), rows_per_step=4)
        x = _merge(x, w, oa, obs, lses, oc, od, tm)
        x = _ffn(x, w, tm, 256)
        x = _ple(x, p[i], w, ln_final, tm, final=(i == DEPTH - 1))
    return x


def kernel(x_prompt, x_sample, p_prompt, p_sample, ln_attn, w_in, q_norm, kv_norm, w_q_up, w_kv_up,
           attn_sink, na_rpb, rel_table, w_gate, w_branch, w_out, ln_ffn, w_ffn_up, ffn_conv_w,
           ffn_conv_b, w_ffn_down, ln_ple, w_ple_gate, w_ple_proj, ln_final):
    weights = _prep_weights(ln_attn, w_in, q_norm, kv_norm, w_q_up, w_kv_up, w_gate, w_branch, w_out,
                            ln_ffn, w_ffn_up, ffn_conv_w, ffn_conv_b, w_ffn_down, ln_ple, w_ple_gate,
                            w_ple_proj)
    nprompt = x_prompt.shape[0]
    x = jnp.concatenate([x_prompt, x_sample], axis=0)
    p = jnp.concatenate([p_prompt, p_sample], axis=1)
    y = _trunk(x, p, weights, attn_sink, na_rpb, rel_table, ln_final)
    return (y[:nprompt], y[nprompt:])
```

```python
import functools
import math

import jax
import jax.numpy as jnp
import numpy as np
from jax import lax
from jax.experimental import pallas as pl
from jax.experimental.pallas import tpu as pltpu

D_MODEL = 1024
DEPTH = 4
PLE_DIM = 256
GRID_W = 64
HEAD_DIM = 64
BRANCH_W = 256
N_BRANCH = 4
EPS = 1e-6
NEG_INF = -1e30

MLA_HEADS = 4
MLA_Q_RANK = 256
MLA_KV_RANK = 128
MLA_NOPE = 64
MLA_ROPE = 32
MLA_V = 64
ROPE_THETA = 10000.0
MLA_SLOT = 128
MLA_VROWS = 96
LANES = 128
LOG2_E = math.log2(math.e)

DIL_PAIRS = ((128, 1), (512, 4), (2048, 16))
DIL_SLOTS = 4
DIL_HEADS = DIL_SLOTS * len(DIL_PAIRS)
DIL_BLOCK = 64

SWA_Q_HEADS = 4
SWA_KV_HEADS = 2
SWA_WINDOW = 128
SWA_BLOCK = 128

NA_HEADS = 4
NA_ROWS = 8
NA_COLS = 16

REL_BUCKETS = 32
REL_MAX_DIST = 1024

D_FF = 2816
CONV_W = 3

A_COLS = MLA_Q_RANK + MLA_KV_RANK + MLA_ROPE
B_COLS = 3 * DIL_HEADS * HEAD_DIM
C_COLS = (SWA_Q_HEADS + 2 * SWA_KV_HEADS) * HEAD_DIM
D_COLS = 3 * NA_HEADS * HEAD_DIM
A_PAD = 512
C_EXP = 3 * BRANCH_W
MAIN_COLS = B_COLS + C_EXP + D_COLS
MAIN_CHUNK = 768

BF16 = jnp.bfloat16
F32 = jnp.float32

VMEM_LIMIT_BYTES = 56 * 1024 * 1024


def _params(*semantics):
    return pltpu.CompilerParams(dimension_semantics=semantics, vmem_limit_bytes=VMEM_LIMIT_BYTES)


def _resident(shape):
    return pl.BlockSpec(shape, lambda *_: (0,) * len(shape), pipeline_mode=pl.Buffered(1))


def _dot(a, b):
    return jnp.dot(a, b, preferred_element_type=F32)


def _dot_nt(a, b):
    return lax.dot_general(a, b, (((1,), (1,)), ((), ())), preferred_element_type=F32)


def _rms(x, g):
    return x * lax.rsqrt(jnp.mean(x * x, axis=-1, keepdims=True) + EPS) * g


def _split_residues(val, scr, out_ref, dil):
    tm = val.shape[0]
    nslab = val.shape[1] // LANES
    for j in range(nslab):
        scr[j] = val[:, j * LANES:(j + 1) * LANES]
    for r in range(dil):
        for j in range(nslab):
            out_ref[r, :, j * LANES:(j + 1) * LANES] = scr[j, pl.ds(r, tm // dil, stride=dil), :].astype(BF16)


def _attn_in_kernel(x_ref, g_ref, wmain_ref, wa_ref, qn_ref, kvn_ref, wq_ref, wqs_ref, wk_ref,
                    wvt_ref, e_ref, es_ref, cos_ref, sin_ref,
                    qa_ref, ka_ref, vt_ref, zb0_ref, zb1_ref, zb2_ref, zc_ref, zd_ref, scr1, scr2):
    hb = _rms(x_ref[...], g_ref[...]).astype(BF16)
    chunk = lambda c: _dot(hb, wmain_ref[:, c * MAIN_CHUNK:(c + 1) * MAIN_CHUNK])
    zb0_ref[...] = chunk(0).astype(BF16)
    _split_residues(chunk(1), scr1, zb1_ref, DIL_PAIRS[1][1])
    _split_residues(chunk(2), scr2, zb2_ref, DIL_PAIRS[2][1])
    zc_ref[...] = chunk(3).astype(BF16)
    zd_ref[...] = chunk(4).astype(BF16)

    za = _dot(hb, wa_ref[...])
    cq = _rms(za[:, :MLA_Q_RANK], qn_ref[...]).astype(BF16)
    ckv = _rms(za[:, MLA_Q_RANK:MLA_Q_RANK + MLA_KV_RANK], kvn_ref[...]).astype(BF16)
    kr = za[:, MLA_Q_RANK + MLA_KV_RANK:]
    kr_hi = kr.astype(BF16)
    kr_lo = (kr - kr_hi.astype(F32)).astype(BF16)
    cos = jnp.concatenate([cos_ref[...]] * MLA_HEADS, axis=1)
    sin = jnp.concatenate([sin_ref[...]] * MLA_HEADS, axis=1)
    q = _dot(cq, wq_ref[...]) * cos + _dot(cq, wqs_ref[...]) * sin
    qa_ref[...] = (q * (LOG2_E * (MLA_NOPE + MLA_ROPE) ** -0.5)).astype(BF16)
    k_rope = _dot(kr_hi, e_ref[...]) + _dot(kr_lo, e_ref[...])
    k_swap = _dot(kr_hi, es_ref[...]) + _dot(kr_lo, es_ref[...])
    ka_ref[...] = ((_dot(ckv, wk_ref[...]) + k_rope) * cos + k_swap * sin).astype(BF16)
    vt = _dot_nt(wvt_ref[...], ckv)
    row = lax.broadcasted_iota(jnp.int32, vt.shape, 0)
    vt_ref[...] = jnp.where(row % MLA_VROWS == MLA_V, 1.0, vt).astype(BF16)


def _attn_in(x, w, cos_t, sin_t, tm):
    bsz, seq, _ = x.shape
    tok = lambda width: pl.BlockSpec((None, tm, width), lambda b, i: (b, i, 0))
    table = pl.BlockSpec((tm, MLA_SLOT), lambda b, i: (i, 0))
    group = 3 * BRANCH_W
    slot_w = MLA_HEADS * MLA_SLOT
    out_specs = [tok(slot_w), tok(slot_w),
                 pl.BlockSpec((None, MLA_HEADS * MLA_VROWS, tm), lambda b, i: (b, 0, i)), tok(group)]
    out_shape = [jax.ShapeDtypeStruct((bsz, seq, slot_w), BF16), jax.ShapeDtypeStruct((bsz, seq, slot_w), BF16),
                 jax.ShapeDtypeStruct((bsz, MLA_HEADS * MLA_VROWS, seq), BF16),
                 jax.ShapeDtypeStruct((bsz, seq, group), BF16)]
    scratch = []
    for _, dil in DIL_PAIRS[1:]:
        out_specs.append(pl.BlockSpec((None, dil, tm // dil, group), lambda b, i: (b, 0, i, 0)))
        out_shape.append(jax.ShapeDtypeStruct((bsz, dil, seq // dil, group), BF16))
        scratch.append(pltpu.VMEM((group // LANES, tm, LANES), F32))
    out_specs += [tok(C_EXP), tok(D_COLS)]
    out_shape += [jax.ShapeDtypeStruct((bsz, seq, C_EXP), BF16), jax.ShapeDtypeStruct((bsz, seq, D_COLS), BF16)]
    return pl.pallas_call(
        _attn_in_kernel,
        grid=(bsz, seq // tm),
        in_specs=[tok(D_MODEL), _resident((1, D_MODEL)), _resident((D_MODEL, MAIN_COLS)),
                  _resident((D_MODEL, A_PAD)), _resident((1, MLA_Q_RANK)), _resident((1, MLA_KV_RANK)),
                  _resident((MLA_Q_RANK, slot_w)), _resident((MLA_Q_RANK, slot_w)),
                  _resident((MLA_KV_RANK, slot_w)), _resident((MLA_HEADS * MLA_VROWS, MLA_KV_RANK)),
                  _resident((MLA_SLOT, slot_w)), _resident((MLA_SLOT, slot_w)),
                  table, table],
        out_specs=out_specs, out_shape=out_shape, scratch_shapes=scratch,
        compiler_params=_params("parallel", "parallel"),
        name="attn_in",
    )(x, w["ln_attn"], w["w_main"], w["w_a"], w["q_norm"], w["kv_norm"], w["wq"], w["wq_s"],
      w["wk"], w["wv_t"], w["e"], w["e_s"], cos_t, sin_t)


def _mla_kernel(q_ref, k_ref, vt_ref, o_ref, m_scr, acc_scr):
    kv = pl.program_id(2)

    @pl.when(kv == 0)
    def _():
        m_scr[...] = jnp.full(m_scr.shape, NEG_INF, F32)
        acc_scr[...] = jnp.zeros(acc_scr.shape, F32)

    def scores(h):
        return _dot_nt(k_ref[:, h * MLA_SLOT:(h + 1) * MLA_SLOT], q_ref[:, h * MLA_SLOT:(h + 1) * MLA_SLOT])

    s_next = scores(0)
    for h in range(MLA_HEADS):
        s = s_next
        if h + 1 < MLA_HEADS:
            s_next = scores(h + 1)
        m_prev = m_scr[h]
        m_new = jnp.maximum(m_prev, jnp.max(s, axis=0, keepdims=True))
        p = jnp.exp2(s - m_new).astype(BF16)
        acc_scr[h] = jnp.exp2(m_prev - m_new) * acc_scr[h] + _dot(vt_ref[h * MLA_VROWS:(h + 1) * MLA_VROWS, :], p)
        m_scr[h] = m_new

    @pl.when(kv == pl.num_programs(2) - 1)
    def _():
        o_t = jnp.concatenate([acc_scr[h, :MLA_V, :] / acc_scr[h, MLA_V:MLA_V + 1, :] for h in range(MLA_HEADS)],
                              axis=0)
        o_ref[...] = o_t.T.astype(BF16)


def _mla(qa, ka, vt, tq, tk):
    bsz, seq, _ = qa.shape
    tq, tk = min(tq, seq), min(tk, seq)
    return pl.pallas_call(
        _mla_kernel,
        grid=(bsz, seq // tq, seq // tk),
        in_specs=[pl.BlockSpec((None, tq, MLA_HEADS * MLA_SLOT), lambda b, i, j: (b, i, 0)),
                  pl.BlockSpec((None, tk, MLA_HEADS * MLA_SLOT), lambda b, i, j: (b, j, 0)),
                  pl.BlockSpec((None, MLA_HEADS * MLA_VROWS, tk), lambda b, i, j: (b, 0, j))],
        out_specs=pl.BlockSpec((None, tq, MLA_HEADS * MLA_V), lambda b, i, j: (b, i, 0)),
        out_shape=jax.ShapeDtypeStruct((bsz, seq, MLA_HEADS * MLA_V), BF16),
        scratch_shapes=[pltpu.VMEM((MLA_HEADS, 1, tq), F32), pltpu.VMEM((MLA_HEADS, MLA_VROWS, tq), F32)],
        compiler_params=_params("parallel", "parallel", "arbitrary"),
        name="mla_attention",
    )(qa, ka, vt)


def _head_rows(x, nheads):
    head = lax.broadcasted_iota(jnp.int32, x.shape, 1) // HEAD_DIM
    xf = x.astype(F32)
    return jnp.concatenate([jnp.where(head == h, xf, 0.0) for h in range(nheads)], axis=0).astype(x.dtype)


def _head_cols(x, nheads):
    n = x.shape[0] // nheads
    head = lax.broadcasted_iota(jnp.int32, (n, x.shape[1]), 1) // HEAD_DIM
    out = x[:n]
    for h in range(1, nheads):
        out = jnp.where(head == h, x[h * n:(h + 1) * n], out)
    return out


def _banded_kernel(*refs, blk, tm, seq_len, tile_axis, with_sink, with_lse):
    q_ref, kp_ref, km_ref, kn_ref, vp_ref, vm_ref, vn_ref, bias_ref = refs[:8]
    refs = refs[8:]
    sink_ref = None
    if with_sink:
        sink_ref, refs = refs[0], refs[1:]
    o_ref = refs[0]
    lse_ref = refs[1] if with_lse else None

    tile = pl.program_id(tile_axis)
    kcat = jnp.concatenate([kp_ref[...], km_ref[...], kn_ref[...]], axis=0)
    vcat = jnp.concatenate([vp_ref[...], vm_ref[...], vn_ref[...]], axis=0)
    nblk = tm // blk
    for jb in range(nblk):
        qbd = _head_rows(q_ref[jb * blk:(jb + 1) * blk, :], DIL_SLOTS)
        kw = kcat[jb * blk:(jb + 3) * blk]
        vw = vcat[jb * blk:(jb + 3) * blk]
        s = _dot_nt(qbd, kw) * (HEAD_DIM ** -0.5) + bias_ref[...]
        if jb == 0 or jb == nblk - 1:
            kpos = tile * tm + (jb - 1) * blk + lax.broadcasted_iota(jnp.int32, s.shape, 1)
            if jb == 0:
                s = jnp.where(kpos >= 0, s, NEG_INF)
            if jb == nblk - 1:
                s = jnp.where(kpos < seq_len, s, NEG_INF)
        m = jnp.max(s, axis=1, keepdims=True)
        if with_sink:
            m = jnp.maximum(m, sink_ref[...])
        p = jnp.exp(s - m)
        l = jnp.sum(p, axis=1, keepdims=True)
        if with_sink:
            l = l + jnp.exp(sink_ref[...] - m)
        obd = _dot(p.astype(BF16), vw) / l
        o_ref[jb * blk:(jb + 1) * blk, :] = _head_cols(obd, DIL_SLOTS).astype(BF16)
        if with_lse:
            lse = jnp.broadcast_to(m + jnp.log(l), obd.shape)
            lse_ref[jb * blk:(jb + 1) * blk, :] = _head_cols(lse, DIL_SLOTS)


def _banded(z, bias, sink, *, blk, tm, with_lse):
    bsz, dil, sub, _ = z.shape
    tm = min(tm, sub)
    per_tile = tm // blk
    last_blk = sub // blk - 1

    def main(j):
        return pl.BlockSpec((None, None, tm, BRANCH_W), lambda b, r, i: (b, r, i, j))

    def prev(j):
        return pl.BlockSpec((None, None, blk, BRANCH_W),
                            lambda b, r, i: (b, r, jnp.maximum(i * per_tile - 1, 0), j))

    def nxt(j):
        return pl.BlockSpec((None, None, blk, BRANCH_W),
                            lambda b, r, i: (b, r, jnp.minimum((i + 1) * per_tile, last_blk), j))

    in_specs = [main(0), prev(1), main(1), nxt(1), prev(2), main(2), nxt(2),
                _resident((DIL_SLOTS * blk, 3 * blk))]
    args = [z, z, z, z, z, z, z, bias]
    if sink is not None:
        in_specs.append(_resident((DIL_SLOTS * blk, 1)))
        args.append(sink)
    out_spec = pl.BlockSpec((None, None, tm, BRANCH_W), lambda b, r, i: (b, r, i, 0))
    out_specs = [out_spec]
    out_shape = [jax.ShapeDtypeStruct((bsz, dil, sub, BRANCH_W), BF16)]
    if with_lse:
        out_specs.append(out_spec)
        out_shape.append(jax.ShapeDtypeStruct((bsz, dil, sub, BRANCH_W), F32))
    return pl.pallas_call(
        functools.partial(_banded_kernel, blk=blk, tm=tm, seq_len=sub, tile_axis=2,
                          with_sink=sink is not None, with_lse=with_lse),
        grid=(bsz, dil, sub // tm),
        in_specs=in_specs, out_specs=out_specs, out_shape=out_shape,
        compiler_params=_params("parallel", "parallel", "parallel"),
        name=f"banded_d{dil}_b{blk}",
    )(*args)


def _na_kernel(q_ref, k_ref, v_ref, bias_ref, o_ref, *, rows, rows_per_step):
    step = pl.program_id(1)
    for rr in range(rows_per_step):
        r = step * rows_per_step + rr
        sr = jnp.clip(r - NA_ROWS // 2, 0, rows - NA_ROWS)
        variant = sr - r + NA_ROWS - 1
        start = pl.multiple_of(sr * GRID_W, GRID_W)
        kw = k_ref[pl.ds(start, NA_ROWS * GRID_W), :]
        vw = v_ref[pl.ds(start, NA_ROWS * GRID_W), :]
        qbd = _head_rows(q_ref[rr * GRID_W:(rr + 1) * GRID_W, :], NA_HEADS)
        s = _dot_nt(qbd, kw) * (HEAD_DIM ** -0.5) + bias_ref[variant]
        m = jnp.max(s, axis=1, keepdims=True)
        p = jnp.exp(s - m)
        l = jnp.sum(p, axis=1, keepdims=True)
        obd = _dot(p.astype(BF16), vw) / l
        o_ref[rr * GRID_W:(rr + 1) * GRID_W, :] = _head_cols(obd, NA_HEADS).astype(BF16)


def _na(zd, bias, rows_per_step):
    bsz, seq, _ = zd.shape
    rows = seq // GRID_W
    assert rows >= NA_ROWS and rows % rows_per_step == 0
    tq = rows_per_step * GRID_W
    whole = lambda j: pl.BlockSpec((None, seq, BRANCH_W), lambda b, i: (b, 0, j))
    return pl.pallas_call(
        functools.partial(_na_kernel, rows=rows, rows_per_step=rows_per_step),
        grid=(bsz, rows // rows_per_step),
        in_specs=[pl.BlockSpec((None, tq, BRANCH_W), lambda b, i: (b, i, 0)), whole(1), whole(2),
                  _resident((NA_ROWS, NA_HEADS * GRID_W, NA_ROWS * GRID_W))],
        out_specs=pl.BlockSpec((None, tq, BRANCH_W), lambda b, i: (b, i, 0)),
        out_shape=jax.ShapeDtypeStruct((bsz, seq, BRANCH_W), BF16),
        compiler_params=_params("parallel", "arbitrary"),
        name="neighborhood_attention",
    )(zd, zd, zd, bias)


def _join_residues(ref, scr):
    dil, rows, width = ref.shape
    for j in range(width // LANES):
        for r in range(dil):
            scr[j, pl.ds(r, rows, stride=dil), :] = ref[r, :, j * LANES:(j + 1) * LANES].astype(F32)
    return jnp.concatenate([scr[j] for j in range(width // LANES)], axis=1)


def _merge_kernel(x_ref, g_ref, oa_ref, ob0_ref, ob1_ref, ob2_ref, l0_ref, l1_ref, l2_ref,
                  oc_ref, od_ref, wg_ref, wb_ref, wo_ref, out_ref, *scratch):
    x = x_ref[...]
    hb = _rms(x, g_ref[...]).astype(BF16)
    outs = (ob0_ref[0].astype(F32), _join_residues(ob1_ref, scratch[0]), _join_residues(ob2_ref, scratch[1]))
    lses = (l0_ref[0], _join_residues(l1_ref, scratch[2]), _join_residues(l2_ref, scratch[3]))
    top = jnp.maximum(jnp.maximum(lses[0], lses[1]), lses[2])
    ws = [jnp.exp(l - top) for l in lses]
    den = ws[0] + ws[1] + ws[2]
    ob = sum((wgt / den) * o for wgt, o in zip(ws, outs))
    branches = (oa_ref[...], ob.astype(BF16), oc_ref[...], od_ref[...])
    merged = None
    for j, o in enumerate(branches):
        term = jax.nn.sigmoid(_dot(hb, wg_ref[j])) * _dot(o, wb_ref[j])
        merged = term if merged is None else merged + term
    out_ref[...] = x + _dot(merged.astype(BF16), wo_ref[...])


def _merge(x, w, oa, obs, lses, oc, od, tm):
    bsz, seq, _ = x.shape
    tok = lambda width: pl.BlockSpec((None, tm, width), lambda b, i: (b, i, 0))
    residues = [pl.BlockSpec((None, dil, tm // dil, BRANCH_W), lambda b, i: (b, 0, i, 0)) for _, dil in DIL_PAIRS]
    return pl.pallas_call(
        _merge_kernel,
        grid=(bsz, seq // tm),
        in_specs=[tok(D_MODEL), _resident((1, D_MODEL)), tok(BRANCH_W)] + residues + residues
                 + [tok(BRANCH_W), tok(BRANCH_W),
                    _resident((N_BRANCH, D_MODEL, D_MODEL)), _resident((N_BRANCH, BRANCH_W, D_MODEL)),
                    _resident((D_MODEL, D_MODEL))],
        out_specs=tok(D_MODEL),
        out_shape=jax.ShapeDtypeStruct(x.shape, F32),
        scratch_shapes=[pltpu.VMEM((BRANCH_W // LANES, tm, LANES), F32)] * 4,
        compiler_params=_params("parallel", "parallel"),
        name="merge",
    )(x, w["ln_attn"], oa, *obs, *lses, oc, od, w["w_gate"], w["w_branch"], w["w_out"])


SUBLANES = 8
GELU_C0 = math.sqrt(2.0 / math.pi)
GELU_C1 = GELU_C0 * 0.044715


def _ffn_kernel(x_ref, xp_ref, xn_ref, g_ref, wu_ref, cw_ref, cb_ref, wd_ref, out_ref, *, tm):
    tile = pl.program_id(1)
    g = g_ref[...]
    x = x_ref[...]
    hn = _rms(x, g).astype(BF16)
    keep_prev = jnp.where(tile > 0, 1.0, 0.0)
    keep_next = jnp.where(tile < pl.num_programs(1) - 1, 1.0, 0.0)
    halo = jnp.concatenate([_rms(xp_ref[...], g) * keep_prev, _rms(xn_ref[...], g) * keep_next], axis=0)
    wu = wu_ref[...]
    u = _dot(hn, wu)
    uh = _dot(halo.astype(BF16), wu)
    row = lax.broadcasted_iota(jnp.int32, (SUBLANES, u.shape[1]), 0)
    below = pltpu.roll(u, 1, 0)
    below = jnp.concatenate([jnp.where(row == 0, uh[SUBLANES - 1:SUBLANES], below[:SUBLANES]), below[SUBLANES:]], axis=0)
    above = pltpu.roll(u, tm - 1, 0)
    above = jnp.concatenate([above[:tm - SUBLANES],
                             jnp.where(row == SUBLANES - 1, uh[SUBLANES:SUBLANES + 1], above[tm - SUBLANES:])], axis=0)
    cw = cw_ref[...]
    y = below * cw[0:1] + u * cw[1:2] + above * cw[2:3] + cb_ref[...]
    a, half_b = y[:, :D_FF], y[:, D_FF:]
    act = a * (1.0 + jnp.tanh(a * (GELU_C0 + GELU_C1 * (a * a)))) * half_b
    out_ref[...] = x + _dot(act.astype(BF16), wd_ref[...])


def _ffn(x, w, tm):
    bsz, seq, _ = x.shape
    per_tile = tm // SUBLANES
    last = seq // SUBLANES - 1
    tok = pl.BlockSpec((None, tm, D_MODEL), lambda b, i: (b, i, 0))
    return pl.pallas_call(
        functools.partial(_ffn_kernel, tm=tm),
        grid=(bsz, seq // tm),
        in_specs=[tok,
                  pl.BlockSpec((None, SUBLANES, D_MODEL), lambda b, i: (b, jnp.maximum(i * per_tile - 1, 0), 0)),
                  pl.BlockSpec((None, SUBLANES, D_MODEL), lambda b, i: (b, jnp.minimum((i + 1) * per_tile, last), 0)),
                  _resident((1, D_MODEL)), _resident((D_MODEL, 2 * D_FF)), _resident((CONV_W, 2 * D_FF)),
                  _resident((1, 2 * D_FF)), _resident((D_FF, D_MODEL))],
        out_specs=tok,
        out_shape=jax.ShapeDtypeStruct(x.shape, F32),
        compiler_params=_params("parallel", "parallel"),
        name="conv_ffn",
    )(x, x, x, w["ln_ffn"], w["w_ffn_up"], w["ffn_conv_w"], w["ffn_conv_b"], w["w_ffn_down"])


def _ple_kernel(x_ref, p_ref, g_ref, wg_ref, wp_ref, gf_ref, out_ref, *, final):
    x = x_ref[...]
    gate = jax.nn.sigmoid(_dot(_rms(x, g_ref[...]).astype(BF16), wg_ref[...]))
    y = x + gate * _dot(p_ref[...].astype(BF16), wp_ref[...])
    if final:
        y = _rms(y, gf_ref[...])
    out_ref[...] = y


def _ple(x, p, layer, w, ln_final, tm, final):
    bsz, seq, _ = x.shape
    tok = lambda width: pl.BlockSpec((None, tm, width), lambda b, i: (b, i, 0))
    return pl.pallas_call(
        functools.partial(_ple_kernel, final=final),
        grid=(bsz, seq // tm),
        in_specs=[tok(D_MODEL), pl.BlockSpec((None, None, tm, PLE_DIM), lambda b, i: (layer, b, i, 0)),
                  _resident((1, D_MODEL)), _resident((D_MODEL, D_MODEL)),
                  _resident((PLE_DIM, D_MODEL)), _resident((1, D_MODEL))],
        out_specs=tok(D_MODEL),
        out_shape=jax.ShapeDtypeStruct(x.shape, F32),
        compiler_params=_params("parallel", "parallel"),
        name="ple_final" if final else "ple",
    )(x, p, w["ln_ple"], w["w_ple_gate"], w["w_ple_proj"], ln_final)


def _gather_cols(wmat, cols, sign=None):
    cols = np.asarray(cols)
    picked = jnp.take(wmat, jnp.asarray(np.maximum(cols, 0)), axis=-1)
    scale = (cols >= 0).astype(np.float32) * (1.0 if sign is None else np.asarray(sign, np.float32))
    return picked * jnp.asarray(scale)


def _prep_weights(ln_attn, w_in, q_norm, kv_norm, w_q_up, w_kv_up, w_gate, w_branch, w_out, ln_ffn,
                  w_ffn_up, ffn_conv_w, ffn_conv_b, w_ffn_down, ln_ple, w_ple_gate, w_ple_proj):
    half = MLA_ROPE // 2
    qk_dim = MLA_NOPE + MLA_ROPE
    slot_w = MLA_HEADS * MLA_SLOT

    c0 = A_COLS + B_COLS
    nq = SWA_Q_HEADS * HEAD_DIM
    nk = SWA_KV_HEADS * HEAD_DIM

    def per_query_head(base):
        heads = np.repeat(np.arange(SWA_KV_HEADS), SWA_Q_HEADS // SWA_KV_HEADS)
        return [w_in[..., base + h * HEAD_DIM:base + (h + 1) * HEAD_DIM] for h in heads]

    w_main = jnp.concatenate([w_in[..., A_COLS:c0 + nq]] + per_query_head(c0 + nq) + per_query_head(c0 + nq + nk)
                             + [w_in[..., c0 + C_COLS:]], axis=-1)
    w_a = jnp.pad(w_in[..., :A_COLS], ((0, 0), (0, 0), (0, A_PAD - A_COLS)))

    q_cols = np.full((slot_w,), -1)
    qs_cols = np.full((slot_w,), -1)
    qs_sign = np.ones((slot_w,), np.float32)
    k_cols = np.full((slot_w,), -1)
    e_rows = np.full((slot_w,), -1)
    es_rows = np.full((slot_w,), -1)
    es_sign = np.ones((slot_w,), np.float32)
    for h in range(MLA_HEADS):
        base = h * MLA_SLOT
        q_cols[base:base + qk_dim] = h * qk_dim + np.arange(qk_dim)
        k_cols[base:base + MLA_NOPE] = h * (MLA_NOPE + MLA_V) + np.arange(MLA_NOPE)
        rope0 = base + MLA_NOPE
        qs_cols[rope0:rope0 + half] = h * qk_dim + MLA_NOPE + half + np.arange(half)
        qs_sign[rope0:rope0 + half] = -1.0
        qs_cols[rope0 + half:rope0 + MLA_ROPE] = h * qk_dim + MLA_NOPE + np.arange(half)
        e_rows[rope0:rope0 + MLA_ROPE] = np.arange(MLA_ROPE)
        es_rows[rope0:rope0 + half] = half + np.arange(half)
        es_sign[rope0:rope0 + half] = -1.0
        es_rows[rope0 + half:rope0 + MLA_ROPE] = np.arange(half)
    v_cols = np.full((MLA_HEADS, MLA_VROWS), -1)
    v_cols[:, :MLA_V] = np.arange(MLA_HEADS)[:, None] * (MLA_NOPE + MLA_V) + MLA_NOPE + np.arange(MLA_V)
    v_cols = v_cols.reshape(-1)
    eye = jnp.eye(MLA_SLOT, dtype=F32)

    row = lambda a: a[:, None, :]
    gate_half = jnp.asarray(np.concatenate([np.ones(D_FF, np.float32), np.full(D_FF, 0.5, np.float32)]))
    stacked = {
        "ln_attn": row(ln_attn), "q_norm": row(q_norm), "kv_norm": row(kv_norm),
        "w_main": w_main.astype(BF16),
        "w_a": w_a.astype(BF16),
        "wq": _gather_cols(w_q_up, q_cols).astype(BF16),
        "wq_s": _gather_cols(w_q_up, qs_cols, qs_sign).astype(BF16),
        "wk": _gather_cols(w_kv_up, k_cols).astype(BF16),
        "wv_t": jnp.swapaxes(_gather_cols(w_kv_up, v_cols), -1, -2).astype(BF16),
        "w_gate": w_gate.astype(BF16), "w_branch": w_branch.astype(BF16), "w_out": w_out.astype(BF16),
        "ln_ffn": row(ln_ffn), "w_ffn_up": w_ffn_up.astype(BF16), "ffn_conv_w": ffn_conv_w * gate_half,
        "ffn_conv_b": row(ffn_conv_b * gate_half), "w_ffn_down": w_ffn_down.astype(BF16),
        "ln_ple": row(ln_ple), "w_ple_gate": w_ple_gate.astype(BF16), "w_ple_proj": w_ple_proj.astype(BF16),
    }
    shared = {
        "e": _gather_cols(eye, e_rows).astype(BF16),
        "e_s": _gather_cols(eye, es_rows, es_sign).astype(BF16),
    }
    return [dict({k: v[i] for k, v in stacked.items()}, **shared) for i in range(DEPTH)]


def _rope_tables(seq):
    inv = ROPE_THETA ** (-jnp.arange(0, MLA_ROPE, 2, dtype=F32) / MLA_ROPE)
    ang = jnp.arange(seq, dtype=F32)[:, None] * inv[None]
    cos, sin = jnp.cos(ang), jnp.sin(ang)
    pad = MLA_SLOT - MLA_NOPE - MLA_ROPE
    cos_t = jnp.concatenate([jnp.ones((seq, MLA_NOPE), F32), cos, cos, jnp.zeros((seq, pad), F32)], axis=1)
    sin_t = jnp.concatenate([jnp.zeros((seq, MLA_NOPE), F32), sin, sin, jnp.zeros((seq, pad), F32)], axis=1)
    return cos_t, sin_t


def _t5_bucket(rel):
    nb = REL_BUCKETS // 2
    max_exact = nb // 2
    ret = jnp.where(rel > 0, nb, 0)
    n = jnp.abs(rel)
    nf = jnp.maximum(n, 1).astype(F32)
    large = max_exact + (jnp.log(nf / max_exact) / math.log(REL_MAX_DIST / max_exact) * (nb - max_exact)).astype(jnp.int32)
    large = jnp.minimum(large, nb - 1)
    return ret + jnp.where(n < max_exact, n, large)


def _band_bias(rel_table, block, window, dil, head0):
    off = np.arange(3 * block)[None, :] - block - np.arange(block)[:, None]
    bias = rel_table[_t5_bucket(jnp.asarray(off * dil))][..., head0:head0 + DIL_SLOTS]
    bias = jnp.where(jnp.asarray(np.abs(off) <= window)[..., None], bias.astype(F32), NEG_INF)
    return jnp.moveaxis(bias, -1, 0).reshape(DIL_SLOTS * block, 3 * block)


def _na_bias(rpb):
    qc = np.arange(GRID_W)[:, None]
    kc = np.arange(GRID_W)[None, :]
    sc = np.clip(qc - NA_COLS // 2, 0, GRID_W - NA_COLS)
    valid = (kc >= sc) & (kc < sc + NA_COLS)
    dc = np.clip(kc - qc, -(NA_COLS - 1), NA_COLS - 1) + NA_COLS - 1
    table = rpb.astype(F32)[:, :, jnp.asarray(dc)]
    table = jnp.where(jnp.asarray(valid)[None, None], table, NEG_INF)
    variants = [jnp.transpose(table[:, v:v + NA_ROWS], (0, 2, 1, 3)).reshape(NA_HEADS * GRID_W, NA_ROWS * GRID_W)
                for v in range(NA_ROWS)]
    return jnp.stack(variants)


def _tables(seq, attn_sink, na_rpb, rel_table):
    cos_t, sin_t = _rope_tables(seq)
    return {
        "cos": cos_t, "sin": sin_t,
        "dil_bias": [_band_bias(rel_table, DIL_BLOCK, window // (2 * dil), dil, gi * DIL_SLOTS)
                     for gi, (window, dil) in enumerate(DIL_PAIRS)],
        "swa_bias": _band_bias(rel_table, SWA_BLOCK, SWA_WINDOW, 1, DIL_HEADS),
        "sink": [jnp.repeat(attn_sink[i].astype(F32), SWA_BLOCK)[:, None] for i in range(DEPTH)],
        "na_bias": [_na_bias(na_rpb[i]) for i in range(DEPTH)],
    }


def _trunk(x, p, weights, tables, ln_final):
    bsz, seq, _ = x.shape
    tm = 512
    for i in range(DEPTH):
        w = weights[i]
        qa, ka, vt, zb0, zb1, zb2, zc, zd = _attn_in(x, w, tables["cos"], tables["sin"], tm)
        oa = _mla(qa, ka, vt, 1024, 2048)
        obs, lses = [], []
        for gi, zb in enumerate((zb0[:, None], zb1, zb2)):
            o, lse = _banded(zb, tables["dil_bias"][gi], None, blk=DIL_BLOCK, tm=512, with_lse=True)
            obs.append(o)
            lses.append(lse)
        (oc,) = _banded(zc[:, None], tables["swa_bias"], tables["sink"][i], blk=SWA_BLOCK, tm=512, with_lse=False)
        oc = oc.reshape(bsz, seq, BRANCH_W)
        od = _na(zd, tables["na_bias"][i], rows_per_step=4)
        x = _merge(x, w, oa, obs, lses, oc, od, tm)
        x = _ffn(x, w, tm)
        x = _ple(x, p, i, w, ln_final, tm, final=(i == DEPTH - 1))
    return x


def kernel(x_prompt, x_sample, p_prompt, p_sample, ln_attn, w_in, q_norm, kv_norm, w_q_up, w_kv_up,
           attn_sink, na_rpb, rel_table, w_gate, w_branch, w_out, ln_ffn, w_ffn_up, ffn_conv_w,
           ffn_conv_b, w_ffn_down, ln_ple, w_ple_gate, w_ple_proj, ln_final):
    weights = _prep_weights(ln_attn, w_in, q_norm, kv_norm, w_q_up, w_kv_up, w_gate, w_branch, w_out,
                            ln_ffn, w_ffn_up, ffn_conv_w, ffn_conv_b, w_ffn_down, ln_ple, w_ple_gate,
                            w_ple_proj)
    assert x_prompt.shape[1] == x_sample.shape[1]
    tables = _tables(x_prompt.shape[1], attn_sink, na_rpb, rel_table)
    ln_final = ln_final[None, :]
    return (_trunk(x_prompt, p_prompt, weights, tables, ln_final),
            _trunk(x_sample, p_sample, weights, tables, ln_final))
```

```python
import functools
import math

import jax
import jax.numpy as jnp
import numpy as np
from jax import lax
from jax.experimental import pallas as pl
from jax.experimental.pallas import tpu as pltpu

D_MODEL = 1024
DEPTH = 4
PLE_DIM = 256
GRID_W = 64
HEAD_DIM = 64
BRANCH_W = 256
N_BRANCH = 4
EPS = 1e-6
NEG_INF = -1e30

MLA_HEADS = 4
MLA_Q_RANK = 256
MLA_KV_RANK = 128
MLA_NOPE = 64
MLA_ROPE = 32
MLA_V = 64
ROPE_THETA = 10000.0
MLA_SLOT = 128
MLA_VROWS = 96
LANES = 128
LOG2_E = math.log2(math.e)

DIL_PAIRS = ((128, 1), (512, 4), (2048, 16))
DIL_SLOTS = 4
DIL_HEADS = DIL_SLOTS * len(DIL_PAIRS)
DIL_BLOCK = 64

SWA_Q_HEADS = 4
SWA_KV_HEADS = 2
SWA_WINDOW = 128
SWA_BLOCK = 128

NA_HEADS = 4
NA_ROWS = 8
NA_COLS = 16

REL_BUCKETS = 32
REL_MAX_DIST = 1024

D_FF = 2816
CONV_W = 3

A_COLS = MLA_Q_RANK + MLA_KV_RANK + MLA_ROPE
B_COLS = 3 * DIL_HEADS * HEAD_DIM
C_COLS = (SWA_Q_HEADS + 2 * SWA_KV_HEADS) * HEAD_DIM
D_COLS = 3 * NA_HEADS * HEAD_DIM
A_PAD = 512
C_EXP = 3 * BRANCH_W
MAIN_COLS = B_COLS + C_EXP + D_COLS
MAIN_CHUNK = 768

BF16 = jnp.bfloat16
F32 = jnp.float32

VMEM_LIMIT_BYTES = 56 * 1024 * 1024


def _params(*semantics):
    return pltpu.CompilerParams(dimension_semantics=semantics, vmem_limit_bytes=VMEM_LIMIT_BYTES)


def _resident(shape):
    return pl.BlockSpec(shape, lambda *_: (0,) * len(shape), pipeline_mode=pl.Buffered(1))


def _dot(a, b):
    return jnp.dot(a, b, preferred_element_type=F32)


def _dot_nt(a, b):
    return lax.dot_general(a, b, (((1,), (1,)), ((), ())), preferred_element_type=F32)


def _rms(x, g):
    return x * lax.rsqrt(jnp.mean(x * x, axis=-1, keepdims=True) + EPS) * g


def _split_residues(val, scr, out_ref, dil):
    tm = val.shape[0]
    nslab = val.shape[1] // LANES
    for j in range(nslab):
        scr[j] = val[:, j * LANES:(j + 1) * LANES]
    for r in range(dil):
        for j in range(nslab):
            out_ref[r, :, j * LANES:(j + 1) * LANES] = scr[j, pl.ds(r, tm // dil, stride=dil), :].astype(BF16)


def _attn_in_kernel(x_ref, g_ref, wmain_ref, wa_ref, qn_ref, kvn_ref, wq_ref, wqs_ref, wk_ref,
                    wvt_ref, e_ref, es_ref, cos_ref, sin_ref,
                    qa_ref, ka_ref, vt_ref, zb0_ref, zb1_ref, zb2_ref, zc_ref, zd_ref, scr1, scr2):
    hb = _rms(x_ref[...], g_ref[...]).astype(BF16)
    chunk = lambda c: _dot(hb, wmain_ref[:, c * MAIN_CHUNK:(c + 1) * MAIN_CHUNK])
    zb0_ref[...] = chunk(0).astype(BF16)
    _split_residues(chunk(1), scr1, zb1_ref, DIL_PAIRS[1][1])
    _split_residues(chunk(2), scr2, zb2_ref, DIL_PAIRS[2][1])
    zc_ref[...] = chunk(3).astype(BF16)
    zd_ref[...] = chunk(4).astype(BF16)

    za = _dot(hb, wa_ref[...])
    cq = _rms(za[:, :MLA_Q_RANK], qn_ref[...]).astype(BF16)
    ckv = _rms(za[:, MLA_Q_RANK:MLA_Q_RANK + MLA_KV_RANK], kvn_ref[...]).astype(BF16)
    kr = za[:, MLA_Q_RANK + MLA_KV_RANK:]
    kr_hi = kr.astype(BF16)
    kr_lo = (kr - kr_hi.astype(F32)).astype(BF16)
    cos = jnp.concatenate([cos_ref[...]] * MLA_HEADS, axis=1)
    sin = jnp.concatenate([sin_ref[...]] * MLA_HEADS, axis=1)
    q = _dot(cq, wq_ref[...]) * cos + _dot(cq, wqs_ref[...]) * sin
    qa_ref[...] = (q * (LOG2_E * (MLA_NOPE + MLA_ROPE) ** -0.5)).astype(BF16)
    k_rope = _dot(kr_hi, e_ref[...]) + _dot(kr_lo, e_ref[...])
    k_swap = _dot(kr_hi, es_ref[...]) + _dot(kr_lo, es_ref[...])
    ka_ref[...] = ((_dot(ckv, wk_ref[...]) + k_rope) * cos + k_swap * sin).astype(BF16)
    vt = _dot_nt(wvt_ref[...], ckv)
    row = lax.broadcasted_iota(jnp.int32, vt.shape, 0)
    vt_ref[...] = jnp.where(row % MLA_VROWS == MLA_V, 1.0, vt).astype(BF16)


def _attn_in(x, w, cos_t, sin_t, tm):
    bsz, seq, _ = x.shape
    tok = lambda width: pl.BlockSpec((None, tm, width), lambda b, i: (b, i, 0))
    table = pl.BlockSpec((tm, MLA_SLOT), lambda b, i: (i, 0))
    group = 3 * BRANCH_W
    slot_w = MLA_HEADS * MLA_SLOT
    out_specs = [tok(slot_w), tok(slot_w),
                 pl.BlockSpec((None, MLA_HEADS * MLA_VROWS, tm), lambda b, i: (b, 0, i)), tok(group)]
    out_shape = [jax.ShapeDtypeStruct((bsz, seq, slot_w), BF16), jax.ShapeDtypeStruct((bsz, seq, slot_w), BF16),
                 jax.ShapeDtypeStruct((bsz, MLA_HEADS * MLA_VROWS, seq), BF16),
                 jax.ShapeDtypeStruct((bsz, seq, group), BF16)]
    scratch = []
    for _, dil in DIL_PAIRS[1:]:
        out_specs.append(pl.BlockSpec((None, dil, tm // dil, group), lambda b, i: (b, 0, i, 0)))
        out_shape.append(jax.ShapeDtypeStruct((bsz, dil, seq // dil, group), BF16))
        scratch.append(pltpu.VMEM((group // LANES, tm, LANES), F32))
    out_specs += [tok(C_EXP), tok(D_COLS)]
    out_shape += [jax.ShapeDtypeStruct((bsz, seq, C_EXP), BF16), jax.ShapeDtypeStruct((bsz, seq, D_COLS), BF16)]
    return pl.pallas_call(
        _attn_in_kernel,
        grid=(bsz, seq // tm),
        in_specs=[tok(D_MODEL), _resident((1, D_MODEL)), _resident((D_MODEL, MAIN_COLS)),
                  _resident((D_MODEL, A_PAD)), _resident((1, MLA_Q_RANK)), _resident((1, MLA_KV_RANK)),
                  _resident((MLA_Q_RANK, slot_w)), _resident((MLA_Q_RANK, slot_w)),
                  _resident((MLA_KV_RANK, slot_w)), _resident((MLA_HEADS * MLA_VROWS, MLA_KV_RANK)),
                  _resident((MLA_SLOT, slot_w)), _resident((MLA_SLOT, slot_w)),
                  table, table],
        out_specs=out_specs, out_shape=out_shape, scratch_shapes=scratch,
        compiler_params=_params("parallel", "parallel"),
        name="attn_in",
    )(x, w["ln_attn"], w["w_main"], w["w_a"], w["q_norm"], w["kv_norm"], w["wq"], w["wq_s"],
      w["wk"], w["wv_t"], w["e"], w["e_s"], cos_t, sin_t)


def _mla_kernel(q_ref, k_ref, vt_ref, o_ref, m_scr, acc_scr):
    kv = pl.program_id(2)

    @pl.when(kv == 0)
    def _():
        m_scr[...] = jnp.full(m_scr.shape, NEG_INF, F32)
        acc_scr[...] = jnp.zeros(acc_scr.shape, F32)

    def scores(h):
        return _dot_nt(k_ref[:, h * MLA_SLOT:(h + 1) * MLA_SLOT], q_ref[:, h * MLA_SLOT:(h + 1) * MLA_SLOT])

    s_next = scores(0)
    for h in range(MLA_HEADS):
        s = s_next
        if h + 1 < MLA_HEADS:
            s_next = scores(h + 1)
        m_prev = m_scr[h]
        m_new = jnp.maximum(m_prev, jnp.max(s, axis=0, keepdims=True))
        p = jnp.exp2(s - m_new).astype(BF16)
        acc_scr[h] = jnp.exp2(m_prev - m_new) * acc_scr[h] + _dot(vt_ref[h * MLA_VROWS:(h + 1) * MLA_VROWS, :], p)
        m_scr[h] = m_new

    @pl.when(kv == pl.num_programs(2) - 1)
    def _():
        o_t = jnp.concatenate([acc_scr[h, :MLA_V, :] / acc_scr[h, MLA_V:MLA_V + 1, :] for h in range(MLA_HEADS)],
                              axis=0)
        o_ref[...] = o_t.T.astype(BF16)


def _mla(qa, ka, vt, tq, tk):
    bsz, seq, _ = qa.shape
    tq, tk = min(tq, seq), min(tk, seq)
    return pl.pallas_call(
        _mla_kernel,
        grid=(bsz, seq // tq, seq // tk),
        in_specs=[pl.BlockSpec((None, tq, MLA_HEADS * MLA_SLOT), lambda b, i, j: (b, i, 0)),
                  pl.BlockSpec((None, tk, MLA_HEADS * MLA_SLOT), lambda b, i, j: (b, j, 0)),
                  pl.BlockSpec((None, MLA_HEADS * MLA_VROWS, tk), lambda b, i, j: (b, 0, j))],
        out_specs=pl.BlockSpec((None, tq, MLA_HEADS * MLA_V), lambda b, i, j: (b, i, 0)),
        out_shape=jax.ShapeDtypeStruct((bsz, seq, MLA_HEADS * MLA_V), BF16),
        scratch_shapes=[pltpu.VMEM((MLA_HEADS, 1, tq), F32), pltpu.VMEM((MLA_HEADS, MLA_VROWS, tq), F32)],
        compiler_params=_params("parallel", "parallel", "arbitrary"),
        name="mla_attention",
    )(qa, ka, vt)


def _head_rows(x, nheads):
    head = lax.broadcasted_iota(jnp.int32, x.shape, 1) // HEAD_DIM
    xf = x.astype(F32)
    return jnp.concatenate([jnp.where(head == h, xf, 0.0) for h in range(nheads)], axis=0).astype(x.dtype)


def _head_cols(x, nheads):
    n = x.shape[0] // nheads
    head = lax.broadcasted_iota(jnp.int32, (n, x.shape[1]), 1) // HEAD_DIM
    out = x[:n]
    for h in range(1, nheads):
        out = jnp.where(head == h, x[h * n:(h + 1) * n], out)
    return out


BAND_UNIT = 128
ONES_ROWS = 16
LN_2 = math.log(2.0)


def _banded_kernel(*refs, halo, tm, seq_len, tile_axis, with_sink, with_lse):
    q_ref, kp_ref, km_ref, kn_ref, vp_ref, vm_ref, vn_ref, bias_ref = refs[:8]
    refs = refs[8:]
    sink_ref = None
    if with_sink:
        sink_ref, refs = refs[0], refs[1:]
    o_ref = refs[0]
    lse_ref = refs[1] if with_lse else None

    tile = pl.program_id(tile_axis)
    kcat = jnp.concatenate([kp_ref[...], km_ref[...], kn_ref[...]], axis=0)
    vcat = jnp.concatenate([vp_ref[...], vm_ref[...], vn_ref[...]], axis=0)
    vt = vcat.astype(F32).T
    ones = jnp.where(lax.broadcasted_iota(jnp.int32, (ONES_ROWS, vt.shape[1]), 0) == 0, 1.0, 0.0)
    vt = jnp.concatenate([vt, ones], axis=0).astype(BF16)
    nk = BAND_UNIT + 2 * halo
    nunit = tm // BAND_UNIT
    lanes = lambda x, h: x[:, h * BAND_UNIT:(h + 1) * BAND_UNIT]

    def scores(u):
        qbd = _head_rows(q_ref[u * BAND_UNIT:(u + 1) * BAND_UNIT, :], DIL_SLOTS)
        return _dot_nt(kcat[u * BAND_UNIT:u * BAND_UNIT + nk], qbd)

    raw = [scores(u) for u in range(nunit)]
    outs = []
    for u in range(nunit):
        s = raw[u] * (LOG2_E * HEAD_DIM ** -0.5) + bias_ref[...]
        if u == 0 or u == nunit - 1:
            kpos = tile * tm + u * BAND_UNIT - halo + lax.broadcasted_iota(jnp.int32, s.shape, 0)
            if u == 0:
                s = jnp.where(kpos >= 0, s, NEG_INF)
            if u == nunit - 1:
                s = jnp.where(kpos < seq_len, s, NEG_INF)
        m = jnp.max(s, axis=0, keepdims=True)
        if with_sink:
            m = jnp.maximum(m, sink_ref[...])
        p = jnp.exp2(s - m).astype(BF16)
        acc = _dot(vt[:, u * BAND_UNIT:u * BAND_UNIT + nk], p)
        l = acc[DIL_SLOTS * HEAD_DIM:DIL_SLOTS * HEAD_DIM + 1]
        if with_sink:
            l = l + jnp.exp2(sink_ref[...] - m)
        inv = 1.0 / l
        out_t = [acc[h * HEAD_DIM:(h + 1) * HEAD_DIM, h * BAND_UNIT:(h + 1) * BAND_UNIT] * lanes(inv, h)
                 for h in range(DIL_SLOTS)]
        if with_lse:
            lse = m * LN_2 + jnp.log(l)
            out_t += [jnp.broadcast_to(lanes(lse, h), (HEAD_DIM, BAND_UNIT)) for h in range(DIL_SLOTS)]
        outs.append(jnp.concatenate(out_t, axis=0))
    for u in range(nunit):
        rows = slice(u * BAND_UNIT, (u + 1) * BAND_UNIT)
        out = outs[u].T
        o_ref[rows, :] = out[:, :BRANCH_W].astype(BF16)
        if with_lse:
            lse_ref[rows, :] = out[:, BRANCH_W:]


def _banded(z, bias, sink, *, halo, tm, with_lse):
    bsz, dil, sub, _ = z.shape
    tm = min(tm, sub)
    assert tm % BAND_UNIT == 0 and BAND_UNIT % halo == 0
    blk = halo
    per_tile = tm // blk
    last_blk = sub // blk - 1
    nk = BAND_UNIT + 2 * halo

    def main(j):
        return pl.BlockSpec((None, None, tm, BRANCH_W), lambda b, r, i: (b, r, i, j))

    def prev(j):
        return pl.BlockSpec((None, None, blk, BRANCH_W),
                            lambda b, r, i: (b, r, jnp.maximum(i * per_tile - 1, 0), j))

    def nxt(j):
        return pl.BlockSpec((None, None, blk, BRANCH_W),
                            lambda b, r, i: (b, r, jnp.minimum((i + 1) * per_tile, last_blk), j))

    in_specs = [main(0), prev(1), main(1), nxt(1), prev(2), main(2), nxt(2),
                _resident((nk, DIL_SLOTS * BAND_UNIT))]
    args = [z, z, z, z, z, z, z, bias]
    if sink is not None:
        in_specs.append(_resident((1, DIL_SLOTS * BAND_UNIT)))
        args.append(sink)
    out_spec = pl.BlockSpec((None, None, tm, BRANCH_W), lambda b, r, i: (b, r, i, 0))
    out_specs = [out_spec]
    out_shape = [jax.ShapeDtypeStruct((bsz, dil, sub, BRANCH_W), BF16)]
    if with_lse:
        out_specs.append(out_spec)
        out_shape.append(jax.ShapeDtypeStruct((bsz, dil, sub, BRANCH_W), F32))
    return pl.pallas_call(
        functools.partial(_banded_kernel, halo=halo, tm=tm, seq_len=sub, tile_axis=2,
                          with_sink=sink is not None, with_lse=with_lse),
        grid=(bsz, dil, sub // tm),
        in_specs=in_specs, out_specs=out_specs, out_shape=out_shape,
        compiler_params=_params("parallel", "parallel", "parallel"),
        name=f"banded_d{dil}_h{halo}",
    )(*args)


def _na_kernel(q_ref, k_ref, v_ref, bias_ref, o_ref, *, rows, rows_per_step):
    step = pl.program_id(1)
    nkeys = NA_ROWS * GRID_W

    def window(rr):
        r = step * rows_per_step + rr
        sr = jnp.clip(r - NA_ROWS // 2, 0, rows - NA_ROWS)
        return sr - r + NA_ROWS - 1, pl.multiple_of(sr * GRID_W, GRID_W)

    def scores(rr):
        qbd = _head_rows(q_ref[rr * GRID_W:(rr + 1) * GRID_W, :], NA_HEADS)
        return _dot_nt(qbd, k_ref[pl.ds(wins[rr][1], nkeys), :])

    wins = [window(rr) for rr in range(rows_per_step)]
    raw = [scores(rr) for rr in range(rows_per_step)]
    for rr in range(rows_per_step):
        variant, start = wins[rr]
        s = raw[rr] * (HEAD_DIM ** -0.5) + bias_ref[variant]
        p = jnp.exp(s - jnp.max(s, axis=1, keepdims=True))
        obd = _dot(p.astype(BF16), v_ref[pl.ds(start, nkeys), :]) / jnp.sum(p, axis=1, keepdims=True)
        o_ref[rr * GRID_W:(rr + 1) * GRID_W, :] = _head_cols(obd, NA_HEADS).astype(BF16)


def _na(zd, bias, rows_per_step):
    bsz, seq, _ = zd.shape
    rows = seq // GRID_W
    assert rows >= NA_ROWS and rows % rows_per_step == 0
    tq = rows_per_step * GRID_W
    whole = lambda j: pl.BlockSpec((None, seq, BRANCH_W), lambda b, i: (b, 0, j))
    return pl.pallas_call(
        functools.partial(_na_kernel, rows=rows, rows_per_step=rows_per_step),
        grid=(bsz, rows // rows_per_step),
        in_specs=[pl.BlockSpec((None, tq, BRANCH_W), lambda b, i: (b, i, 0)), whole(1), whole(2),
                  _resident((NA_ROWS, NA_HEADS * GRID_W, NA_ROWS * GRID_W))],
        out_specs=pl.BlockSpec((None, tq, BRANCH_W), lambda b, i: (b, i, 0)),
        out_shape=jax.ShapeDtypeStruct((bsz, seq, BRANCH_W), BF16),
        compiler_params=_params("parallel", "arbitrary"),
        name="neighborhood_attention",
    )(zd, zd, zd, bias)


def _join_residues(ref, scr):
    dil, rows, width = ref.shape
    for j in range(width // LANES):
        for r in range(dil):
            scr[j, pl.ds(r, rows, stride=dil), :] = ref[r, :, j * LANES:(j + 1) * LANES].astype(F32)
    return jnp.concatenate([scr[j] for j in range(width // LANES)], axis=1)


def _merge_kernel(x_ref, g_ref, oa_ref, ob0_ref, ob1_ref, ob2_ref, l0_ref, l1_ref, l2_ref,
                  oc_ref, od_ref, wg_ref, wb_ref, wo_ref, out_ref, *scratch):
    x = x_ref[...]
    hb = _rms(x, g_ref[...]).astype(BF16)
    outs = (ob0_ref[0].astype(F32), _join_residues(ob1_ref, scratch[0]), _join_residues(ob2_ref, scratch[1]))
    lses = (l0_ref[0], _join_residues(l1_ref, scratch[2]), _join_residues(l2_ref, scratch[3]))
    top = jnp.maximum(jnp.maximum(lses[0], lses[1]), lses[2])
    ws = [jnp.exp(l - top) for l in lses]
    den = ws[0] + ws[1] + ws[2]
    ob = sum((wgt / den) * o for wgt, o in zip(ws, outs))
    branches = (oa_ref[...], ob.astype(BF16), oc_ref[...], od_ref[...])
    merged = None
    for j, o in enumerate(branches):
        term = jax.nn.sigmoid(_dot(hb, wg_ref[j])) * _dot(o, wb_ref[j])
        merged = term if merged is None else merged + term
    out_ref[...] = x + _dot(merged.astype(BF16), wo_ref[...])


def _merge(x, w, oa, obs, lses, oc, od, tm):
    bsz, seq, _ = x.shape
    tok = lambda width: pl.BlockSpec((None, tm, width), lambda b, i: (b, i, 0))
    residues = [pl.BlockSpec((None, dil, tm // dil, BRANCH_W), lambda b, i: (b, 0, i, 0)) for _, dil in DIL_PAIRS]
    return pl.pallas_call(
        _merge_kernel,
        grid=(bsz, seq // tm),
        in_specs=[tok(D_MODEL), _resident((1, D_MODEL)), tok(BRANCH_W)] + residues + residues
                 + [tok(BRANCH_W), tok(BRANCH_W),
                    _resident((N_BRANCH, D_MODEL, D_MODEL)), _resident((N_BRANCH, BRANCH_W, D_MODEL)),
                    _resident((D_MODEL, D_MODEL))],
        out_specs=tok(D_MODEL),
        out_shape=jax.ShapeDtypeStruct(x.shape, F32),
        scratch_shapes=[pltpu.VMEM((BRANCH_W // LANES, tm, LANES), F32)] * 4,
        compiler_params=_params("parallel", "parallel"),
        name="merge",
    )(x, w["ln_attn"], oa, *obs, *lses, oc, od, w["w_gate"], w["w_branch"], w["w_out"])


SUBLANES = 8
GELU_C0 = math.sqrt(2.0 / math.pi)
GELU_C1 = GELU_C0 * 0.044715


def _ffn_kernel(x_ref, xp_ref, xn_ref, g_ref, wu_ref, cw_ref, cb_ref, wd_ref, out_ref, *, tm):
    tile = pl.program_id(1)
    g = g_ref[...]
    x = x_ref[...]
    hn = _rms(x, g).astype(BF16)
    keep_prev = jnp.where(tile > 0, 1.0, 0.0)
    keep_next = jnp.where(tile < pl.num_programs(1) - 1, 1.0, 0.0)
    halo = jnp.concatenate([_rms(xp_ref[...], g) * keep_prev, _rms(xn_ref[...], g) * keep_next], axis=0)
    wu = wu_ref[...]
    u = _dot(hn, wu)
    uh = _dot(halo.astype(BF16), wu)
    row = lax.broadcasted_iota(jnp.int32, (SUBLANES, u.shape[1]), 0)
    below = pltpu.roll(u, 1, 0)
    below = jnp.concatenate([jnp.where(row == 0, uh[SUBLANES - 1:SUBLANES], below[:SUBLANES]), below[SUBLANES:]], axis=0)
    above = pltpu.roll(u, tm - 1, 0)
    above = jnp.concatenate([above[:tm - SUBLANES],
                             jnp.where(row == SUBLANES - 1, uh[SUBLANES:SUBLANES + 1], above[tm - SUBLANES:])], axis=0)
    cw = cw_ref[...]
    y = below * cw[0:1] + u * cw[1:2] + above * cw[2:3] + cb_ref[...]
    a, half_b = y[:, :D_FF], y[:, D_FF:]
    act = a * (1.0 + jnp.tanh(a * (GELU_C0 + GELU_C1 * (a * a)))) * half_b
    out_ref[...] = x + _dot(act.astype(BF16), wd_ref[...])


def _ffn(x, w, tm):
    bsz, seq, _ = x.shape
    per_tile = tm // SUBLANES
    last = seq // SUBLANES - 1
    tok = pl.BlockSpec((None, tm, D_MODEL), lambda b, i: (b, i, 0))
    return pl.pallas_call(
        functools.partial(_ffn_kernel, tm=tm),
        grid=(bsz, seq // tm),
        in_specs=[tok,
                  pl.BlockSpec((None, SUBLANES, D_MODEL), lambda b, i: (b, jnp.maximum(i * per_tile - 1, 0), 0)),
                  pl.BlockSpec((None, SUBLANES, D_MODEL), lambda b, i: (b, jnp.minimum((i + 1) * per_tile, last), 0)),
                  _resident((1, D_MODEL)), _resident((D_MODEL, 2 * D_FF)), _resident((CONV_W, 2 * D_FF)),
                  _resident((1, 2 * D_FF)), _resident((D_FF, D_MODEL))],
        out_specs=tok,
        out_shape=jax.ShapeDtypeStruct(x.shape, F32),
        compiler_params=_params("parallel", "parallel"),
        name="conv_ffn",
    )(x, x, x, w["ln_ffn"], w["w_ffn_up"], w["ffn_conv_w"], w["ffn_conv_b"], w["w_ffn_down"])


def _ple_kernel(x_ref, p_ref, g_ref, wg_ref, wp_ref, gf_ref, out_ref, *, final):
    x = x_ref[...]
    gate = jax.nn.sigmoid(_dot(_rms(x, g_ref[...]).astype(BF16), wg_ref[...]))
    y = x + gate * _dot(p_ref[...].astype(BF16), wp_ref[...])
    if final:
        y = _rms(y, gf_ref[...])
    out_ref[...] = y


def _ple(x, p, layer, w, ln_final, tm, final):
    bsz, seq, _ = x.shape
    tok = lambda width: pl.BlockSpec((None, tm, width), lambda b, i: (b, i, 0))
    return pl.pallas_call(
        functools.partial(_ple_kernel, final=final),
        grid=(bsz, seq // tm),
        in_specs=[tok(D_MODEL), pl.BlockSpec((None, None, tm, PLE_DIM), lambda b, i: (layer, b, i, 0)),
                  _resident((1, D_MODEL)), _resident((D_MODEL, D_MODEL)),
                  _resident((PLE_DIM, D_MODEL)), _resident((1, D_MODEL))],
        out_specs=tok(D_MODEL),
        out_shape=jax.ShapeDtypeStruct(x.shape, F32),
        compiler_params=_params("parallel", "parallel"),
        name="ple_final" if final else "ple",
    )(x, p, w["ln_ple"], w["w_ple_gate"], w["w_ple_proj"], ln_final)


def _gather_cols(wmat, cols, sign=None):
    cols = np.asarray(cols)
    picked = jnp.take(wmat, jnp.asarray(np.maximum(cols, 0)), axis=-1)
    scale = (cols >= 0).astype(np.float32) * (1.0 if sign is None else np.asarray(sign, np.float32))
    return picked * jnp.asarray(scale)


def _prep_weights(ln_attn, w_in, q_norm, kv_norm, w_q_up, w_kv_up, w_gate, w_branch, w_out, ln_ffn,
                  w_ffn_up, ffn_conv_w, ffn_conv_b, w_ffn_down, ln_ple, w_ple_gate, w_ple_proj):
    half = MLA_ROPE // 2
    qk_dim = MLA_NOPE + MLA_ROPE
    slot_w = MLA_HEADS * MLA_SLOT

    c0 = A_COLS + B_COLS
    nq = SWA_Q_HEADS * HEAD_DIM
    nk = SWA_KV_HEADS * HEAD_DIM

    def per_query_head(base):
        heads = np.repeat(np.arange(SWA_KV_HEADS), SWA_Q_HEADS // SWA_KV_HEADS)
        return [w_in[..., base + h * HEAD_DIM:base + (h + 1) * HEAD_DIM] for h in heads]

    w_main = jnp.concatenate([w_in[..., A_COLS:c0 + nq]] + per_query_head(c0 + nq) + per_query_head(c0 + nq + nk)
                             + [w_in[..., c0 + C_COLS:]], axis=-1)
    w_a = jnp.pad(w_in[..., :A_COLS], ((0, 0), (0, 0), (0, A_PAD - A_COLS)))

    q_cols = np.full((slot_w,), -1)
    qs_cols = np.full((slot_w,), -1)
    qs_sign = np.ones((slot_w,), np.float32)
    k_cols = np.full((slot_w,), -1)
    e_rows = np.full((slot_w,), -1)
    es_rows = np.full((slot_w,), -1)
    es_sign = np.ones((slot_w,), np.float32)
    for h in range(MLA_HEADS):
        base = h * MLA_SLOT
        q_cols[base:base + qk_dim] = h * qk_dim + np.arange(qk_dim)
        k_cols[base:base + MLA_NOPE] = h * (MLA_NOPE + MLA_V) + np.arange(MLA_NOPE)
        rope0 = base + MLA_NOPE
        qs_cols[rope0:rope0 + half] = h * qk_dim + MLA_NOPE + half + np.arange(half)
        qs_sign[rope0:rope0 + half] = -1.0
        qs_cols[rope0 + half:rope0 + MLA_ROPE] = h * qk_dim + MLA_NOPE + np.arange(half)
        e_rows[rope0:rope0 + MLA_ROPE] = np.arange(MLA_ROPE)
        es_rows[rope0:rope0 + half] = half + np.arange(half)
        es_sign[rope0:rope0 + half] = -1.0
        es_rows[rope0 + half:rope0 + MLA_ROPE] = np.arange(half)
    v_cols = np.full((MLA_HEADS, MLA_VROWS), -1)
    v_cols[:, :MLA_V] = np.arange(MLA_HEADS)[:, None] * (MLA_NOPE + MLA_V) + MLA_NOPE + np.arange(MLA_V)
    v_cols = v_cols.reshape(-1)
    eye = jnp.eye(MLA_SLOT, dtype=F32)

    row = lambda a: a[:, None, :]
    gate_half = jnp.asarray(np.concatenate([np.ones(D_FF, np.float32), np.full(D_FF, 0.5, np.float32)]))
    stacked = {
        "ln_attn": row(ln_attn), "q_norm": row(q_norm), "kv_norm": row(kv_norm),
        "w_main": w_main.astype(BF16),
        "w_a": w_a.astype(BF16),
        "wq": _gather_cols(w_q_up, q_cols).astype(BF16),
        "wq_s": _gather_cols(w_q_up, qs_cols, qs_sign).astype(BF16),
        "wk": _gather_cols(w_kv_up, k_cols).astype(BF16),
        "wv_t": jnp.swapaxes(_gather_cols(w_kv_up, v_cols), -1, -2).astype(BF16),
        "w_gate": w_gate.astype(BF16), "w_branch": w_branch.astype(BF16), "w_out": w_out.astype(BF16),
        "ln_ffn": row(ln_ffn), "w_ffn_up": w_ffn_up.astype(BF16), "ffn_conv_w": ffn_conv_w * gate_half,
        "ffn_conv_b": row(ffn_conv_b * gate_half), "w_ffn_down": w_ffn_down.astype(BF16),
        "ln_ple": row(ln_ple), "w_ple_gate": w_ple_gate.astype(BF16), "w_ple_proj": w_ple_proj.astype(BF16),
    }
    shared = {
        "e": _gather_cols(eye, e_rows).astype(BF16),
        "e_s": _gather_cols(eye, es_rows, es_sign).astype(BF16),
    }
    return [dict({k: v[i] for k, v in stacked.items()}, **shared) for i in range(DEPTH)]


def _rope_tables(seq):
    inv = ROPE_THETA ** (-jnp.arange(0, MLA_ROPE, 2, dtype=F32) / MLA_ROPE)
    ang = jnp.arange(seq, dtype=F32)[:, None] * inv[None]
    cos, sin = jnp.cos(ang), jnp.sin(ang)
    pad = MLA_SLOT - MLA_NOPE - MLA_ROPE
    cos_t = jnp.concatenate([jnp.ones((seq, MLA_NOPE), F32), cos, cos, jnp.zeros((seq, pad), F32)], axis=1)
    sin_t = jnp.concatenate([jnp.zeros((seq, MLA_NOPE), F32), sin, sin, jnp.zeros((seq, pad), F32)], axis=1)
    return cos_t, sin_t


def _t5_bucket(rel):
    nb = REL_BUCKETS // 2
    max_exact = nb // 2
    ret = jnp.where(rel > 0, nb, 0)
    n = jnp.abs(rel)
    nf = jnp.maximum(n, 1).astype(F32)
    large = max_exact + (jnp.log(nf / max_exact) / math.log(REL_MAX_DIST / max_exact) * (nb - max_exact)).astype(jnp.int32)
    large = jnp.minimum(large, nb - 1)
    return ret + jnp.where(n < max_exact, n, large)


def _band_bias(rel_table, window, dil, head0):
    off = np.arange(BAND_UNIT + 2 * window)[:, None] - window - np.arange(BAND_UNIT)[None, :]
    bias = rel_table[_t5_bucket(jnp.asarray(off * dil))][..., head0:head0 + DIL_SLOTS]
    bias = jnp.where(jnp.asarray(np.abs(off) <= window)[..., None], bias.astype(F32) * LOG2_E, NEG_INF)
    return jnp.swapaxes(bias, 1, 2).reshape(off.shape[0], DIL_SLOTS * BAND_UNIT)


def _na_bias(rpb):
    qc = np.arange(GRID_W)[:, None]
    kc = np.arange(GRID_W)[None, :]
    sc = np.clip(qc - NA_COLS // 2, 0, GRID_W - NA_COLS)
    valid = (kc >= sc) & (kc < sc + NA_COLS)
    dc = np.clip(kc - qc, -(NA_COLS - 1), NA_COLS - 1) + NA_COLS - 1
    table = rpb.astype(F32)[:, :, jnp.asarray(dc)]
    table = jnp.where(jnp.asarray(valid)[None, None], table, NEG_INF)
    variants = [jnp.transpose(table[:, v:v + NA_ROWS], (0, 2, 1, 3)).reshape(NA_HEADS * GRID_W, NA_ROWS * GRID_W)
                for v in range(NA_ROWS)]
    return jnp.stack(variants)


def _tables(seq, attn_sink, na_rpb, rel_table):
    cos_t, sin_t = _rope_tables(seq)
    return {
        "cos": cos_t, "sin": sin_t,
        "dil_bias": [_band_bias(rel_table, window // (2 * dil), dil, gi * DIL_SLOTS)
                     for gi, (window, dil) in enumerate(DIL_PAIRS)],
        "swa_bias": _band_bias(rel_table, SWA_WINDOW, 1, DIL_HEADS),
        "sink": [jnp.repeat(attn_sink[i].astype(F32) * LOG2_E, BAND_UNIT)[None, :] for i in range(DEPTH)],
        "na_bias": [_na_bias(na_rpb[i]) for i in range(DEPTH)],
    }


def _trunk(x, p, weights, tables, ln_final):
    bsz, seq, _ = x.shape
    tm = 512
    for i in range(DEPTH):
        w = weights[i]
        qa, ka, vt, zb0, zb1, zb2, zc, zd = _attn_in(x, w, tables["cos"], tables["sin"], tm)
        oa = _mla(qa, ka, vt, 1024, 2048)
        obs, lses = [], []
        for gi, zb in enumerate((zb0[:, None], zb1, zb2)):
            halo = DIL_PAIRS[gi][0] // (2 * DIL_PAIRS[gi][1])
            o, lse = _banded(zb, tables["dil_bias"][gi], None, halo=halo, tm=1024, with_lse=True)
            obs.append(o)
            lses.append(lse)
        (oc,) = _banded(zc[:, None], tables["swa_bias"], tables["sink"][i], halo=SWA_WINDOW, tm=1024, with_lse=False)
        oc = oc.reshape(bsz, seq, BRANCH_W)
        od = _na(zd, tables["na_bias"][i], rows_per_step=8)
        x = _merge(x, w, oa, obs, lses, oc, od, tm)
        x = _ffn(x, w, tm)
        x = _ple(x, p, i, w, ln_final, tm, final=(i == DEPTH - 1))
    return x


def kernel(x_prompt, x_sample, p_prompt, p_sample, ln_attn, w_in, q_norm, kv_norm, w_q_up, w_kv_up,
           attn_sink, na_rpb, rel_table, w_gate, w_branch, w_out, ln_ffn, w_ffn_up, ffn_conv_w,
           ffn_conv_b, w_ffn_down, ln_ple, w_ple_gate, w_ple_proj, ln_final):
    weights = _prep_weights(ln_attn, w_in, q_norm, kv_norm, w_q_up, w_kv_up, w_gate, w_branch, w_out,
                            ln_ffn, w_ffn_up, ffn_conv_w, ffn_conv_b, w_ffn_down, ln_ple, w_ple_gate,
                            w_ple_proj)
    assert x_prompt.shape[1] == x_sample.shape[1]
    tables = _tables(x_prompt.shape[1], attn_sink, na_rpb, rel_table)
    ln_final = ln_final[None, :]
    return (_trunk(x_prompt, p_prompt, weights, tables, ln_final),
            _trunk(x_sample, p_sample, weights, tables, ln_final))
```

```python
import functools
import math

import jax
import jax.numpy as jnp
import numpy as np
from jax import lax
from jax.experimental import pallas as pl
from jax.experimental.pallas import tpu as pltpu

D_MODEL = 1024
DEPTH = 4
PLE_DIM = 256
GRID_W = 64
HEAD_DIM = 64
BRANCH_W = 256
N_BRANCH = 4
EPS = 1e-6
NEG_INF = -1e30

MLA_HEADS = 4
MLA_Q_RANK = 256
MLA_KV_RANK = 128
MLA_NOPE = 64
MLA_ROPE = 32
MLA_V = 64
ROPE_THETA = 10000.0
MLA_SLOT = 128
MLA_VROWS = 96
LANES = 128
LOG2_E = math.log2(math.e)

DIL_PAIRS = ((128, 1), (512, 4), (2048, 16))
DIL_SLOTS = 4
DIL_HEADS = DIL_SLOTS * len(DIL_PAIRS)
DIL_BLOCK = 64

SWA_Q_HEADS = 4
SWA_KV_HEADS = 2
SWA_WINDOW = 128
SWA_BLOCK = 128

NA_HEADS = 4
NA_ROWS = 8
NA_COLS = 16

REL_BUCKETS = 32
REL_MAX_DIST = 1024

D_FF = 2816
CONV_W = 3

A_COLS = MLA_Q_RANK + MLA_KV_RANK + MLA_ROPE
B_COLS = 3 * DIL_HEADS * HEAD_DIM
C_COLS = (SWA_Q_HEADS + 2 * SWA_KV_HEADS) * HEAD_DIM
D_COLS = 3 * NA_HEADS * HEAD_DIM
A_PAD = 512
C_EXP = 3 * BRANCH_W
MAIN_COLS = B_COLS + C_EXP + D_COLS
MAIN_CHUNK = 768

BF16 = jnp.bfloat16
F32 = jnp.float32

VMEM_LIMIT_BYTES = 56 * 1024 * 1024


def _params(*semantics):
    return pltpu.CompilerParams(dimension_semantics=semantics, vmem_limit_bytes=VMEM_LIMIT_BYTES)


def _resident(shape):
    return pl.BlockSpec(shape, lambda *_: (0,) * len(shape), pipeline_mode=pl.Buffered(1))


def _dot(a, b):
    return jnp.dot(a, b, preferred_element_type=F32)


def _dot_nt(a, b):
    return lax.dot_general(a, b, (((1,), (1,)), ((), ())), preferred_element_type=F32)


def _rms(x, g):
    return x * lax.rsqrt(jnp.mean(x * x, axis=-1, keepdims=True) + EPS) * g


def _split_residues(val, scr, out_ref, dil):
    tm = val.shape[0]
    nslab = val.shape[1] // LANES
    for j in range(nslab):
        scr[j] = val[:, j * LANES:(j + 1) * LANES]
    for r in range(dil):
        for j in range(nslab):
            out_ref[r, :, j * LANES:(j + 1) * LANES] = scr[j, pl.ds(r, tm // dil, stride=dil), :].astype(BF16)


def _attn_in_kernel(x_ref, g_ref, wmain_ref, wa_ref, qn_ref, kvn_ref, wq_ref, wqs_ref, wk_ref,
                    wvt_ref, e_ref, es_ref, cos_ref, sin_ref,
                    qa_ref, ka_ref, vt_ref, zb0_ref, zb1_ref, zb2_ref, zc_ref, zd_ref, scr1, scr2):
    hb = _rms(x_ref[...], g_ref[...]).astype(BF16)
    chunk = lambda c: _dot(hb, wmain_ref[:, c * MAIN_CHUNK:(c + 1) * MAIN_CHUNK])
    zb0_ref[...] = chunk(0).astype(BF16)
    _split_residues(chunk(1), scr1, zb1_ref, DIL_PAIRS[1][1])
    _split_residues(chunk(2), scr2, zb2_ref, DIL_PAIRS[2][1])
    zc_ref[...] = chunk(3).astype(BF16)
    zd_ref[...] = chunk(4).astype(BF16)

    za = _dot(hb, wa_ref[...])
    cq = _rms(za[:, :MLA_Q_RANK], qn_ref[...]).astype(BF16)
    ckv = _rms(za[:, MLA_Q_RANK:MLA_Q_RANK + MLA_KV_RANK], kvn_ref[...]).astype(BF16)
    kr = za[:, MLA_Q_RANK + MLA_KV_RANK:]
    kr_hi = kr.astype(BF16)
    kr_lo = (kr - kr_hi.astype(F32)).astype(BF16)
    cos = jnp.concatenate([cos_ref[...]] * MLA_HEADS, axis=1)
    sin = jnp.concatenate([sin_ref[...]] * MLA_HEADS, axis=1)
    q = _dot(cq, wq_ref[...]) * cos + _dot(cq, wqs_ref[...]) * sin
    qa_ref[...] = (q * (LOG2_E * (MLA_NOPE + MLA_ROPE) ** -0.5)).astype(BF16)
    k_rope = _dot(kr_hi, e_ref[...]) + _dot(kr_lo, e_ref[...])
    k_swap = _dot(kr_hi, es_ref[...]) + _dot(kr_lo, es_ref[...])
    ka_ref[...] = ((_dot(ckv, wk_ref[...]) + k_rope) * cos + k_swap * sin).astype(BF16)
    vt = _dot_nt(wvt_ref[...], ckv)
    row = lax.broadcasted_iota(jnp.int32, vt.shape, 0)
    vt_ref[...] = jnp.where(row % MLA_VROWS == MLA_V, 1.0, vt).astype(BF16)


def _attn_in(x, w, cos_t, sin_t, tm):
    bsz, seq, _ = x.shape
    tok = lambda width: pl.BlockSpec((None, tm, width), lambda b, i: (b, i, 0))
    table = pl.BlockSpec((tm, MLA_SLOT), lambda b, i: (i, 0))
    group = 3 * BRANCH_W
    slot_w = MLA_HEADS * MLA_SLOT
    out_specs = [tok(slot_w), tok(slot_w),
                 pl.BlockSpec((None, MLA_HEADS * MLA_VROWS, tm), lambda b, i: (b, 0, i)), tok(group)]
    out_shape = [jax.ShapeDtypeStruct((bsz, seq, slot_w), BF16), jax.ShapeDtypeStruct((bsz, seq, slot_w), BF16),
                 jax.ShapeDtypeStruct((bsz, MLA_HEADS * MLA_VROWS, seq), BF16),
                 jax.ShapeDtypeStruct((bsz, seq, group), BF16)]
    scratch = []
    for _, dil in DIL_PAIRS[1:]:
        out_specs.append(pl.BlockSpec((None, dil, tm // dil, group), lambda b, i: (b, 0, i, 0)))
        out_shape.append(jax.ShapeDtypeStruct((bsz, dil, seq // dil, group), BF16))
        scratch.append(pltpu.VMEM((group // LANES, tm, LANES), F32))
    out_specs += [tok(C_EXP), tok(D_COLS)]
    out_shape += [jax.ShapeDtypeStruct((bsz, seq, C_EXP), BF16), jax.ShapeDtypeStruct((bsz, seq, D_COLS), BF16)]
    return pl.pallas_call(
        _attn_in_kernel,
        grid=(bsz, seq // tm),
        in_specs=[tok(D_MODEL), _resident((1, D_MODEL)), _resident((D_MODEL, MAIN_COLS)),
                  _resident((D_MODEL, A_PAD)), _resident((1, MLA_Q_RANK)), _resident((1, MLA_KV_RANK)),
                  _resident((MLA_Q_RANK, slot_w)), _resident((MLA_Q_RANK, slot_w)),
                  _resident((MLA_KV_RANK, slot_w)), _resident((MLA_HEADS * MLA_VROWS, MLA_KV_RANK)),
                  _resident((MLA_SLOT, slot_w)), _resident((MLA_SLOT, slot_w)),
                  table, table],
        out_specs=out_specs, out_shape=out_shape, scratch_shapes=scratch,
        compiler_params=_params("parallel", "parallel"),
        name="attn_in",
    )(x, w["ln_attn"], w["w_main"], w["w_a"], w["q_norm"], w["kv_norm"], w["wq"], w["wq_s"],
      w["wk"], w["wv_t"], w["e"], w["e_s"], cos_t, sin_t)


def _mla_kernel(q_ref, k_ref, vt_ref, o_ref, m_scr, acc_scr):
    kv = pl.program_id(2)

    @pl.when(kv == 0)
    def _():
        m_scr[...] = jnp.full(m_scr.shape, NEG_INF, F32)
        acc_scr[...] = jnp.zeros(acc_scr.shape, F32)

    def scores(h):
        return _dot_nt(k_ref[:, h * MLA_SLOT:(h + 1) * MLA_SLOT], q_ref[:, h * MLA_SLOT:(h + 1) * MLA_SLOT])

    s_next = scores(0)
    for h in range(MLA_HEADS):
        s = s_next
        if h + 1 < MLA_HEADS:
            s_next = scores(h + 1)
        m_prev = m_scr[h]
        m_new = jnp.maximum(m_prev, jnp.max(s, axis=0, keepdims=True))
        p = jnp.exp2(s - m_new).astype(BF16)
        acc_scr[h] = jnp.exp2(m_prev - m_new) * acc_scr[h] + _dot(vt_ref[h * MLA_VROWS:(h + 1) * MLA_VROWS, :], p)
        m_scr[h] = m_new

    @pl.when(kv == pl.num_programs(2) - 1)
    def _():
        o_t = jnp.concatenate([acc_scr[h, :MLA_V, :] / acc_scr[h, MLA_V:MLA_V + 1, :] for h in range(MLA_HEADS)],
                              axis=0)
        o_ref[...] = o_t.T.astype(BF16)


def _mla(qa, ka, vt, tq, tk):
    bsz, seq, _ = qa.shape
    tq, tk = min(tq, seq), min(tk, seq)
    return pl.pallas_call(
        _mla_kernel,
        grid=(bsz, seq // tq, seq // tk),
        in_specs=[pl.BlockSpec((None, tq, MLA_HEADS * MLA_SLOT), lambda b, i, j: (b, i, 0)),
                  pl.BlockSpec((None, tk, MLA_HEADS * MLA_SLOT), lambda b, i, j: (b, j, 0)),
                  pl.BlockSpec((None, MLA_HEADS * MLA_VROWS, tk), lambda b, i, j: (b, 0, j))],
        out_specs=pl.BlockSpec((None, tq, MLA_HEADS * MLA_V), lambda b, i, j: (b, i, 0)),
        out_shape=jax.ShapeDtypeStruct((bsz, seq, MLA_HEADS * MLA_V), BF16),
        scratch_shapes=[pltpu.VMEM((MLA_HEADS, 1, tq), F32), pltpu.VMEM((MLA_HEADS, MLA_VROWS, tq), F32)],
        compiler_params=_params("parallel", "parallel", "arbitrary"),
        name="mla_attention",
    )(qa, ka, vt)


def _head_rows(x, nheads):
    head = lax.broadcasted_iota(jnp.int32, x.shape, 1) // HEAD_DIM
    xf = x.astype(F32)
    return jnp.concatenate([jnp.where(head == h, xf, 0.0) for h in range(nheads)], axis=0).astype(x.dtype)


def _head_cols(x, nheads):
    n = x.shape[0] // nheads
    head = lax.broadcasted_iota(jnp.int32, (n, x.shape[1]), 1) // HEAD_DIM
    out = x[:n]
    for h in range(1, nheads):
        out = jnp.where(head == h, x[h * n:(h + 1) * n], out)
    return out


BAND_UNIT = 128
ONES_ROWS = 16
LN_2 = math.log(2.0)


def _banded_kernel(*refs, halo, tm, seq_len, tile_axis, with_sink, with_lse):
    q_ref, kp_ref, km_ref, kn_ref, vp_ref, vm_ref, vn_ref, bias_ref = refs[:8]
    refs = refs[8:]
    sink_ref = None
    if with_sink:
        sink_ref, refs = refs[0], refs[1:]
    o_ref = refs[0]
    lse_ref = refs[1] if with_lse else None

    tile = pl.program_id(tile_axis)
    kcat = jnp.concatenate([kp_ref[...], km_ref[...], kn_ref[...]], axis=0)
    vcat = jnp.concatenate([vp_ref[...], vm_ref[...], vn_ref[...]], axis=0)
    vt = vcat.astype(F32).T
    ones = jnp.where(lax.broadcasted_iota(jnp.int32, (ONES_ROWS, vt.shape[1]), 0) == 0, 1.0, 0.0)
    vt = jnp.concatenate([vt, ones], axis=0).astype(BF16)
    nk = BAND_UNIT + 2 * halo
    nunit = tm // BAND_UNIT
    lanes = lambda x, h: x[:, h * BAND_UNIT:(h + 1) * BAND_UNIT]

    def scores(u):
        qbd = _head_rows(q_ref[u * BAND_UNIT:(u + 1) * BAND_UNIT, :], DIL_SLOTS)
        return _dot_nt(kcat[u * BAND_UNIT:u * BAND_UNIT + nk], qbd)

    raw = [scores(u) for u in range(nunit)]
    outs = []
    for u in range(nunit):
        s = raw[u] * (LOG2_E * HEAD_DIM ** -0.5) + bias_ref[...]
        if u == 0 or u == nunit - 1:
            kpos = tile * tm + u * BAND_UNIT - halo + lax.broadcasted_iota(jnp.int32, s.shape, 0)
            if u == 0:
                s = jnp.where(kpos >= 0, s, NEG_INF)
            if u == nunit - 1:
                s = jnp.where(kpos < seq_len, s, NEG_INF)
        m = jnp.max(s, axis=0, keepdims=True)
        if with_sink:
            m = jnp.maximum(m, sink_ref[...])
        p = jnp.exp2(s - m).astype(BF16)
        acc = _dot(vt[:, u * BAND_UNIT:u * BAND_UNIT + nk], p)
        l = acc[DIL_SLOTS * HEAD_DIM:DIL_SLOTS * HEAD_DIM + 1]
        if with_sink:
            l = l + jnp.exp2(sink_ref[...] - m)
        inv = 1.0 / l
        out_t = [acc[h * HEAD_DIM:(h + 1) * HEAD_DIM, h * BAND_UNIT:(h + 1) * BAND_UNIT] * lanes(inv, h)
                 for h in range(DIL_SLOTS)]
        if with_lse:
            lse = m * LN_2 + jnp.log(l)
            out_t += [jnp.broadcast_to(lanes(lse, h), (HEAD_DIM, BAND_UNIT)) for h in range(DIL_SLOTS)]
        outs.append(jnp.concatenate(out_t, axis=0))
    for u in range(nunit):
        rows = slice(u * BAND_UNIT, (u + 1) * BAND_UNIT)
        out = outs[u].T
        o_ref[rows, :] = out[:, :BRANCH_W].astype(BF16)
        if with_lse:
            lse_ref[rows, :] = out[:, BRANCH_W:]


def _banded(z, bias, sink, *, halo, tm, with_lse):
    bsz, dil, sub, _ = z.shape
    tm = min(tm, sub)
    assert tm % BAND_UNIT == 0 and BAND_UNIT % halo == 0
    blk = halo
    per_tile = tm // blk
    last_blk = sub // blk - 1
    nk = BAND_UNIT + 2 * halo

    def main(j):
        return pl.BlockSpec((None, None, tm, BRANCH_W), lambda b, r, i: (b, r, i, j))

    def prev(j):
        return pl.BlockSpec((None, None, blk, BRANCH_W),
                            lambda b, r, i: (b, r, jnp.maximum(i * per_tile - 1, 0), j))

    def nxt(j):
        return pl.BlockSpec((None, None, blk, BRANCH_W),
                            lambda b, r, i: (b, r, jnp.minimum((i + 1) * per_tile, last_blk), j))

    in_specs = [main(0), prev(1), main(1), nxt(1), prev(2), main(2), nxt(2),
                _resident((nk, DIL_SLOTS * BAND_UNIT))]
    args = [z, z, z, z, z, z, z, bias]
    if sink is not None:
        in_specs.append(_resident((1, DIL_SLOTS * BAND_UNIT)))
        args.append(sink)
    out_spec = pl.BlockSpec((None, None, tm, BRANCH_W), lambda b, r, i: (b, r, i, 0))
    out_specs = [out_spec]
    out_shape = [jax.ShapeDtypeStruct((bsz, dil, sub, BRANCH_W), BF16)]
    if with_lse:
        out_specs.append(out_spec)
        out_shape.append(jax.ShapeDtypeStruct((bsz, dil, sub, BRANCH_W), F32))
    return pl.pallas_call(
        functools.partial(_banded_kernel, halo=halo, tm=tm, seq_len=sub, tile_axis=2,
                          with_sink=sink is not None, with_lse=with_lse),
        grid=(bsz, dil, sub // tm),
        in_specs=in_specs, out_specs=out_specs, out_shape=out_shape,
        compiler_params=_params("parallel", "parallel", "parallel"),
        name=f"banded_d{dil}_h{halo}",
    )(*args)


def _na_kernel(q_ref, k_ref, v_ref, bias_ref, o_ref, *, rows, rows_per_step):
    step = pl.program_id(1)
    nkeys = NA_ROWS * GRID_W

    def window(rr):
        r = step * rows_per_step + rr
        sr = jnp.clip(r - NA_ROWS // 2, 0, rows - NA_ROWS)
        return sr - r + NA_ROWS - 1, pl.multiple_of(sr * GRID_W, GRID_W)

    def scores(rr):
        qbd = _head_rows(q_ref[rr * GRID_W:(rr + 1) * GRID_W, :], NA_HEADS)
        return _dot_nt(qbd, k_ref[pl.ds(wins[rr][1], nkeys), :])

    wins = [window(rr) for rr in range(rows_per_step)]
    raw = [scores(rr) for rr in range(rows_per_step)]
    for rr in range(rows_per_step):
        variant, start = wins[rr]
        s = raw[rr] * (HEAD_DIM ** -0.5) + bias_ref[variant]
        p = jnp.exp(s - jnp.max(s, axis=1, keepdims=True))
        obd = _dot(p.astype(BF16), v_ref[pl.ds(start, nkeys), :]) / jnp.sum(p, axis=1, keepdims=True)
        o_ref[rr * GRID_W:(rr + 1) * GRID_W, :] = _head_cols(obd, NA_HEADS).astype(BF16)


def _na(zd, bias, rows_per_step):
    bsz, seq, _ = zd.shape
    rows = seq // GRID_W
    assert rows >= NA_ROWS and rows % rows_per_step == 0
    tq = rows_per_step * GRID_W
    whole = lambda j: pl.BlockSpec((None, seq, BRANCH_W), lambda b, i: (b, 0, j))
    return pl.pallas_call(
        functools.partial(_na_kernel, rows=rows, rows_per_step=rows_per_step),
        grid=(bsz, rows // rows_per_step),
        in_specs=[pl.BlockSpec((None, tq, BRANCH_W), lambda b, i: (b, i, 0)), whole(1), whole(2),
                  _resident((NA_ROWS, NA_HEADS * GRID_W, NA_ROWS * GRID_W))],
        out_specs=pl.BlockSpec((None, tq, BRANCH_W), lambda b, i: (b, i, 0)),
        out_shape=jax.ShapeDtypeStruct((bsz, seq, BRANCH_W), BF16),
        compiler_params=_params("parallel", "arbitrary"),
        name="neighborhood_attention",
    )(zd, zd, zd, bias)


def _join_residues(ref, scr):
    dil, rows, width = ref.shape
    for j in range(width // LANES):
        for r in range(dil):
            scr[j, pl.ds(r, rows, stride=dil), :] = ref[r, :, j * LANES:(j + 1) * LANES].astype(F32)
    return jnp.concatenate([scr[j] for j in range(width // LANES)], axis=1)


def _merge_kernel(x_ref, g_ref, oa_ref, ob0_ref, ob1_ref, ob2_ref, l0_ref, l1_ref, l2_ref,
                  oc_ref, od_ref, wg_ref, wb_ref, wo_ref, out_ref, *scratch):
    x = x_ref[...]
    hb = _rms(x, g_ref[...]).astype(BF16)
    outs = (ob0_ref[0].astype(F32), _join_residues(ob1_ref, scratch[0]), _join_residues(ob2_ref, scratch[1]))
    lses = (l0_ref[0], _join_residues(l1_ref, scratch[2]), _join_residues(l2_ref, scratch[3]))
    top = jnp.maximum(jnp.maximum(lses[0], lses[1]), lses[2])
    ws = [jnp.exp(l - top) for l in lses]
    den = ws[0] + ws[1] + ws[2]
    ob = sum((wgt / den) * o for wgt, o in zip(ws, outs))
    branches = (oa_ref[...], ob.astype(BF16), oc_ref[...], od_ref[...])
    merged = None
    for j, o in enumerate(branches):
        term = jax.nn.sigmoid(_dot(hb, wg_ref[j])) * _dot(o, wb_ref[j])
        merged = term if merged is None else merged + term
    out_ref[...] = x + _dot(merged.astype(BF16), wo_ref[...])


def _merge(x, w, oa, obs, lses, oc, od, tm):
    bsz, seq, _ = x.shape
    tok = lambda width: pl.BlockSpec((None, tm, width), lambda b, i: (b, i, 0))
    residues = [pl.BlockSpec((None, dil, tm // dil, BRANCH_W), lambda b, i: (b, 0, i, 0)) for _, dil in DIL_PAIRS]
    return pl.pallas_call(
        _merge_kernel,
        grid=(bsz, seq // tm),
        in_specs=[tok(D_MODEL), _resident((1, D_MODEL)), tok(BRANCH_W)] + residues + residues
                 + [tok(BRANCH_W), tok(BRANCH_W),
                    _resident((N_BRANCH, D_MODEL, D_MODEL)), _resident((N_BRANCH, BRANCH_W, D_MODEL)),
                    _resident((D_MODEL, D_MODEL))],
        out_specs=tok(D_MODEL),
        out_shape=jax.ShapeDtypeStruct(x.shape, F32),
        scratch_shapes=[pltpu.VMEM((BRANCH_W // LANES, tm, LANES), F32)] * 4,
        compiler_params=_params("parallel", "parallel"),
        name="merge",
    )(x, w["ln_attn"], oa, *obs, *lses, oc, od, w["w_gate"], w["w_branch"], w["w_out"])


SUBLANES = 8
GELU_C0 = math.sqrt(2.0 / math.pi)
GELU_C1 = GELU_C0 * 0.044715


def _ffn_kernel(x_ref, xp_ref, xn_ref, g_ref, wu_ref, cw_ref, cb_ref, wd_ref, out_ref, *, tm):
    tile = pl.program_id(1)
    g = g_ref[...]
    x = x_ref[...]
    hn = _rms(x, g).astype(BF16)
    keep_prev = jnp.where(tile > 0, 1.0, 0.0)
    keep_next = jnp.where(tile < pl.num_programs(1) - 1, 1.0, 0.0)
    halo = jnp.concatenate([_rms(xp_ref[...], g) * keep_prev, _rms(xn_ref[...], g) * keep_next], axis=0)
    wu = wu_ref[...]
    u = _dot(hn, wu)
    uh = _dot(halo.astype(BF16), wu)
    row = lax.broadcasted_iota(jnp.int32, (SUBLANES, u.shape[1]), 0)
    below = pltpu.roll(u, 1, 0)
    below = jnp.concatenate([jnp.where(row == 0, uh[SUBLANES - 1:SUBLANES], below[:SUBLANES]), below[SUBLANES:]], axis=0)
    above = pltpu.roll(u, tm - 1, 0)
    above = jnp.concatenate([above[:tm - SUBLANES],
                             jnp.where(row == SUBLANES - 1, uh[SUBLANES:SUBLANES + 1], above[tm - SUBLANES:])], axis=0)
    cw = cw_ref[...]
    y = below * cw[0:1] + u * cw[1:2] + above * cw[2:3] + cb_ref[...]
    a, half_b = y[:, :D_FF], y[:, D_FF:]
    act = a * (1.0 + jnp.tanh(a * (GELU_C0 + GELU_C1 * (a * a)))) * half_b
    out_ref[...] = x + _dot(act.astype(BF16), wd_ref[...])


def _ffn(x, w, tm):
    bsz, seq, _ = x.shape
    per_tile = tm // SUBLANES
    last = seq // SUBLANES - 1
    tok = pl.BlockSpec((None, tm, D_MODEL), lambda b, i: (b, i, 0))
    return pl.pallas_call(
        functools.partial(_ffn_kernel, tm=tm),
        grid=(bsz, seq // tm),
        in_specs=[tok,
                  pl.BlockSpec((None, SUBLANES, D_MODEL), lambda b, i: (b, jnp.maximum(i * per_tile - 1, 0), 0)),
                  pl.BlockSpec((None, SUBLANES, D_MODEL), lambda b, i: (b, jnp.minimum((i + 1) * per_tile, last), 0)),
                  _resident((1, D_MODEL)), _resident((D_MODEL, 2 * D_FF)), _resident((CONV_W, 2 * D_FF)),
                  _resident((1, 2 * D_FF)), _resident((D_FF, D_MODEL))],
        out_specs=tok,
        out_shape=jax.ShapeDtypeStruct(x.shape, F32),
        compiler_params=_params("parallel", "parallel"),
        name="conv_ffn",
    )(x, x, x, w["ln_ffn"], w["w_ffn_up"], w["ffn_conv_w"], w["ffn_conv_b"], w["w_ffn_down"])


def _ple_kernel(x_ref, p_ref, g_ref, wg_ref, wp_ref, gf_ref, out_ref, *, final):
    x = x_ref[...]
    gate = jax.nn.sigmoid(_dot(_rms(x, g_ref[...]).astype(BF16), wg_ref[...]))
    y = x + gate * _dot(p_ref[...].astype(BF16), wp_ref[...])
    if final:
        y = _rms(y, gf_ref[...])
    out_ref[...] = y


def _ple(x, p, layer, w, ln_final, tm, final):
    bsz, seq, _ = x.shape
    tok = lambda width: pl.BlockSpec((None, tm, width), lambda b, i: (b, i, 0))
    return pl.pallas_call(
        functools.partial(_ple_kernel, final=final),
        grid=(bsz, seq // tm),
        in_specs=[tok(D_MODEL), pl.BlockSpec((None, None, tm, PLE_DIM), lambda b, i: (layer, b, i, 0)),
                  _resident((1, D_MODEL)), _resident((D_MODEL, D_MODEL)),
                  _resident((PLE_DIM, D_MODEL)), _resident((1, D_MODEL))],
        out_specs=tok(D_MODEL),
        out_shape=jax.ShapeDtypeStruct(x.shape, F32),
        compiler_params=_params("parallel", "parallel"),
        name="ple_final" if final else "ple",
    )(x, p, w["ln_ple"], w["w_ple_gate"], w["w_ple_proj"], ln_final)


def _gather_cols(wmat, cols, sign=None):
    cols = np.asarray(cols)
    picked = jnp.take(wmat, jnp.asarray(np.maximum(cols, 0)), axis=-1)
    scale = (cols >= 0).astype(np.float32) * (1.0 if sign is None else np.asarray(sign, np.float32))
    return picked * jnp.asarray(scale)


def _prep_weights(ln_attn, w_in, q_norm, kv_norm, w_q_up, w_kv_up, w_gate, w_branch, w_out, ln_ffn,
                  w_ffn_up, ffn_conv_w, ffn_conv_b, w_ffn_down, ln_ple, w_ple_gate, w_ple_proj):
    half = MLA_ROPE // 2
    qk_dim = MLA_NOPE + MLA_ROPE
    slot_w = MLA_HEADS * MLA_SLOT

    c0 = A_COLS + B_COLS
    nq = SWA_Q_HEADS * HEAD_DIM
    nk = SWA_KV_HEADS * HEAD_DIM

    def per_query_head(base):
        heads = np.repeat(np.arange(SWA_KV_HEADS), SWA_Q_HEADS // SWA_KV_HEADS)
        return [w_in[..., base + h * HEAD_DIM:base + (h + 1) * HEAD_DIM] for h in heads]

    w_main = jnp.concatenate([w_in[..., A_COLS:c0 + nq]] + per_query_head(c0 + nq) + per_query_head(c0 + nq + nk)
                             + [w_in[..., c0 + C_COLS:]], axis=-1)
    w_a = jnp.pad(w_in[..., :A_COLS], ((0, 0), (0, 0), (0, A_PAD - A_COLS)))

    q_cols = np.full((slot_w,), -1)
    qs_cols = np.full((slot_w,), -1)
    qs_sign = np.ones((slot_w,), np.float32)
    k_cols = np.full((slot_w,), -1)
    e_rows = np.full((slot_w,), -1)
    es_rows = np.full((slot_w,), -1)
    es_sign = np.ones((slot_w,), np.float32)
    for h in range(MLA_HEADS):
        base = h * MLA_SLOT
        q_cols[base:base + qk_dim] = h * qk_dim + np.arange(qk_dim)
        k_cols[base:base + MLA_NOPE] = h * (MLA_NOPE + MLA_V) + np.arange(MLA_NOPE)
        rope0 = base + MLA_NOPE
        qs_cols[rope0:rope0 + half] = h * qk_dim + MLA_NOPE + half + np.arange(half)
        qs_sign[rope0:rope0 + half] = -1.0
        qs_cols[rope0 + half:rope0 + MLA_ROPE] = h * qk_dim + MLA_NOPE + np.arange(half)
        e_rows[rope0:rope0 + MLA_ROPE] = np.arange(MLA_ROPE)
        es_rows[rope0:rope0 + half] = half + np.arange(half)
        es_sign[rope0:rope0 + half] = -1.0
        es_rows[rope0 + half:rope0 + MLA_ROPE] = np.arange(half)
    v_cols = np.full((MLA_HEADS, MLA_VROWS), -1)
    v_cols[:, :MLA_V] = np.arange(MLA_HEADS)[:, None] * (MLA_NOPE + MLA_V) + MLA_NOPE + np.arange(MLA_V)
    v_cols = v_cols.reshape(-1)
    eye = jnp.eye(MLA_SLOT, dtype=F32)

    row = lambda a: a[:, None, :]
    gate_half = jnp.asarray(np.concatenate([np.ones(D_FF, np.float32), np.full(D_FF, 0.5, np.float32)]))
    stacked = {
        "ln_attn": row(ln_attn), "q_norm": row(q_norm), "kv_norm": row(kv_norm),
        "w_main": w_main.astype(BF16),
        "w_a": w_a.astype(BF16),
        "wq": _gather_cols(w_q_up, q_cols).astype(BF16),
        "wq_s": _gather_cols(w_q_up, qs_cols, qs_sign).astype(BF16),
        "wk": _gather_cols(w_kv_up, k_cols).astype(BF16),
        "wv_t": jnp.swapaxes(_gather_cols(w_kv_up, v_cols), -1, -2).astype(BF16),
        "w_gate": w_gate.astype(BF16), "w_branch": w_branch.astype(BF16), "w_out": w_out.astype(BF16),
        "ln_ffn": row(ln_ffn), "w_ffn_up": w_ffn_up.astype(BF16), "ffn_conv_w": ffn_conv_w * gate_half,
        "ffn_conv_b": row(ffn_conv_b * gate_half), "w_ffn_down": w_ffn_down.astype(BF16),
        "ln_ple": row(ln_ple), "w_ple_gate": w_ple_gate.astype(BF16), "w_ple_proj": w_ple_proj.astype(BF16),
    }
    shared = {
        "e": _gather_cols(eye, e_rows).astype(BF16),
        "e_s": _gather_cols(eye, es_rows, es_sign).astype(BF16),
    }
    return [dict({k: v[i] for k, v in stacked.items()}, **shared) for i in range(DEPTH)]


def _rope_tables(seq):
    inv = ROPE_THETA ** (-jnp.arange(0, MLA_ROPE, 2, dtype=F32) / MLA_ROPE)
    ang = jnp.arange(seq, dtype=F32)[:, None] * inv[None]
    cos, sin = jnp.cos(ang), jnp.sin(ang)
    pad = MLA_SLOT - MLA_NOPE - MLA_ROPE
    cos_t = jnp.concatenate([jnp.ones((seq, MLA_NOPE), F32), cos, cos, jnp.zeros((seq, pad), F32)], axis=1)
    sin_t = jnp.concatenate([jnp.zeros((seq, MLA_NOPE), F32), sin, sin, jnp.zeros((seq, pad), F32)], axis=1)
    return cos_t, sin_t


def _t5_bucket(rel):
    nb = REL_BUCKETS // 2
    max_exact = nb // 2
    ret = jnp.where(rel > 0, nb, 0)
    n = jnp.abs(rel)
    nf = jnp.maximum(n, 1).astype(F32)
    large = max_exact + (jnp.log(nf / max_exact) / math.log(REL_MAX_DIST / max_exact) * (nb - max_exact)).astype(jnp.int32)
    large = jnp.minimum(large, nb - 1)
    return ret + jnp.where(n < max_exact, n, large)


def _band_bias(rel_table, window, dil, head0):
    nk = BAND_UNIT + 2 * window
    off = np.arange(-(window + BAND_UNIT - 1), window + BAND_UNIT)
    picked = jax.nn.one_hot(_t5_bucket(jnp.asarray(off * dil)), REL_BUCKETS, dtype=F32)
    per_off = jnp.dot(picked, rel_table[:, head0:head0 + DIL_SLOTS].astype(F32), precision=lax.Precision.HIGHEST)
    per_off = jnp.where(jnp.asarray(np.abs(off) <= window)[:, None], per_off * LOG2_E, NEG_INF)
    cols = [per_off[BAND_UNIT - 1 - q:BAND_UNIT - 1 - q + nk] for q in range(BAND_UNIT)]
    return jnp.transpose(jnp.stack(cols, axis=0), (1, 2, 0)).reshape(nk, DIL_SLOTS * BAND_UNIT)


def _na_bias(rpb):
    qc = np.arange(GRID_W)[:, None]
    kc = np.arange(GRID_W)[None, :]
    sc = np.clip(qc - NA_COLS // 2, 0, GRID_W - NA_COLS)
    valid = (kc >= sc) & (kc < sc + NA_COLS)
    edge = GRID_W - NA_COLS
    ext = jnp.concatenate([jnp.repeat(rpb[..., :1], edge, axis=-1), rpb, jnp.repeat(rpb[..., -1:], edge, axis=-1)],
                          axis=-1).astype(F32)
    table = jnp.stack([ext[..., GRID_W - 1 - q:2 * GRID_W - 1 - q] for q in range(GRID_W)], axis=2)
    table = jnp.where(jnp.asarray(valid)[None, None], table, NEG_INF)
    variants = [jnp.transpose(table[:, v:v + NA_ROWS], (0, 2, 1, 3)).reshape(NA_HEADS * GRID_W, NA_ROWS * GRID_W)
                for v in range(NA_ROWS)]
    return jnp.stack(variants)


def _tables(seq, attn_sink, na_rpb, rel_table):
    cos_t, sin_t = _rope_tables(seq)
    return {
        "cos": cos_t, "sin": sin_t,
        "dil_bias": [_band_bias(rel_table, window // (2 * dil), dil, gi * DIL_SLOTS)
                     for gi, (window, dil) in enumerate(DIL_PAIRS)],
        "swa_bias": _band_bias(rel_table, SWA_WINDOW, 1, DIL_HEADS),
        "sink": [jnp.repeat(attn_sink[i].astype(F32) * LOG2_E, BAND_UNIT)[None, :] for i in range(DEPTH)],
        "na_bias": [_na_bias(na_rpb[i]) for i in range(DEPTH)],
    }


TILE_PROJ = 1024
TILE_WIDE = 512
TILE_MLA_Q, TILE_MLA_K = 1024, 2048
TILE_BAND = 1024
NA_ROWS_PER_STEP = 8


def _trunk(x, p, weights, tables, ln_final):
    bsz, seq, _ = x.shape
    tile_proj = min(TILE_PROJ, seq)
    for i in range(DEPTH):
        w = weights[i]
        qa, ka, vt, zb0, zb1, zb2, zc, zd = _attn_in(x, w, tables["cos"], tables["sin"], tile_proj)
        oa = _mla(qa, ka, vt, TILE_MLA_Q, TILE_MLA_K)
        obs, lses = [], []
        for gi, zb in enumerate((zb0[:, None], zb1, zb2)):
            halo = DIL_PAIRS[gi][0] // (2 * DIL_PAIRS[gi][1])
            o, lse = _banded(zb, tables["dil_bias"][gi], None, halo=halo, tm=TILE_BAND, with_lse=True)
            obs.append(o)
            lses.append(lse)
        (oc,) = _banded(zc[:, None], tables["swa_bias"], tables["sink"][i], halo=SWA_WINDOW, tm=TILE_BAND,
                        with_lse=False)
        oc = oc.reshape(bsz, seq, BRANCH_W)
        od = _na(zd, tables["na_bias"][i], rows_per_step=NA_ROWS_PER_STEP)
        x = _merge(x, w, oa, obs, lses, oc, od, TILE_WIDE)
        x = _ffn(x, w, TILE_WIDE)
        x = _ple(x, p, i, w, ln_final, tile_proj, final=(i == DEPTH - 1))
    return x


def kernel(x_prompt, x_sample, p_prompt, p_sample, ln_attn, w_in, q_norm, kv_norm, w_q_up, w_kv_up,
           attn_sink, na_rpb, rel_table, w_gate, w_branch, w_out, ln_ffn, w_ffn_up, ffn_conv_w,
           ffn_conv_b, w_ffn_down, ln_ple, w_ple_gate, w_ple_proj, ln_final):
    weights = _prep_weights(ln_attn, w_in, q_norm, kv_norm, w_q_up, w_kv_up, w_gate, w_branch, w_out,
                            ln_ffn, w_ffn_up, ffn_conv_w, ffn_conv_b, w_ffn_down, ln_ple, w_ple_gate,
                            w_ple_proj)
    assert x_prompt.shape[1] == x_sample.shape[1]
    tables = _tables(x_prompt.shape[1], attn_sink, na_rpb, rel_table)
    ln_final = ln_final[None, :]
    return (_trunk(x_prompt, p_prompt, weights, tables, ln_final),
            _trunk(x_sample, p_sample, weights, tables, ln_final))
```

```python
import functools
import math

import jax
import jax.numpy as jnp
import numpy as np
from jax import lax
from jax.experimental import pallas as pl
from jax.experimental.pallas import tpu as pltpu

D_MODEL = 1024
DEPTH = 4
PLE_DIM = 256
GRID_W = 64
HEAD_DIM = 64
BRANCH_W = 256
N_BRANCH = 4
EPS = 1e-6
NEG_INF = -1e30

MLA_HEADS = 4
MLA_Q_RANK = 256
MLA_KV_RANK = 128
MLA_NOPE = 64
MLA_ROPE = 32
MLA_V = 64
ROPE_THETA = 10000.0
MLA_SLOT = 128
MLA_VROWS = 96
LANES = 128
LOG2_E = math.log2(math.e)

DIL_PAIRS = ((128, 1), (512, 4), (2048, 16))
DIL_SLOTS = 4
DIL_HEADS = DIL_SLOTS * len(DIL_PAIRS)
DIL_BLOCK = 64

SWA_Q_HEADS = 4
SWA_KV_HEADS = 2
SWA_WINDOW = 128
SWA_BLOCK = 128

NA_HEADS = 4
NA_ROWS = 8
NA_COLS = 16

REL_BUCKETS = 32
REL_MAX_DIST = 1024

D_FF = 2816
CONV_W = 3

A_COLS = MLA_Q_RANK + MLA_KV_RANK + MLA_ROPE
B_COLS = 3 * DIL_HEADS * HEAD_DIM
C_COLS = (SWA_Q_HEADS + 2 * SWA_KV_HEADS) * HEAD_DIM
D_COLS = 3 * NA_HEADS * HEAD_DIM
A_PAD = 512
MAIN_COLS = B_COLS + C_COLS + D_COLS
GROUP_COLS = 3 * BRANCH_W
MAIN_SPLITS = (0, GROUP_COLS, 2 * GROUP_COLS, B_COLS, B_COLS + C_COLS, MAIN_COLS)

BF16 = jnp.bfloat16
F32 = jnp.float32

VMEM_LIMIT_BYTES = 56 * 1024 * 1024


def _params(*semantics):
    return pltpu.CompilerParams(dimension_semantics=semantics, vmem_limit_bytes=VMEM_LIMIT_BYTES)


def _resident(shape):
    return pl.BlockSpec(shape, lambda *_: (0,) * len(shape), pipeline_mode=pl.Buffered(1))


def _dot(a, b):
    return jnp.dot(a, b, preferred_element_type=F32)


def _dot_nt(a, b):
    return lax.dot_general(a, b, (((1,), (1,)), ((), ())), preferred_element_type=F32)


def _rms(x, g):
    return x * lax.rsqrt(jnp.mean(x * x, axis=-1, keepdims=True) + EPS) * g


def _split_residues(val, scr, out_ref, dil):
    tm = val.shape[0]
    nslab = val.shape[1] // LANES
    for j in range(nslab):
        scr[j] = val[:, j * LANES:(j + 1) * LANES]
    for r in range(dil):
        for j in range(nslab):
            out_ref[r, :, j * LANES:(j + 1) * LANES] = scr[j, pl.ds(r, tm // dil, stride=dil), :].astype(BF16)


def _attn_in_kernel(x_ref, g_ref, wmain_ref, wa_ref, qn_ref, kvn_ref, wq_ref, wqs_ref, wk_ref,
                    wvt_ref, es_ref, cos_ref, sin_ref,
                    qa_ref, ka_ref, vt_ref, zb0_ref, zb1_ref, zb2_ref, zc_ref, zd_ref, scr1, scr2):
    hb = _rms(x_ref[...], g_ref[...]).astype(BF16)
    chunk = lambda c: _dot(hb, wmain_ref[:, MAIN_SPLITS[c]:MAIN_SPLITS[c + 1]])
    zb0_ref[...] = chunk(0).astype(BF16)
    _split_residues(chunk(1), scr1, zb1_ref, DIL_PAIRS[1][1])
    _split_residues(chunk(2), scr2, zb2_ref, DIL_PAIRS[2][1])
    zc_ref[...] = chunk(3).astype(BF16)
    zd_ref[...] = chunk(4).astype(BF16)

    za = _dot(hb, wa_ref[...])
    cq = _rms(za[:, :MLA_Q_RANK], qn_ref[...]).astype(BF16)
    ckv = _rms(za[:, MLA_Q_RANK:MLA_Q_RANK + MLA_KV_RANK], kvn_ref[...]).astype(BF16)
    kr = za[:, MLA_Q_RANK + MLA_KV_RANK:]
    kr_hi = kr.astype(BF16).astype(F32)
    kr2 = (kr_hi + pltpu.roll(kr - kr_hi, MLA_ROPE, 1)).astype(BF16)
    cos = jnp.concatenate([cos_ref[...]] * MLA_HEADS, axis=1)
    sin = jnp.concatenate([sin_ref[...]] * MLA_HEADS, axis=1)
    q = _dot(cq, wq_ref[...]) * cos + _dot(cq, wqs_ref[...]) * sin
    qa_ref[...] = (q * (LOG2_E * (MLA_NOPE + MLA_ROPE) ** -0.5)).astype(BF16)
    k_plain = _dot(jnp.concatenate([ckv, kr2], axis=1), wk_ref[...])
    ka_ref[...] = (k_plain * cos + _dot(kr2, es_ref[...]) * sin).astype(BF16)
    vt = _dot_nt(wvt_ref[...], ckv)
    row = lax.broadcasted_iota(jnp.int32, vt.shape, 0)
    vt_ref[...] = jnp.where(row % MLA_VROWS == MLA_V, 1.0, vt).astype(BF16)


def _attn_in(x, w, cos_t, sin_t, tm):
    bsz, seq, _ = x.shape
    tok = lambda width: pl.BlockSpec((None, tm, width), lambda b, i: (b, i, 0))
    table = pl.BlockSpec((tm, MLA_SLOT), lambda b, i: (i, 0))
    group = 3 * BRANCH_W
    slot_w = MLA_HEADS * MLA_SLOT
    out_specs = [tok(slot_w), tok(slot_w),
                 pl.BlockSpec((None, MLA_HEADS * MLA_VROWS, tm), lambda b, i: (b, 0, i)), tok(group)]
    out_shape = [jax.ShapeDtypeStruct((bsz, seq, slot_w), BF16), jax.ShapeDtypeStruct((bsz, seq, slot_w), BF16),
                 jax.ShapeDtypeStruct((bsz, MLA_HEADS * MLA_VROWS, seq), BF16),
                 jax.ShapeDtypeStruct((bsz, seq, group), BF16)]
    scratch = []
    for _, dil in DIL_PAIRS[1:]:
        out_specs.append(pl.BlockSpec((None, dil, tm // dil, group), lambda b, i: (b, 0, i, 0)))
        out_shape.append(jax.ShapeDtypeStruct((bsz, dil, seq // dil, group), BF16))
        scratch.append(pltpu.VMEM((group // LANES, tm, LANES), F32))
    out_specs += [tok(C_COLS), tok(D_COLS)]
    out_shape += [jax.ShapeDtypeStruct((bsz, seq, C_COLS), BF16), jax.ShapeDtypeStruct((bsz, seq, D_COLS), BF16)]
    return pl.pallas_call(
        _attn_in_kernel,
        grid=(bsz, seq // tm),
        in_specs=[tok(D_MODEL), _resident((1, D_MODEL)), _resident((D_MODEL, MAIN_COLS)),
                  _resident((D_MODEL, A_PAD)), _resident((1, MLA_Q_RANK)), _resident((1, MLA_KV_RANK)),
                  _resident((MLA_Q_RANK, slot_w)), _resident((MLA_Q_RANK, slot_w)),
                  _resident((MLA_KV_RANK + MLA_SLOT, slot_w)), _resident((MLA_HEADS * MLA_VROWS, MLA_KV_RANK)),
                  _resident((MLA_SLOT, slot_w)), table, table],
        out_specs=out_specs, out_shape=out_shape, scratch_shapes=scratch,
        compiler_params=_params("parallel", "parallel"),
        name="attn_in",
    )(x, w["ln_attn"], w["w_main"], w["w_a"], w["q_norm"], w["kv_norm"], w["wq"], w["wq_s"],
      w["wk"], w["wv_t"], w["e_s"], cos_t, sin_t)


def _mla_kernel(q_ref, k_ref, vt_ref, o_ref, m_scr, acc_scr):
    kv = pl.program_id(2)

    @pl.when(kv == 0)
    def _():
        m_scr[...] = jnp.full(m_scr.shape, NEG_INF, F32)
        acc_scr[...] = jnp.zeros(acc_scr.shape, F32)

    def scores(h):
        return _dot_nt(k_ref[:, h * MLA_SLOT:(h + 1) * MLA_SLOT], q_ref[:, h * MLA_SLOT:(h + 1) * MLA_SLOT])

    s_next = scores(0)
    for h in range(MLA_HEADS):
        s = s_next
        if h + 1 < MLA_HEADS:
            s_next = scores(h + 1)
        m_prev = m_scr[h]
        m_new = jnp.maximum(m_prev, jnp.max(s, axis=0, keepdims=True))
        p = jnp.exp2(s - m_new).astype(BF16)
        acc_scr[h] = jnp.exp2(m_prev - m_new) * acc_scr[h] + _dot(vt_ref[h * MLA_VROWS:(h + 1) * MLA_VROWS, :], p)
        m_scr[h] = m_new

    @pl.when(kv == pl.num_programs(2) - 1)
    def _():
        o_t = jnp.concatenate([acc_scr[h, :MLA_V, :] / acc_scr[h, MLA_V:MLA_V + 1, :] for h in range(MLA_HEADS)],
                              axis=0)
        o_ref[...] = o_t.T.astype(BF16)


def _mla(qa, ka, vt, tq, tk):
    bsz, seq, _ = qa.shape
    tq, tk = min(tq, seq), min(tk, seq)
    return pl.pallas_call(
        _mla_kernel,
        grid=(bsz, seq // tq, seq // tk),
        in_specs=[pl.BlockSpec((None, tq, MLA_HEADS * MLA_SLOT), lambda b, i, j: (b, i, 0)),
                  pl.BlockSpec((None, tk, MLA_HEADS * MLA_SLOT), lambda b, i, j: (b, j, 0)),
                  pl.BlockSpec((None, MLA_HEADS * MLA_VROWS, tk), lambda b, i, j: (b, 0, j))],
        out_specs=pl.BlockSpec((None, tq, MLA_HEADS * MLA_V), lambda b, i, j: (b, i, 0)),
        out_shape=jax.ShapeDtypeStruct((bsz, seq, MLA_HEADS * MLA_V), BF16),
        scratch_shapes=[pltpu.VMEM((MLA_HEADS, 1, tq), F32), pltpu.VMEM((MLA_HEADS, MLA_VROWS, tq), F32)],
        compiler_params=_params("parallel", "parallel", "arbitrary"),
        name="mla_attention",
    )(qa, ka, vt)


def _head_rows(x, nheads):
    head = lax.broadcasted_iota(jnp.int32, x.shape, 1) // HEAD_DIM
    xf = x.astype(F32)
    return jnp.concatenate([jnp.where(head == h, xf, 0.0) for h in range(nheads)], axis=0).astype(x.dtype)


def _head_cols(x, nheads):
    n = x.shape[0] // nheads
    head = lax.broadcasted_iota(jnp.int32, (n, x.shape[1]), 1) // HEAD_DIM
    out = x[:n]
    for h in range(1, nheads):
        out = jnp.where(head == h, x[h * n:(h + 1) * n], out)
    return out


BAND_UNIT = 128
ONES_ROWS = 16
LN_2 = math.log(2.0)


def _query_rows(q, kv_heads):
    if kv_heads == DIL_SLOTS:
        return _head_rows(q, DIL_SLOTS)
    kvw = kv_heads * HEAD_DIM
    slot = lax.broadcasted_iota(jnp.int32, (q.shape[0], kvw), 1) // HEAD_DIM
    tiles = [q[:, t * kvw:(t + 1) * kvw].astype(F32) for t in range(DIL_SLOTS // kv_heads)]
    rows = [jnp.where(slot == h // (DIL_SLOTS // kv_heads), tiles[h % (DIL_SLOTS // kv_heads)], 0.0)
            for h in range(DIL_SLOTS)]
    return jnp.concatenate(rows, axis=0).astype(q.dtype)


def _banded_kernel(*refs, halo, tm, seq_len, tile_axis, kv_heads, with_sink, with_lse):
    q_ref, kp_ref, km_ref, kn_ref, vp_ref, vm_ref, vn_ref, bias_ref = refs[:8]
    refs = refs[8:]
    sink_ref = None
    if with_sink:
        sink_ref, refs = refs[0], refs[1:]
    o_ref = refs[0]
    lse_ref = refs[1] if with_lse else None

    tile = pl.program_id(tile_axis)
    kvw = kv_heads * HEAD_DIM
    kcat = jnp.concatenate([kp_ref[...], km_ref[...], kn_ref[...]], axis=0)
    vcat = jnp.concatenate([vp_ref[...], vm_ref[...], vn_ref[...]], axis=0)
    vt = vcat.astype(F32).T
    ones = jnp.where(lax.broadcasted_iota(jnp.int32, (ONES_ROWS, vt.shape[1]), 0) == 0, 1.0, 0.0)
    vt = jnp.concatenate([vt, ones], axis=0).astype(BF16)
    nk = BAND_UNIT + 2 * halo
    nunit = tm // BAND_UNIT
    lanes = lambda x, h: x[:, h * BAND_UNIT:(h + 1) * BAND_UNIT]

    def scores(u):
        qbd = _query_rows(q_ref[u * BAND_UNIT:(u + 1) * BAND_UNIT, :], kv_heads)
        return _dot_nt(kcat[u * BAND_UNIT:u * BAND_UNIT + nk], qbd)

    raw = [scores(u) for u in range(nunit)]
    outs = []
    for u in range(nunit):
        s = raw[u] * (LOG2_E * HEAD_DIM ** -0.5) + bias_ref[...]
        if u == 0 or u == nunit - 1:
            kpos = tile * tm + u * BAND_UNIT - halo + lax.broadcasted_iota(jnp.int32, s.shape, 0)
            if u == 0:
                s = jnp.where(kpos >= 0, s, NEG_INF)
            if u == nunit - 1:
                s = jnp.where(kpos < seq_len, s, NEG_INF)
        m = jnp.max(s, axis=0, keepdims=True)
        if with_sink:
            m = jnp.maximum(m, sink_ref[...])
        p = jnp.exp2(s - m).astype(BF16)
        acc = _dot(vt[:, u * BAND_UNIT:u * BAND_UNIT + nk], p)
        l = acc[kvw:kvw + 1]
        if with_sink:
            l = l + jnp.exp2(sink_ref[...] - m)
        inv = 1.0 / l
        kv_row = lambda h: (h * kv_heads // DIL_SLOTS) * HEAD_DIM
        out_t = [lanes(acc[kv_row(h):kv_row(h) + HEAD_DIM], h) * lanes(inv, h) for h in range(DIL_SLOTS)]
        if with_lse:
            lse = m * LN_2 + jnp.log(l)
            out_t += [jnp.broadcast_to(lanes(lse, h), (HEAD_DIM, BAND_UNIT)) for h in range(DIL_SLOTS)]
        outs.append(jnp.concatenate(out_t, axis=0))
    for u in range(nunit):
        rows = slice(u * BAND_UNIT, (u + 1) * BAND_UNIT)
        out = outs[u].T
        o_ref[rows, :] = out[:, :BRANCH_W].astype(BF16)
        if with_lse:
            lse_ref[rows, :] = out[:, BRANCH_W:]


def _banded(z, bias, sink, *, halo, tm, kv_heads, with_lse):
    bsz, dil, sub, _ = z.shape
    tm = min(tm, sub)
    assert tm % BAND_UNIT == 0 and BAND_UNIT % halo == 0
    blk = halo
    per_tile = tm // blk
    last_blk = sub // blk - 1
    nk = BAND_UNIT + 2 * halo
    kvw = kv_heads * HEAD_DIM
    first_kv = BRANCH_W // kvw

    def main(j):
        return pl.BlockSpec((None, None, tm, kvw), lambda b, r, i: (b, r, i, first_kv + j))

    def prev(j):
        return pl.BlockSpec((None, None, blk, kvw),
                            lambda b, r, i: (b, r, jnp.maximum(i * per_tile - 1, 0), first_kv + j))

    def nxt(j):
        return pl.BlockSpec((None, None, blk, kvw),
                            lambda b, r, i: (b, r, jnp.minimum((i + 1) * per_tile, last_blk), first_kv + j))

    in_specs = [pl.BlockSpec((None, None, tm, BRANCH_W), lambda b, r, i: (b, r, i, 0)),
                prev(0), main(0), nxt(0), prev(1), main(1), nxt(1),
                _resident((nk, DIL_SLOTS * BAND_UNIT))]
    args = [z, z, z, z, z, z, z, bias]
    if sink is not None:
        in_specs.append(_resident((1, DIL_SLOTS * BAND_UNIT)))
        args.append(sink)
    out_spec = pl.BlockSpec((None, None, tm, BRANCH_W), lambda b, r, i: (b, r, i, 0))
    out_specs = [out_spec]
    out_shape = [jax.ShapeDtypeStruct((bsz, dil, sub, BRANCH_W), BF16)]
    if with_lse:
        out_specs.append(out_spec)
        out_shape.append(jax.ShapeDtypeStruct((bsz, dil, sub, BRANCH_W), F32))
    return pl.pallas_call(
        functools.partial(_banded_kernel, halo=halo, tm=tm, seq_len=sub, tile_axis=2, kv_heads=kv_heads,
                          with_sink=sink is not None, with_lse=with_lse),
        grid=(bsz, dil, sub // tm),
        in_specs=in_specs, out_specs=out_specs, out_shape=out_shape,
        compiler_params=_params("parallel", "parallel", "parallel"),
        name=f"banded_d{dil}_h{halo}_kv{kv_heads}",
    )(*args)


def _na_kernel(q_ref, k_ref, v_ref, bias_ref, o_ref, *, rows, rows_per_step):
    step = pl.program_id(1)
    nkeys = NA_ROWS * GRID_W

    def window(rr):
        r = step * rows_per_step + rr
        sr = jnp.clip(r - NA_ROWS // 2, 0, rows - NA_ROWS)
        return sr - r + NA_ROWS - 1, pl.multiple_of(sr * GRID_W, GRID_W)

    def scores(rr):
        qbd = _head_rows(q_ref[rr * GRID_W:(rr + 1) * GRID_W, :], NA_HEADS)
        return _dot_nt(qbd, k_ref[pl.ds(wins[rr][1], nkeys), :])

    wins = [window(rr) for rr in range(rows_per_step)]
    raw = [scores(rr) for rr in range(rows_per_step)]
    for rr in range(rows_per_step):
        variant, start = wins[rr]
        s = raw[rr] * (HEAD_DIM ** -0.5) + bias_ref[variant]
        p = jnp.exp(s - jnp.max(s, axis=1, keepdims=True))
        obd = _dot(p.astype(BF16), v_ref[pl.ds(start, nkeys), :]) / jnp.sum(p, axis=1, keepdims=True)
        o_ref[rr * GRID_W:(rr + 1) * GRID_W, :] = _head_cols(obd, NA_HEADS).astype(BF16)


def _na(zd, bias, rows_per_step):
    bsz, seq, _ = zd.shape
    rows = seq // GRID_W
    assert rows >= NA_ROWS and rows % rows_per_step == 0
    tq = rows_per_step * GRID_W
    whole = lambda j: pl.BlockSpec((None, seq, BRANCH_W), lambda b, i: (b, 0, j))
    return pl.pallas_call(
        functools.partial(_na_kernel, rows=rows, rows_per_step=rows_per_step),
        grid=(bsz, rows // rows_per_step),
        in_specs=[pl.BlockSpec((None, tq, BRANCH_W), lambda b, i: (b, i, 0)), whole(1), whole(2),
                  _resident((NA_ROWS, NA_HEADS * GRID_W, NA_ROWS * GRID_W))],
        out_specs=pl.BlockSpec((None, tq, BRANCH_W), lambda b, i: (b, i, 0)),
        out_shape=jax.ShapeDtypeStruct((bsz, seq, BRANCH_W), BF16),
        compiler_params=_params("parallel", "arbitrary"),
        name="neighborhood_attention",
    )(zd, zd, zd, bias)


def _join_residues(ref, scr):
    dil, rows, width = ref.shape
    for j in range(width // LANES):
        for r in range(dil):
            scr[j, pl.ds(r, rows, stride=dil), :] = ref[r, :, j * LANES:(j + 1) * LANES].astype(F32)
    return jnp.concatenate([scr[j] for j in range(width // LANES)], axis=1)


def _merge_kernel(x_ref, g_ref, oa_ref, ob0_ref, ob1_ref, ob2_ref, l0_ref, l1_ref, l2_ref,
                  oc_ref, od_ref, wg_ref, wb_ref, wo_ref, out_ref, *scratch):
    x = x_ref[...]
    hb = _rms(x, g_ref[...]).astype(BF16)
    outs = (ob0_ref[0].astype(F32), _join_residues(ob1_ref, scratch[0]), _join_residues(ob2_ref, scratch[1]))
    lses = (l0_ref[0], _join_residues(l1_ref, scratch[2]), _join_residues(l2_ref, scratch[3]))
    top = jnp.maximum(jnp.maximum(lses[0], lses[1]), lses[2])
    ws = [jnp.exp(l - top) for l in lses]
    den = ws[0] + ws[1] + ws[2]
    ob = sum((wgt / den) * o for wgt, o in zip(ws, outs))
    branches = (oa_ref[...], ob.astype(BF16), oc_ref[...], od_ref[...])
    merged = None
    for j, o in enumerate(branches):
        term = jax.nn.sigmoid(_dot(hb, wg_ref[j])) * _dot(o, wb_ref[j])
        merged = term if merged is None else merged + term
    out_ref[...] = x + _dot(merged.astype(BF16), wo_ref[...])


def _merge(x, w, oa, obs, lses, oc, od, tm):
    bsz, seq, _ = x.shape
    tok = lambda width: pl.BlockSpec((None, tm, width), lambda b, i: (b, i, 0))
    residues = [pl.BlockSpec((None, dil, tm // dil, BRANCH_W), lambda b, i: (b, 0, i, 0)) for _, dil in DIL_PAIRS]
    return pl.pallas_call(
        _merge_kernel,
        grid=(bsz, seq // tm),
        in_specs=[tok(D_MODEL), _resident((1, D_MODEL)), tok(BRANCH_W)] + residues + residues
                 + [tok(BRANCH_W), tok(BRANCH_W),
                    _resident((N_BRANCH, D_MODEL, D_MODEL)), _resident((N_BRANCH, BRANCH_W, D_MODEL)),
                    _resident((D_MODEL, D_MODEL))],
        out_specs=tok(D_MODEL),
        out_shape=jax.ShapeDtypeStruct(x.shape, F32),
        scratch_shapes=[pltpu.VMEM((BRANCH_W // LANES, tm, LANES), F32)] * 4,
        compiler_params=_params("parallel", "parallel"),
        name="merge",
    )(x, w["ln_attn"], oa, *obs, *lses, oc, od, w["w_gate"], w["w_branch"], w["w_out"])


SUBLANES = 8
GELU_C0 = math.sqrt(2.0 / math.pi)
GELU_C1 = GELU_C0 * 0.044715


def _ffn_kernel(x_ref, xp_ref, xn_ref, g_ref, wu_ref, cw_ref, cb_ref, wd_ref, out_ref, *, tm):
    tile = pl.program_id(1)
    g = g_ref[...]
    x = x_ref[...]
    hn = _rms(x, g).astype(BF16)
    keep_prev = jnp.where(tile > 0, 1.0, 0.0)
    keep_next = jnp.where(tile < pl.num_programs(1) - 1, 1.0, 0.0)
    halo = jnp.concatenate([_rms(xp_ref[...], g) * keep_prev, _rms(xn_ref[...], g) * keep_next], axis=0)
    wu = wu_ref[...]
    u = _dot(hn, wu)
    uh = _dot(halo.astype(BF16), wu)
    row = lax.broadcasted_iota(jnp.int32, (SUBLANES, u.shape[1]), 0)
    below = pltpu.roll(u, 1, 0)
    below = jnp.concatenate([jnp.where(row == 0, uh[SUBLANES - 1:SUBLANES], below[:SUBLANES]), below[SUBLANES:]], axis=0)
    above = pltpu.roll(u, tm - 1, 0)
    above = jnp.concatenate([above[:tm - SUBLANES],
                             jnp.where(row == SUBLANES - 1, uh[SUBLANES:SUBLANES + 1], above[tm - SUBLANES:])], axis=0)
    cw = cw_ref[...]
    y = below * cw[0:1] + u * cw[1:2] + above * cw[2:3] + cb_ref[...]
    a, half_b = y[:, :D_FF], y[:, D_FF:]
    act = a * (1.0 + jnp.tanh(a * (GELU_C0 + GELU_C1 * (a * a)))) * half_b
    out_ref[...] = x + _dot(act.astype(BF16), wd_ref[...])


def _ffn(x, w, tm):
    bsz, seq, _ = x.shape
    per_tile = tm // SUBLANES
    last = seq // SUBLANES - 1
    tok = pl.BlockSpec((None, tm, D_MODEL), lambda b, i: (b, i, 0))
    return pl.pallas_call(
        functools.partial(_ffn_kernel, tm=tm),
        grid=(bsz, seq // tm),
        in_specs=[tok,
                  pl.BlockSpec((None, SUBLANES, D_MODEL), lambda b, i: (b, jnp.maximum(i * per_tile - 1, 0), 0)),
                  pl.BlockSpec((None, SUBLANES, D_MODEL), lambda b, i: (b, jnp.minimum((i + 1) * per_tile, last), 0)),
                  _resident((1, D_MODEL)), _resident((D_MODEL, 2 * D_FF)), _resident((CONV_W, 2 * D_FF)),
                  _resident((1, 2 * D_FF)), _resident((D_FF, D_MODEL))],
        out_specs=tok,
        out_shape=jax.ShapeDtypeStruct(x.shape, F32),
        compiler_params=_params("parallel", "parallel"),
        name="conv_ffn",
    )(x, x, x, w["ln_ffn"], w["w_ffn_up"], w["ffn_conv_w"], w["ffn_conv_b"], w["w_ffn_down"])


def _ple_kernel(x_ref, p_ref, g_ref, wg_ref, wp_ref, gf_ref, out_ref, *, final):
    x = x_ref[...]
    gate = jax.nn.sigmoid(_dot(_rms(x, g_ref[...]).astype(BF16), wg_ref[...]))
    y = x + gate * _dot(p_ref[...].astype(BF16), wp_ref[...])
    if final:
        y = _rms(y, gf_ref[...])
    out_ref[...] = y


def _ple(x, p, layer, w, ln_final, tm, final):
    bsz, seq, _ = x.shape
    tok = lambda width: pl.BlockSpec((None, tm, width), lambda b, i: (b, i, 0))
    return pl.pallas_call(
        functools.partial(_ple_kernel, final=final),
        grid=(bsz, seq // tm),
        in_specs=[tok(D_MODEL), pl.BlockSpec((None, None, tm, PLE_DIM), lambda b, i: (layer, b, i, 0)),
                  _resident((1, D_MODEL)), _resident((D_MODEL, D_MODEL)),
                  _resident((PLE_DIM, D_MODEL)), _resident((1, D_MODEL))],
        out_specs=tok(D_MODEL),
        out_shape=jax.ShapeDtypeStruct(x.shape, F32),
        compiler_params=_params("parallel", "parallel"),
        name="ple_final" if final else "ple",
    )(x, p, w["ln_ple"], w["w_ple_gate"], w["w_ple_proj"], ln_final)


def _gather_cols(wmat, cols, sign=None):
    cols = np.asarray(cols)
    picked = jnp.take(wmat, jnp.asarray(np.maximum(cols, 0)), axis=-1)
    scale = (cols >= 0).astype(np.float32) * (1.0 if sign is None else np.asarray(sign, np.float32))
    return picked * jnp.asarray(scale)


def _prep_weights(ln_attn, w_in, q_norm, kv_norm, w_q_up, w_kv_up, w_gate, w_branch, w_out, ln_ffn,
                  w_ffn_up, ffn_conv_w, ffn_conv_b, w_ffn_down, ln_ple, w_ple_gate, w_ple_proj):
    half = MLA_ROPE // 2
    qk_dim = MLA_NOPE + MLA_ROPE
    slot_w = MLA_HEADS * MLA_SLOT

    c0 = A_COLS + B_COLS
    group = SWA_Q_HEADS // SWA_KV_HEADS
    q_order = [g + group * kv for g in range(group) for kv in range(SWA_KV_HEADS)]
    q_heads = [w_in[..., c0 + h * HEAD_DIM:c0 + (h + 1) * HEAD_DIM] for h in q_order]
    w_main = jnp.concatenate([w_in[..., A_COLS:c0]] + q_heads + [w_in[..., c0 + SWA_Q_HEADS * HEAD_DIM:]], axis=-1)
    w_a = jnp.pad(w_in[..., :A_COLS], ((0, 0), (0, 0), (0, A_PAD - A_COLS)))

    q_cols = np.full((slot_w,), -1)
    qs_cols = np.full((slot_w,), -1)
    qs_sign = np.ones((slot_w,), np.float32)
    k_cols = np.full((slot_w,), -1)
    e_rows = np.full((slot_w,), -1)
    es_rows = np.full((slot_w,), -1)
    es_sign = np.ones((slot_w,), np.float32)
    for h in range(MLA_HEADS):
        base = h * MLA_SLOT
        q_cols[base:base + qk_dim] = h * qk_dim + np.arange(qk_dim)
        k_cols[base:base + MLA_NOPE] = h * (MLA_NOPE + MLA_V) + np.arange(MLA_NOPE)
        rope0 = base + MLA_NOPE
        qs_cols[rope0:rope0 + half] = h * qk_dim + MLA_NOPE + half + np.arange(half)
        qs_sign[rope0:rope0 + half] = -1.0
        qs_cols[rope0 + half:rope0 + MLA_ROPE] = h * qk_dim + MLA_NOPE + np.arange(half)
        e_rows[rope0:rope0 + MLA_ROPE] = np.arange(MLA_ROPE)
        es_rows[rope0:rope0 + half] = half + np.arange(half)
        es_sign[rope0:rope0 + half] = -1.0
        es_rows[rope0 + half:rope0 + MLA_ROPE] = np.arange(half)
    v_cols = np.full((MLA_HEADS, MLA_VROWS), -1)
    v_cols[:, :MLA_V] = np.arange(MLA_HEADS)[:, None] * (MLA_NOPE + MLA_V) + MLA_NOPE + np.arange(MLA_V)
    v_cols = v_cols.reshape(-1)
    eye = jnp.eye(MLA_SLOT, dtype=F32)

    def place(rows, sign=None):
        m = _gather_cols(eye, rows, sign)
        return m.at[MLA_ROPE:2 * MLA_ROPE].set(m[:MLA_ROPE])

    wk = jnp.concatenate([_gather_cols(w_kv_up, k_cols),
                          jnp.broadcast_to(place(e_rows), (DEPTH, MLA_SLOT, slot_w))], axis=1)

    row = lambda a: a[:, None, :]
    gate_half = jnp.asarray(np.concatenate([np.ones(D_FF, np.float32), np.full(D_FF, 0.5, np.float32)]))
    stacked = {
        "ln_attn": row(ln_attn), "q_norm": row(q_norm), "kv_norm": row(kv_norm),
        "w_main": w_main.astype(BF16),
        "w_a": w_a.astype(BF16),
        "wq": _gather_cols(w_q_up, q_cols).astype(BF16),
        "wq_s": _gather_cols(w_q_up, qs_cols, qs_sign).astype(BF16),
        "wk": wk.astype(BF16),
        "wv_t": jnp.swapaxes(_gather_cols(w_kv_up, v_cols), -1, -2).astype(BF16),
        "w_gate": w_gate.astype(BF16), "w_branch": w_branch.astype(BF16), "w_out": w_out.astype(BF16),
        "ln_ffn": row(ln_ffn), "w_ffn_up": w_ffn_up.astype(BF16), "ffn_conv_w": ffn_conv_w * gate_half,
        "ffn_conv_b": row(ffn_conv_b * gate_half), "w_ffn_down": w_ffn_down.astype(BF16),
        "ln_ple": row(ln_ple), "w_ple_gate": w_ple_gate.astype(BF16), "w_ple_proj": w_ple_proj.astype(BF16),
    }
    shared = {"e_s": place(es_rows, es_sign).astype(BF16)}
    return [dict({k: v[i] for k, v in stacked.items()}, **shared) for i in range(DEPTH)]


def _rope_tables(seq):
    inv = ROPE_THETA ** (-jnp.arange(0, MLA_ROPE, 2, dtype=F32) / MLA_ROPE)
    ang = jnp.arange(seq, dtype=F32)[:, None] * inv[None]
    cos, sin = jnp.cos(ang), jnp.sin(ang)
    pad = MLA_SLOT - MLA_NOPE - MLA_ROPE
    cos_t = jnp.concatenate([jnp.ones((seq, MLA_NOPE), F32), cos, cos, jnp.zeros((seq, pad), F32)], axis=1)
    sin_t = jnp.concatenate([jnp.zeros((seq, MLA_NOPE), F32), sin, sin, jnp.zeros((seq, pad), F32)], axis=1)
    return cos_t, sin_t


def _t5_bucket(rel):
    nb = REL_BUCKETS // 2
    max_exact = nb // 2
    ret = jnp.where(rel > 0, nb, 0)
    n = jnp.abs(rel)
    nf = jnp.maximum(n, 1).astype(F32)
    large = max_exact + (jnp.log(nf / max_exact) / math.log(REL_MAX_DIST / max_exact) * (nb - max_exact)).astype(jnp.int32)
    large = jnp.minimum(large, nb - 1)
    return ret + jnp.where(n < max_exact, n, large)


def _band_bias(rel_table, window, dil, head0):
    nk = BAND_UNIT + 2 * window
    off = np.arange(-(window + BAND_UNIT - 1), window + BAND_UNIT)
    picked = jax.nn.one_hot(_t5_bucket(jnp.asarray(off * dil)), REL_BUCKETS, dtype=F32)
    per_off = jnp.dot(picked, rel_table[:, head0:head0 + DIL_SLOTS].astype(F32), precision=lax.Precision.HIGHEST)
    per_off = jnp.where(jnp.asarray(np.abs(off) <= window)[:, None], per_off * LOG2_E, NEG_INF)
    cols = [per_off[BAND_UNIT - 1 - q:BAND_UNIT - 1 - q + nk] for q in range(BAND_UNIT)]
    return jnp.transpose(jnp.stack(cols, axis=0), (1, 2, 0)).reshape(nk, DIL_SLOTS * BAND_UNIT)


def _na_bias(rpb):
    qc = np.arange(GRID_W)[:, None]
    kc = np.arange(GRID_W)[None, :]
    sc = np.clip(qc - NA_COLS // 2, 0, GRID_W - NA_COLS)
    valid = (kc >= sc) & (kc < sc + NA_COLS)
    edge = GRID_W - NA_COLS
    ext = jnp.concatenate([jnp.repeat(rpb[..., :1], edge, axis=-1), rpb, jnp.repeat(rpb[..., -1:], edge, axis=-1)],
                          axis=-1).astype(F32)
    table = jnp.stack([ext[..., GRID_W - 1 - q:2 * GRID_W - 1 - q] for q in range(GRID_W)], axis=2)
    table = jnp.where(jnp.asarray(valid)[None, None], table, NEG_INF)
    variants = [jnp.transpose(table[:, v:v + NA_ROWS], (0, 2, 1, 3)).reshape(NA_HEADS * GRID_W, NA_ROWS * GRID_W)
                for v in range(NA_ROWS)]
    return jnp.stack(variants)


def _tables(seq, attn_sink, na_rpb, rel_table):
    cos_t, sin_t = _rope_tables(seq)
    return {
        "cos": cos_t, "sin": sin_t,
        "dil_bias": [_band_bias(rel_table, window // (2 * dil), dil, gi * DIL_SLOTS)
                     for gi, (window, dil) in enumerate(DIL_PAIRS)],
        "swa_bias": _band_bias(rel_table, SWA_WINDOW, 1, DIL_HEADS),
        "sink": [jnp.repeat(attn_sink[i].astype(F32) * LOG2_E, BAND_UNIT)[None, :] for i in range(DEPTH)],
        "na_bias": [_na_bias(na_rpb[i]) for i in range(DEPTH)],
    }


TILE_PROJ = 1024
TILE_WIDE = 512
TILE_MLA_Q, TILE_MLA_K = 1024, 2048
TILE_BAND = 1024
NA_ROWS_PER_STEP = 8


def _trunk(x, p, weights, tables, ln_final):
    bsz, seq, _ = x.shape
    tile_proj = min(TILE_PROJ, seq)
    for i in range(DEPTH):
        w = weights[i]
        qa, ka, vt, zb0, zb1, zb2, zc, zd = _attn_in(x, w, tables["cos"], tables["sin"], tile_proj)
        oa = _mla(qa, ka, vt, TILE_MLA_Q, TILE_MLA_K)
        obs, lses = [], []
        for gi, zb in enumerate((zb0[:, None], zb1, zb2)):
            halo = DIL_PAIRS[gi][0] // (2 * DIL_PAIRS[gi][1])
            o, lse = _banded(zb, tables["dil_bias"][gi], None, halo=halo, tm=TILE_BAND, kv_heads=DIL_SLOTS,
                             with_lse=True)
            obs.append(o)
            lses.append(lse)
        (oc,) = _banded(zc[:, None], tables["swa_bias"], tables["sink"][i], halo=SWA_WINDOW, tm=TILE_BAND,
                        kv_heads=SWA_KV_HEADS, with_lse=False)
        oc = oc.reshape(bsz, seq, BRANCH_W)
        od = _na(zd, tables["na_bias"][i], rows_per_step=NA_ROWS_PER_STEP)
        x = _merge(x, w, oa, obs, lses, oc, od, TILE_WIDE)
        x = _ffn(x, w, TILE_WIDE)
        x = _ple(x, p, i, w, ln_final, tile_proj, final=(i == DEPTH - 1))
    return x


def kernel(x_prompt, x_sample, p_prompt, p_sample, ln_attn, w_in, q_norm, kv_norm, w_q_up, w_kv_up,
           attn_sink, na_rpb, rel_table, w_gate, w_branch, w_out, ln_ffn, w_ffn_up, ffn_conv_w,
           ffn_conv_b, w_ffn_down, ln_ple, w_ple_gate, w_ple_proj, ln_final):
    weights = _prep_weights(ln_attn, w_in, q_norm, kv_norm, w_q_up, w_kv_up, w_gate, w_branch, w_out,
                            ln_ffn, w_ffn_up, ffn_conv_w, ffn_conv_b, w_ffn_down, ln_ple, w_ple_gate,
                            w_ple_proj)
    assert x_prompt.shape[1] == x_sample.shape[1]
    tables = _tables(x_prompt.shape[1], attn_sink, na_rpb, rel_table)
    ln_final = ln_final[None, :]
    return (_trunk(x_prompt, p_prompt, weights, tables, ln_final),
            _trunk(x_sample, p_sample, weights, tables, ln_final))
```

```python
import functools
import math

import jax
import jax.numpy as jnp
import numpy as np
from jax import lax
from jax.experimental import pallas as pl
from jax.experimental.pallas import tpu as pltpu

D_MODEL = 1024
DEPTH = 4
PLE_DIM = 256
GRID_W = 64
HEAD_DIM = 64
BRANCH_W = 256
N_BRANCH = 4
EPS = 1e-6
NEG_INF = -1e30

MLA_HEADS = 4
MLA_Q_RANK = 256
MLA_KV_RANK = 128
MLA_NOPE = 64
MLA_ROPE = 32
MLA_V = 64
ROPE_THETA = 10000.0
MLA_SLOT = 128
MLA_VROWS = 96
LANES = 128
LOG2_E = math.log2(math.e)

DIL_PAIRS = ((128, 1), (512, 4), (2048, 16))
DIL_SLOTS = 4
DIL_HEADS = DIL_SLOTS * len(DIL_PAIRS)

SWA_Q_HEADS = 4
SWA_KV_HEADS = 2
SWA_WINDOW = 128

NA_HEADS = 4
NA_ROWS = 8
NA_COLS = 16

REL_BUCKETS = 32
REL_MAX_DIST = 1024

D_FF = 2816
CONV_W = 3

A_COLS = MLA_Q_RANK + MLA_KV_RANK + MLA_ROPE
B_COLS = 3 * DIL_HEADS * HEAD_DIM
C_COLS = (SWA_Q_HEADS + 2 * SWA_KV_HEADS) * HEAD_DIM
D_COLS = 3 * NA_HEADS * HEAD_DIM
A_PAD = 512
MAIN_COLS = B_COLS + C_COLS + D_COLS
GROUP_COLS = 3 * BRANCH_W
MAIN_SPLITS = (0, GROUP_COLS, 2 * GROUP_COLS, B_COLS, B_COLS + C_COLS, MAIN_COLS)

BF16 = jnp.bfloat16
F32 = jnp.float32

VMEM_LIMIT_BYTES = 56 * 1024 * 1024


def _params(*semantics):
    return pltpu.CompilerParams(dimension_semantics=semantics, vmem_limit_bytes=VMEM_LIMIT_BYTES)


def _resident(shape):
    return pl.BlockSpec(shape, lambda *_: (0,) * len(shape), pipeline_mode=pl.Buffered(1))


def _dot(a, b):
    return jnp.dot(a, b, preferred_element_type=F32)


def _dot_nt(a, b):
    return lax.dot_general(a, b, (((1,), (1,)), ((), ())), preferred_element_type=F32)


def _rms(x, g):
    return x * lax.rsqrt(jnp.mean(x * x, axis=-1, keepdims=True) + EPS) * g


def _split_residues(val, scr, out_ref, dil):
    tm = val.shape[0]
    nslab = val.shape[1] // LANES
    for j in range(nslab):
        scr[j] = val[:, j * LANES:(j + 1) * LANES]
    for r in range(dil):
        for j in range(nslab):
            out_ref[r, :, j * LANES:(j + 1) * LANES] = scr[j, pl.ds(r, tm // dil, stride=dil), :].astype(BF16)


def _attn_in_kernel(x_ref, g_ref, wmain_ref, wa_ref, qn_ref, kvn_ref, wq_ref, wqs_ref, wk_ref,
                    wvt_ref, es_ref, cos_ref, sin_ref,
                    qa_ref, ka_ref, vt_ref, zb0_ref, zb1_ref, zb2_ref, zc_ref, zd_ref, scr1, scr2):
    hb = _rms(x_ref[...], g_ref[...]).astype(BF16)
    chunk = lambda c: _dot(hb, wmain_ref[:, MAIN_SPLITS[c]:MAIN_SPLITS[c + 1]])
    zb0_ref[...] = chunk(0).astype(BF16)
    _split_residues(chunk(1), scr1, zb1_ref, DIL_PAIRS[1][1])
    _split_residues(chunk(2), scr2, zb2_ref, DIL_PAIRS[2][1])
    zc_ref[...] = chunk(3).astype(BF16)
    zd_ref[...] = chunk(4).astype(BF16)

    za = _dot(hb, wa_ref[...])
    cq = _rms(za[:, :MLA_Q_RANK], qn_ref[...]).astype(BF16)
    ckv = _rms(za[:, MLA_Q_RANK:MLA_Q_RANK + MLA_KV_RANK], kvn_ref[...]).astype(BF16)
    kr = za[:, MLA_Q_RANK + MLA_KV_RANK:]
    kr_hi = kr.astype(BF16).astype(F32)
    kr2 = (kr_hi + pltpu.roll(kr - kr_hi, MLA_ROPE, 1)).astype(BF16)
    cos = jnp.concatenate([cos_ref[...]] * MLA_HEADS, axis=1)
    sin = jnp.concatenate([sin_ref[...]] * MLA_HEADS, axis=1)
    q = _dot(cq, wq_ref[...]) * cos + _dot(cq, wqs_ref[...]) * sin
    qa_ref[...] = (q * (LOG2_E * (MLA_NOPE + MLA_ROPE) ** -0.5)).astype(BF16)
    k_plain = _dot(jnp.concatenate([ckv, kr2], axis=1), wk_ref[...])
    ka_ref[...] = (k_plain * cos + _dot(kr2, es_ref[...]) * sin).astype(BF16)
    vt = _dot_nt(wvt_ref[...], ckv)
    row = lax.broadcasted_iota(jnp.int32, vt.shape, 0)
    vt_ref[...] = jnp.where(row % MLA_VROWS == MLA_V, 1.0, vt).astype(BF16)


def _attn_in(x, w, cos_t, sin_t, tm):
    bsz, seq, _ = x.shape
    tok = lambda width: pl.BlockSpec((None, tm, width), lambda b, i: (b, i, 0))
    table = pl.BlockSpec((tm, MLA_SLOT), lambda b, i: (i, 0))
    group = GROUP_COLS
    slot_w = MLA_HEADS * MLA_SLOT
    out_specs = [tok(slot_w), tok(slot_w),
                 pl.BlockSpec((None, MLA_HEADS * MLA_VROWS, tm), lambda b, i: (b, 0, i)), tok(group)]
    out_shape = [jax.ShapeDtypeStruct((bsz, seq, slot_w), BF16), jax.ShapeDtypeStruct((bsz, seq, slot_w), BF16),
                 jax.ShapeDtypeStruct((bsz, MLA_HEADS * MLA_VROWS, seq), BF16),
                 jax.ShapeDtypeStruct((bsz, seq, group), BF16)]
    scratch = []
    for _, dil in DIL_PAIRS[1:]:
        out_specs.append(pl.BlockSpec((None, dil, tm // dil, group), lambda b, i: (b, 0, i, 0)))
        out_shape.append(jax.ShapeDtypeStruct((bsz, dil, seq // dil, group), BF16))
        scratch.append(pltpu.VMEM((group // LANES, tm, LANES), F32))
    out_specs += [tok(C_COLS), tok(D_COLS)]
    out_shape += [jax.ShapeDtypeStruct((bsz, seq, C_COLS), BF16), jax.ShapeDtypeStruct((bsz, seq, D_COLS), BF16)]
    return pl.pallas_call(
        _attn_in_kernel,
        grid=(bsz, seq // tm),
        in_specs=[tok(D_MODEL), _resident((1, D_MODEL)), _resident((D_MODEL, MAIN_COLS)),
                  _resident((D_MODEL, A_PAD)), _resident((1, MLA_Q_RANK)), _resident((1, MLA_KV_RANK)),
                  _resident((MLA_Q_RANK, slot_w)), _resident((MLA_Q_RANK, slot_w)),
                  _resident((MLA_KV_RANK + MLA_SLOT, slot_w)), _resident((MLA_HEADS * MLA_VROWS, MLA_KV_RANK)),
                  _resident((MLA_SLOT, slot_w)), table, table],
        out_specs=out_specs, out_shape=out_shape, scratch_shapes=scratch,
        compiler_params=_params("parallel", "parallel"),
        name="attn_in",
    )(x, w["ln_attn"], w["w_main"], w["w_a"], w["q_norm"], w["kv_norm"], w["wq"], w["wq_s"],
      w["wk"], w["wv_t"], w["e_s"], cos_t, sin_t)


def _mla_kernel(q_ref, k_ref, vt_ref, o_ref, m_scr, acc_scr):
    kv = pl.program_id(2)

    @pl.when(kv == 0)
    def _():
        m_scr[...] = jnp.full(m_scr.shape, NEG_INF, F32)
        acc_scr[...] = jnp.zeros(acc_scr.shape, F32)

    def scores(h):
        return _dot_nt(k_ref[:, h * MLA_SLOT:(h + 1) * MLA_SLOT], q_ref[:, h * MLA_SLOT:(h + 1) * MLA_SLOT])

    s_next = scores(0)
    for h in range(MLA_HEADS):
        s = s_next
        if h + 1 < MLA_HEADS:
            s_next = scores(h + 1)
        m_prev = m_scr[h]
        m_new = jnp.maximum(m_prev, jnp.max(s, axis=0, keepdims=True))
        p = jnp.exp2(s - m_new).astype(BF16)
        acc_scr[h] = jnp.exp2(m_prev - m_new) * acc_scr[h] + _dot(vt_ref[h * MLA_VROWS:(h + 1) * MLA_VROWS, :], p)
        m_scr[h] = m_new

    @pl.when(kv == pl.num_programs(2) - 1)
    def _():
        o_t = jnp.concatenate([acc_scr[h, :MLA_V, :] / acc_scr[h, MLA_V:MLA_V + 1, :] for h in range(MLA_HEADS)],
                              axis=0)
        o_ref[...] = o_t.T.astype(BF16)


def _mla(qa, ka, vt, tq, tk):
    bsz, seq, _ = qa.shape
    tq, tk = min(tq, seq), min(tk, seq)
    return pl.pallas_call(
        _mla_kernel,
        grid=(bsz, seq // tq, seq // tk),
        in_specs=[pl.BlockSpec((None, tq, MLA_HEADS * MLA_SLOT), lambda b, i, j: (b, i, 0)),
                  pl.BlockSpec((None, tk, MLA_HEADS * MLA_SLOT), lambda b, i, j: (b, j, 0)),
                  pl.BlockSpec((None, MLA_HEADS * MLA_VROWS, tk), lambda b, i, j: (b, 0, j))],
        out_specs=pl.BlockSpec((None, tq, MLA_HEADS * MLA_V), lambda b, i, j: (b, i, 0)),
        out_shape=jax.ShapeDtypeStruct((bsz, seq, MLA_HEADS * MLA_V), BF16),
        scratch_shapes=[pltpu.VMEM((MLA_HEADS, 1, tq), F32), pltpu.VMEM((MLA_HEADS, MLA_VROWS, tq), F32)],
        compiler_params=_params("parallel", "parallel", "arbitrary"),
        name="mla_attention",
    )(qa, ka, vt)


def _head_rows(x, nheads):
    head = lax.broadcasted_iota(jnp.int32, x.shape, 1) // HEAD_DIM
    xf = x.astype(F32)
    return jnp.concatenate([jnp.where(head == h, xf, 0.0) for h in range(nheads)], axis=0).astype(x.dtype)


def _head_cols(x, nheads):
    n = x.shape[0] // nheads
    head = lax.broadcasted_iota(jnp.int32, (n, x.shape[1]), 1) // HEAD_DIM
    out = x[:n]
    for h in range(1, nheads):
        out = jnp.where(head == h, x[h * n:(h + 1) * n], out)
    return out


BAND_UNIT = 128
ONES_ROWS = 16
LN_2 = math.log(2.0)


def _query_rows(q, kv_heads):
    if kv_heads == DIL_SLOTS:
        return _head_rows(q, DIL_SLOTS)
    kvw = kv_heads * HEAD_DIM
    slot = lax.broadcasted_iota(jnp.int32, (q.shape[0], kvw), 1) // HEAD_DIM
    tiles = [q[:, t * kvw:(t + 1) * kvw].astype(F32) for t in range(DIL_SLOTS // kv_heads)]
    rows = [jnp.where(slot == h // (DIL_SLOTS // kv_heads), tiles[h % (DIL_SLOTS // kv_heads)], 0.0)
            for h in range(DIL_SLOTS)]
    return jnp.concatenate(rows, axis=0).astype(q.dtype)


def _banded_kernel(*refs, halo, tm, seq_len, tile_axis, kv_heads, with_sink, with_lse):
    q_ref, kp_ref, km_ref, kn_ref, vp_ref, vm_ref, vn_ref, bias_ref = refs[:8]
    refs = refs[8:]
    sink_ref = None
    if with_sink:
        sink_ref, refs = refs[0], refs[1:]
    o_ref = refs[0]
    lse_ref = refs[1] if with_lse else None

    tile = pl.program_id(tile_axis)
    kvw = kv_heads * HEAD_DIM
    kcat = jnp.concatenate([kp_ref[...], km_ref[...], kn_ref[...]], axis=0)
    vcat = jnp.concatenate([vp_ref[...], vm_ref[...], vn_ref[...]], axis=0)
    vt = vcat.astype(F32).T
    ones = jnp.where(lax.broadcasted_iota(jnp.int32, (ONES_ROWS, vt.shape[1]), 0) == 0, 1.0, 0.0)
    vt = jnp.concatenate([vt, ones], axis=0).astype(BF16)
    nk = BAND_UNIT + 2 * halo
    nunit = tm // BAND_UNIT
    lanes = lambda x, h: x[:, h * BAND_UNIT:(h + 1) * BAND_UNIT]

    def scores(u):
        qbd = _query_rows(q_ref[u * BAND_UNIT:(u + 1) * BAND_UNIT, :], kv_heads)
        return _dot_nt(kcat[u * BAND_UNIT:u * BAND_UNIT + nk], qbd)

    raw = [scores(u) for u in range(nunit)]
    outs = []
    for u in range(nunit):
        s = raw[u] * (LOG2_E * HEAD_DIM ** -0.5) + bias_ref[...]
        if u == 0 or u == nunit - 1:
            kpos = tile * tm + u * BAND_UNIT - halo + lax.broadcasted_iota(jnp.int32, s.shape, 0)
            if u == 0:
                s = jnp.where(kpos >= 0, s, NEG_INF)
            if u == nunit - 1:
                s = jnp.where(kpos < seq_len, s, NEG_INF)
        m = jnp.max(s, axis=0, keepdims=True)
        if with_sink:
            m = jnp.maximum(m, sink_ref[...])
        p = jnp.exp2(s - m).astype(BF16)
        acc = _dot(vt[:, u * BAND_UNIT:u * BAND_UNIT + nk], p)
        l = acc[kvw:kvw + 1]
        if with_sink:
            l = l + jnp.exp2(sink_ref[...] - m)
        inv = 1.0 / l
        kv_row = lambda h: (h * kv_heads // DIL_SLOTS) * HEAD_DIM
        out_t = [lanes(acc[kv_row(h):kv_row(h) + HEAD_DIM], h) * lanes(inv, h) for h in range(DIL_SLOTS)]
        if with_lse:
            lse = m * LN_2 + jnp.log(l)
            out_t += [jnp.broadcast_to(lanes(lse, h), (HEAD_DIM, BAND_UNIT)) for h in range(DIL_SLOTS)]
        outs.append(jnp.concatenate(out_t, axis=0))
    for u in range(nunit):
        rows = slice(u * BAND_UNIT, (u + 1) * BAND_UNIT)
        out = outs[u].T
        o_ref[rows, :] = out[:, :BRANCH_W].astype(BF16)
        if with_lse:
            lse_ref[rows, :] = out[:, BRANCH_W:]


def _banded(z, bias, sink, *, halo, tm, kv_heads, with_lse):
    bsz, dil, sub, _ = z.shape
    tm = min(tm, sub)
    assert tm % BAND_UNIT == 0 and BAND_UNIT % halo == 0
    blk = halo
    per_tile = tm // blk
    last_blk = sub // blk - 1
    nk = BAND_UNIT + 2 * halo
    kvw = kv_heads * HEAD_DIM
    first_kv = BRANCH_W // kvw

    def main(j):
        return pl.BlockSpec((None, None, tm, kvw), lambda b, r, i: (b, r, i, first_kv + j))

    def prev(j):
        return pl.BlockSpec((None, None, blk, kvw),
                            lambda b, r, i: (b, r, jnp.maximum(i * per_tile - 1, 0), first_kv + j))

    def nxt(j):
        return pl.BlockSpec((None, None, blk, kvw),
                            lambda b, r, i: (b, r, jnp.minimum((i + 1) * per_tile, last_blk), first_kv + j))

    in_specs = [pl.BlockSpec((None, None, tm, BRANCH_W), lambda b, r, i: (b, r, i, 0)),
                prev(0), main(0), nxt(0), prev(1), main(1), nxt(1),
                _resident((nk, DIL_SLOTS * BAND_UNIT))]
    args = [z, z, z, z, z, z, z, bias]
    if sink is not None:
        in_specs.append(_resident((1, DIL_SLOTS * BAND_UNIT)))
        args.append(sink)
    out_spec = pl.BlockSpec((None, None, tm, BRANCH_W), lambda b, r, i: (b, r, i, 0))
    out_specs = [out_spec]
    out_shape = [jax.ShapeDtypeStruct((bsz, dil, sub, BRANCH_W), BF16)]
    if with_lse:
        out_specs.append(out_spec)
        out_shape.append(jax.ShapeDtypeStruct((bsz, dil, sub, BRANCH_W), F32))
    return pl.pallas_call(
        functools.partial(_banded_kernel, halo=halo, tm=tm, seq_len=sub, tile_axis=2, kv_heads=kv_heads,
                          with_sink=sink is not None, with_lse=with_lse),
        grid=(bsz, dil, sub // tm),
        in_specs=in_specs, out_specs=out_specs, out_shape=out_shape,
        compiler_params=_params("parallel", "parallel", "parallel"),
        name=f"banded_d{dil}_h{halo}_kv{kv_heads}",
    )(*args)


def _na_kernel(q_ref, k_ref, v_ref, bias_ref, o_ref, *, rows, rows_per_step):
    step = pl.program_id(1)
    nkeys = NA_ROWS * GRID_W

    def window(rr):
        r = step * rows_per_step + rr
        sr = jnp.clip(r - NA_ROWS // 2, 0, rows - NA_ROWS)
        return sr - r + NA_ROWS - 1, pl.multiple_of(sr * GRID_W, GRID_W)

    def scores(rr):
        qbd = _head_rows(q_ref[rr * GRID_W:(rr + 1) * GRID_W, :], NA_HEADS)
        return _dot_nt(qbd, k_ref[pl.ds(wins[rr][1], nkeys), :])

    wins = [window(rr) for rr in range(rows_per_step)]
    raw = [scores(rr) for rr in range(rows_per_step)]
    for rr in range(rows_per_step):
        variant, start = wins[rr]
        s = raw[rr] * (HEAD_DIM ** -0.5) + bias_ref[variant]
        p = jnp.exp(s - jnp.max(s, axis=1, keepdims=True))
        obd = _dot(p.astype(BF16), v_ref[pl.ds(start, nkeys), :]) / jnp.sum(p, axis=1, keepdims=True)
        o_ref[rr * GRID_W:(rr + 1) * GRID_W, :] = _head_cols(obd, NA_HEADS).astype(BF16)


def _na(zd, bias, rows_per_step):
    bsz, seq, _ = zd.shape
    rows = seq // GRID_W
    assert rows >= NA_ROWS and rows % rows_per_step == 0
    tq = rows_per_step * GRID_W
    whole = lambda j: pl.BlockSpec((None, seq, BRANCH_W), lambda b, i: (b, 0, j))
    return pl.pallas_call(
        functools.partial(_na_kernel, rows=rows, rows_per_step=rows_per_step),
        grid=(bsz, rows // rows_per_step),
        in_specs=[pl.BlockSpec((None, tq, BRANCH_W), lambda b, i: (b, i, 0)), whole(1), whole(2),
                  _resident((NA_ROWS, NA_HEADS * GRID_W, NA_ROWS * GRID_W))],
        out_specs=pl.BlockSpec((None, tq, BRANCH_W), lambda b, i: (b, i, 0)),
        out_shape=jax.ShapeDtypeStruct((bsz, seq, BRANCH_W), BF16),
        compiler_params=_params("parallel", "arbitrary"),
        name="neighborhood_attention",
    )(zd, zd, zd, bias)


def _join_residues(ref, scr):
    dil, rows, width = ref.shape
    for j in range(width // LANES):
        for r in range(dil):
            scr[j, pl.ds(r, rows, stride=dil), :] = ref[r, :, j * LANES:(j + 1) * LANES].astype(F32)
    return jnp.concatenate([scr[j] for j in range(width // LANES)], axis=1)


def _merge_kernel(x_ref, g_ref, oa_ref, ob0_ref, ob1_ref, ob2_ref, l0_ref, l1_ref, l2_ref,
                  oc_ref, od_ref, wg_ref, wb_ref, wo_ref, out_ref, *scratch):
    x = x_ref[...]
    hb = _rms(x, g_ref[...]).astype(BF16)
    outs = (ob0_ref[0].astype(F32), _join_residues(ob1_ref, scratch[0]), _join_residues(ob2_ref, scratch[1]))
    lses = (l0_ref[0], _join_residues(l1_ref, scratch[2]), _join_residues(l2_ref, scratch[3]))
    top = jnp.maximum(jnp.maximum(lses[0], lses[1]), lses[2])
    ws = [jnp.exp(l - top) for l in lses]
    den = ws[0] + ws[1] + ws[2]
    ob = sum((wgt / den) * o for wgt, o in zip(ws, outs))
    branches = (oa_ref[...], ob.astype(BF16), oc_ref[...], od_ref[...])
    merged = None
    for j, o in enumerate(branches):
        term = jax.nn.sigmoid(_dot(hb, wg_ref[j])) * _dot(o, wb_ref[j])
        merged = term if merged is None else merged + term
    out_ref[...] = x + _dot(merged.astype(BF16), wo_ref[...])


def _merge(x, w, oa, obs, lses, oc, od, tm):
    bsz, seq, _ = x.shape
    tok = lambda width: pl.BlockSpec((None, tm, width), lambda b, i: (b, i, 0))
    residues = [pl.BlockSpec((None, dil, tm // dil, BRANCH_W), lambda b, i: (b, 0, i, 0)) for _, dil in DIL_PAIRS]
    return pl.pallas_call(
        _merge_kernel,
        grid=(bsz, seq // tm),
        in_specs=[tok(D_MODEL), _resident((1, D_MODEL)), tok(BRANCH_W)] + residues + residues
                 + [tok(BRANCH_W), tok(BRANCH_W),
                    _resident((N_BRANCH, D_MODEL, D_MODEL)), _resident((N_BRANCH, BRANCH_W, D_MODEL)),
                    _resident((D_MODEL, D_MODEL))],
        out_specs=tok(D_MODEL),
        out_shape=jax.ShapeDtypeStruct(x.shape, F32),
        scratch_shapes=[pltpu.VMEM((BRANCH_W // LANES, tm, LANES), F32)] * 4,
        compiler_params=_params("parallel", "parallel"),
        name="merge",
    )(x, w["ln_attn"], oa, *obs, *lses, oc, od, w["w_gate"], w["w_branch"], w["w_out"])


SUBLANES = 8
GELU_C0 = math.sqrt(2.0 / math.pi)
GELU_C1 = GELU_C0 * 0.044715


def _ffn_kernel(x_ref, xp_ref, xn_ref, g_ref, wu_ref, cw_ref, cb_ref, wd_ref, out_ref, *, tm):
    tile = pl.program_id(1)
    g = g_ref[...]
    x = x_ref[...]
    hn = _rms(x, g).astype(BF16)
    keep_prev = jnp.where(tile > 0, 1.0, 0.0)
    keep_next = jnp.where(tile < pl.num_programs(1) - 1, 1.0, 0.0)
    halo = jnp.concatenate([_rms(xp_ref[...], g) * keep_prev, _rms(xn_ref[...], g) * keep_next], axis=0)
    wu = wu_ref[...]
    u = _dot(hn, wu)
    uh = _dot(halo.astype(BF16), wu)
    row = lax.broadcasted_iota(jnp.int32, (SUBLANES, u.shape[1]), 0)
    below = pltpu.roll(u, 1, 0)
    below = jnp.concatenate([jnp.where(row == 0, uh[SUBLANES - 1:SUBLANES], below[:SUBLANES]), below[SUBLANES:]], axis=0)
    above = pltpu.roll(u, tm - 1, 0)
    above = jnp.concatenate([above[:tm - SUBLANES],
                             jnp.where(row == SUBLANES - 1, uh[SUBLANES:SUBLANES + 1], above[tm - SUBLANES:])], axis=0)
    cw = cw_ref[...]
    y = below * cw[0:1] + u * cw[1:2] + above * cw[2:3] + cb_ref[...]
    a, half_b = y[:, :D_FF], y[:, D_FF:]
    act = a * (1.0 + jnp.tanh(a * (GELU_C0 + GELU_C1 * (a * a)))) * half_b
    out_ref[...] = x + _dot(act.astype(BF16), wd_ref[...])


def _ffn(x, w, tm):
    bsz, seq, _ = x.shape
    per_tile = tm // SUBLANES
    last = seq // SUBLANES - 1
    tok = pl.BlockSpec((None, tm, D_MODEL), lambda b, i: (b, i, 0))
    return pl.pallas_call(
        functools.partial(_ffn_kernel, tm=tm),
        grid=(bsz, seq // tm),
        in_specs=[tok,
                  pl.BlockSpec((None, SUBLANES, D_MODEL), lambda b, i: (b, jnp.maximum(i * per_tile - 1, 0), 0)),
                  pl.BlockSpec((None, SUBLANES, D_MODEL), lambda b, i: (b, jnp.minimum((i + 1) * per_tile, last), 0)),
                  _resident((1, D_MODEL)), _resident((D_MODEL, 2 * D_FF)), _resident((CONV_W, 2 * D_FF)),
                  _resident((1, 2 * D_FF)), _resident((D_FF, D_MODEL))],
        out_specs=tok,
        out_shape=jax.ShapeDtypeStruct(x.shape, F32),
        compiler_params=_params("parallel", "parallel"),
        name="conv_ffn",
    )(x, x, x, w["ln_ffn"], w["w_ffn_up"], w["ffn_conv_w"], w["ffn_conv_b"], w["w_ffn_down"])


def _ple_kernel(x_ref, p_ref, g_ref, wg_ref, wp_ref, gf_ref, out_ref, *, final):
    x = x_ref[...]
    gate = jax.nn.sigmoid(_dot(_rms(x, g_ref[...]).astype(BF16), wg_ref[...]))
    y = x + gate * _dot(p_ref[...].astype(BF16), wp_ref[...])
    if final:
        y = _rms(y, gf_ref[...])
    out_ref[...] = y


def _ple(x, p, layer, w, ln_final, tm, final):
    bsz, seq, _ = x.shape
    tok = lambda width: pl.BlockSpec((None, tm, width), lambda b, i: (b, i, 0))
    return pl.pallas_call(
        functools.partial(_ple_kernel, final=final),
        grid=(bsz, seq // tm),
        in_specs=[tok(D_MODEL), pl.BlockSpec((None, None, tm, PLE_DIM), lambda b, i: (layer, b, i, 0)),
                  _resident((1, D_MODEL)), _resident((D_MODEL, D_MODEL)),
                  _resident((PLE_DIM, D_MODEL)), _resident((1, D_MODEL))],
        out_specs=tok(D_MODEL),
        out_shape=jax.ShapeDtypeStruct(x.shape, F32),
        compiler_params=_params("parallel", "parallel"),
        name="ple_final" if final else "ple",
    )(x, p, w["ln_ple"], w["w_ple_gate"], w["w_ple_proj"], ln_final)


def _gather_cols(wmat, cols, sign=None):
    cols = np.asarray(cols)
    picked = jnp.take(wmat, jnp.asarray(np.maximum(cols, 0)), axis=-1)
    scale = (cols >= 0).astype(np.float32) * (1.0 if sign is None else np.asarray(sign, np.float32))
    return picked * jnp.asarray(scale)


def _prep_weights(ln_attn, w_in, q_norm, kv_norm, w_q_up, w_kv_up, w_gate, w_branch, w_out, ln_ffn,
                  w_ffn_up, ffn_conv_w, ffn_conv_b, w_ffn_down, ln_ple, w_ple_gate, w_ple_proj):
    half = MLA_ROPE // 2
    qk_dim = MLA_NOPE + MLA_ROPE
    slot_w = MLA_HEADS * MLA_SLOT

    c0 = A_COLS + B_COLS
    group = SWA_Q_HEADS // SWA_KV_HEADS
    q_order = [g + group * kv for g in range(group) for kv in range(SWA_KV_HEADS)]
    q_heads = [w_in[..., c0 + h * HEAD_DIM:c0 + (h + 1) * HEAD_DIM] for h in q_order]
    w_main = jnp.concatenate([w_in[..., A_COLS:c0]] + q_heads + [w_in[..., c0 + SWA_Q_HEADS * HEAD_DIM:]], axis=-1)
    w_a = jnp.pad(w_in[..., :A_COLS], ((0, 0), (0, 0), (0, A_PAD - A_COLS)))

    q_cols = np.full((slot_w,), -1)
    qs_cols = np.full((slot_w,), -1)
    qs_sign = np.ones((slot_w,), np.float32)
    k_cols = np.full((slot_w,), -1)
    e_rows = np.full((slot_w,), -1)
    es_rows = np.full((slot_w,), -1)
    es_sign = np.ones((slot_w,), np.float32)
    for h in range(MLA_HEADS):
        base = h * MLA_SLOT
        q_cols[base:base + qk_dim] = h * qk_dim + np.arange(qk_dim)
        k_cols[base:base + MLA_NOPE] = h * (MLA_NOPE + MLA_V) + np.arange(MLA_NOPE)
        rope0 = base + MLA_NOPE
        qs_cols[rope0:rope0 + half] = h * qk_dim + MLA_NOPE + half + np.arange(half)
        qs_sign[rope0:rope0 + half] = -1.0
        qs_cols[rope0 + half:rope0 + MLA_ROPE] = h * qk_dim + MLA_NOPE + np.arange(half)
        e_rows[rope0:rope0 + MLA_ROPE] = np.arange(MLA_ROPE)
        es_rows[rope0:rope0 + half] = half + np.arange(half)
        es_sign[rope0:rope0 + half] = -1.0
        es_rows[rope0 + half:rope0 + MLA_ROPE] = np.arange(half)
    v_cols = np.full((MLA_HEADS, MLA_VROWS), -1)
    v_cols[:, :MLA_V] = np.arange(MLA_HEADS)[:, None] * (MLA_NOPE + MLA_V) + MLA_NOPE + np.arange(MLA_V)
    v_cols = v_cols.reshape(-1)
    eye = jnp.eye(MLA_SLOT, dtype=F32)

    def place(rows, sign=None):
        m = _gather_cols(eye, rows, sign)
        return m.at[MLA_ROPE:2 * MLA_ROPE].set(m[:MLA_ROPE])

    wk = jnp.concatenate([_gather_cols(w_kv_up, k_cols),
                          jnp.broadcast_to(place(e_rows), (DEPTH, MLA_SLOT, slot_w))], axis=1)

    row = lambda a: a[:, None, :]
    gate_half = jnp.asarray(np.concatenate([np.ones(D_FF, np.float32), np.full(D_FF, 0.5, np.float32)]))
    stacked = {
        "ln_attn": row(ln_attn), "q_norm": row(q_norm), "kv_norm": row(kv_norm),
        "w_main": w_main.astype(BF16),
        "w_a": w_a.astype(BF16),
        "wq": _gather_cols(w_q_up, q_cols).astype(BF16),
        "wq_s": _gather_cols(w_q_up, qs_cols, qs_sign).astype(BF16),
        "wk": wk.astype(BF16),
        "wv_t": jnp.swapaxes(_gather_cols(w_kv_up, v_cols), -1, -2).astype(BF16),
        "w_gate": w_gate.astype(BF16), "w_branch": w_branch.astype(BF16), "w_out": w_out.astype(BF16),
        "ln_ffn": row(ln_ffn), "w_ffn_up": w_ffn_up.astype(BF16), "ffn_conv_w": ffn_conv_w * gate_half,
        "ffn_conv_b": row(ffn_conv_b * gate_half), "w_ffn_down": w_ffn_down.astype(BF16),
        "ln_ple": row(ln_ple), "w_ple_gate": w_ple_gate.astype(BF16), "w_ple_proj": w_ple_proj.astype(BF16),
    }
    shared = {"e_s": place(es_rows, es_sign).astype(BF16)}
    return [dict({k: v[i] for k, v in stacked.items()}, **shared) for i in range(DEPTH)]


def _rope_tables(seq):
    inv = ROPE_THETA ** (-jnp.arange(0, MLA_ROPE, 2, dtype=F32) / MLA_ROPE)
    ang = jnp.arange(seq, dtype=F32)[:, None] * inv[None]
    cos, sin = jnp.cos(ang), jnp.sin(ang)
    pad = MLA_SLOT - MLA_NOPE - MLA_ROPE
    cos_t = jnp.concatenate([jnp.ones((seq, MLA_NOPE), F32), cos, cos, jnp.zeros((seq, pad), F32)], axis=1)
    sin_t = jnp.concatenate([jnp.zeros((seq, MLA_NOPE), F32), sin, sin, jnp.zeros((seq, pad), F32)], axis=1)
    return cos_t, sin_t


def _t5_bucket(rel):
    nb = REL_BUCKETS // 2
    max_exact = nb // 2
    ret = jnp.where(rel > 0, nb, 0)
    n = jnp.abs(rel)
    nf = jnp.maximum(n, 1).astype(F32)
    large = max_exact + (jnp.log(nf / max_exact) / math.log(REL_MAX_DIST / max_exact) * (nb - max_exact)).astype(jnp.int32)
    large = jnp.minimum(large, nb - 1)
    return ret + jnp.where(n < max_exact, n, large)


def _band_bias(rel_table, window, dil, head0):
    nk = BAND_UNIT + 2 * window
    off = np.arange(-(window + BAND_UNIT - 1), window + BAND_UNIT)
    picked = jax.nn.one_hot(_t5_bucket(jnp.asarray(off * dil)), REL_BUCKETS, dtype=F32)
    per_off = jnp.dot(picked, rel_table[:, head0:head0 + DIL_SLOTS].astype(F32), precision=lax.Precision.HIGHEST)
    per_off = jnp.where(jnp.asarray(np.abs(off) <= window)[:, None], per_off * LOG2_E, NEG_INF)
    cols = [per_off[BAND_UNIT - 1 - q:BAND_UNIT - 1 - q + nk] for q in range(BAND_UNIT)]
    return jnp.transpose(jnp.stack(cols, axis=0), (1, 2, 0)).reshape(nk, DIL_SLOTS * BAND_UNIT)


def _na_bias(rpb):
    qc = np.arange(GRID_W)[:, None]
    kc = np.arange(GRID_W)[None, :]
    sc = np.clip(qc - NA_COLS // 2, 0, GRID_W - NA_COLS)
    valid = (kc >= sc) & (kc < sc + NA_COLS)
    edge = GRID_W - NA_COLS
    ext = jnp.concatenate([jnp.repeat(rpb[..., :1], edge, axis=-1), rpb, jnp.repeat(rpb[..., -1:], edge, axis=-1)],
                          axis=-1).astype(F32)
    table = jnp.stack([ext[..., GRID_W - 1 - q:2 * GRID_W - 1 - q] for q in range(GRID_W)], axis=2)
    table = jnp.where(jnp.asarray(valid)[None, None], table, NEG_INF)
    variants = [jnp.transpose(table[:, v:v + NA_ROWS], (0, 2, 1, 3)).reshape(NA_HEADS * GRID_W, NA_ROWS * GRID_W)
                for v in range(NA_ROWS)]
    return jnp.stack(variants)


def _tables(seq, attn_sink, na_rpb, rel_table):
    cos_t, sin_t = _rope_tables(seq)
    return {
        "cos": cos_t, "sin": sin_t,
        "dil_bias": [_band_bias(rel_table, window // (2 * dil), dil, gi * DIL_SLOTS)
                     for gi, (window, dil) in enumerate(DIL_PAIRS)],
        "swa_bias": _band_bias(rel_table, SWA_WINDOW, 1, DIL_HEADS),
        "sink": [jnp.repeat(attn_sink[i].astype(F32) * LOG2_E, BAND_UNIT)[None, :] for i in range(DEPTH)],
        "na_bias": [_na_bias(na_rpb[i]) for i in range(DEPTH)],
    }


TILE_PROJ = 1024
TILE_WIDE = 512
TILE_MLA_Q, TILE_MLA_K = 1024, 2048
TILE_BAND = 2048
NA_ROWS_PER_STEP = 16


def _trunk(x, p, weights, tables, ln_final):
    bsz, seq, _ = x.shape
    tile_proj = min(TILE_PROJ, seq)
    for i in range(DEPTH):
        w = weights[i]
        qa, ka, vt, zb0, zb1, zb2, zc, zd = _attn_in(x, w, tables["cos"], tables["sin"], tile_proj)
        oa = _mla(qa, ka, vt, TILE_MLA_Q, TILE_MLA_K)
        obs, lses = [], []
        for gi, zb in enumerate((zb0[:, None], zb1, zb2)):
            halo = DIL_PAIRS[gi][0] // (2 * DIL_PAIRS[gi][1])
            o, lse = _banded(zb, tables["dil_bias"][gi], None, halo=halo, tm=TILE_BAND, kv_heads=DIL_SLOTS,
                             with_lse=True)
            obs.append(o)
            lses.append(lse)
        (oc,) = _banded(zc[:, None], tables["swa_bias"], tables["sink"][i], halo=SWA_WINDOW, tm=TILE_BAND,
                        kv_heads=SWA_KV_HEADS, with_lse=False)
        oc = oc.reshape(bsz, seq, BRANCH_W)
        od = _na(zd, tables["na_bias"][i], rows_per_step=NA_ROWS_PER_STEP)
        x = _merge(x, w, oa, obs, lses, oc, od, TILE_WIDE)
        x = _ffn(x, w, TILE_WIDE)
        x = _ple(x, p, i, w, ln_final, tile_proj, final=(i == DEPTH - 1))
    return x


def kernel(x_prompt, x_sample, p_prompt, p_sample, ln_attn, w_in, q_norm, kv_norm, w_q_up, w_kv_up,
           attn_sink, na_rpb, rel_table, w_gate, w_branch, w_out, ln_ffn, w_ffn_up, ffn_conv_w,
           ffn_conv_b, w_ffn_down, ln_ple, w_ple_gate, w_ple_proj, ln_final):
    weights = _prep_weights(ln_attn, w_in, q_norm, kv_norm, w_q_up, w_kv_up, w_gate, w_branch, w_out,
                            ln_ffn, w_ffn_up, ffn_conv_w, ffn_conv_b, w_ffn_down, ln_ple, w_ple_gate,
                            w_ple_proj)
    assert x_prompt.shape[1] == x_sample.shape[1]
    tables = _tables(x_prompt.shape[1], attn_sink, na_rpb, rel_table)
    ln_final = ln_final[None, :]
    return (_trunk(x_prompt, p_prompt, weights, tables, ln_final),
            _trunk(x_sample, p_sample, weights, tables, ln_final))
```

```python
import functools
import math

import jax
import jax.numpy as jnp
import numpy as np
from jax import lax
from jax.experimental import pallas as pl
from jax.experimental.pallas import tpu as pltpu

D_MODEL = 1024
DEPTH = 4
PLE_DIM = 256
GRID_W = 64
HEAD_DIM = 64
BRANCH_W = 256
N_BRANCH = 4
EPS = 1e-6
NEG_INF = -1e30

MLA_HEADS = 4
MLA_Q_RANK = 256
MLA_KV_RANK = 128
MLA_NOPE = 64
MLA_ROPE = 32
MLA_V = 64
ROPE_THETA = 10000.0
MLA_SLOT = 128
MLA_VROWS = 96
LANES = 128
LOG2_E = math.log2(math.e)

DIL_PAIRS = ((128, 1), (512, 4), (2048, 16))
DIL_SLOTS = 4
DIL_HEADS = DIL_SLOTS * len(DIL_PAIRS)

SWA_Q_HEADS = 4
SWA_KV_HEADS = 2
SWA_WINDOW = 128

NA_HEADS = 4
NA_ROWS = 8
NA_COLS = 16

REL_BUCKETS = 32
REL_MAX_DIST = 1024

D_FF = 2816
CONV_W = 3

A_COLS = MLA_Q_RANK + MLA_KV_RANK + MLA_ROPE
B_COLS = 3 * DIL_HEADS * HEAD_DIM
C_COLS = (SWA_Q_HEADS + 2 * SWA_KV_HEADS) * HEAD_DIM
D_COLS = 3 * NA_HEADS * HEAD_DIM
A_PAD = 512
MAIN_COLS = B_COLS + C_COLS + D_COLS
GROUP_COLS = 3 * BRANCH_W
MAIN_SPLITS = (0, GROUP_COLS, 2 * GROUP_COLS, B_COLS, B_COLS + C_COLS, MAIN_COLS)

BF16 = jnp.bfloat16
F32 = jnp.float32

VMEM_LIMIT_BYTES = 56 * 1024 * 1024


def _params(*semantics):
    return pltpu.CompilerParams(dimension_semantics=semantics, vmem_limit_bytes=VMEM_LIMIT_BYTES)


def _resident(shape):
    return pl.BlockSpec(shape, lambda *_: (0,) * len(shape), pipeline_mode=pl.Buffered(1))


def _dot(a, b):
    return jnp.dot(a, b, preferred_element_type=F32)


def _dot_nt(a, b):
    return lax.dot_general(a, b, (((1,), (1,)), ((), ())), preferred_element_type=F32)


def _rms(x, g):
    return x * lax.rsqrt(jnp.mean(x * x, axis=-1, keepdims=True) + EPS) * g


def _split_residues(val, scr, out_ref, dil):
    tm = val.shape[0]
    nslab = val.shape[1] // LANES
    for j in range(nslab):
        scr[j] = val[:, j * LANES:(j + 1) * LANES]
    for r in range(dil):
        for j in range(nslab):
            out_ref[r, :, j * LANES:(j + 1) * LANES] = scr[j, pl.ds(r, tm // dil, stride=dil), :].astype(BF16)


def _attn_in_kernel(x_ref, g_ref, wmain_ref, wa_ref, qn_ref, kvn_ref, wq_ref, wqs_ref, wk_ref,
                    wvt_ref, es_ref, cos_ref, sin_ref,
                    qa_ref, ka_ref, vt_ref, zb0_ref, zb1_ref, zb2_ref, zc_ref, zd_ref, scr1, scr2):
    hb = _rms(x_ref[...], g_ref[...]).astype(BF16)
    chunk = lambda c: _dot(hb, wmain_ref[:, MAIN_SPLITS[c]:MAIN_SPLITS[c + 1]])
    zb0_ref[...] = chunk(0).astype(BF16)
    _split_residues(chunk(1), scr1, zb1_ref, DIL_PAIRS[1][1])
    _split_residues(chunk(2), scr2, zb2_ref, DIL_PAIRS[2][1])
    zc_ref[...] = chunk(3).astype(BF16)
    zd_ref[...] = chunk(4).astype(BF16)

    za = _dot(hb, wa_ref[...])
    cq = _rms(za[:, :MLA_Q_RANK], qn_ref[...]).astype(BF16)
    ckv = _rms(za[:, MLA_Q_RANK:MLA_Q_RANK + MLA_KV_RANK], kvn_ref[...]).astype(BF16)
    kr = za[:, MLA_Q_RANK + MLA_KV_RANK:]
    kr_hi = kr.astype(BF16).astype(F32)
    kr2 = (kr_hi + pltpu.roll(kr - kr_hi, MLA_ROPE, 1)).astype(BF16)
    cos = jnp.concatenate([cos_ref[...]] * MLA_HEADS, axis=1)
    sin = jnp.concatenate([sin_ref[...]] * MLA_HEADS, axis=1)
    q = _dot(cq, wq_ref[...]) * cos + _dot(cq, wqs_ref[...]) * sin
    qa_ref[...] = (q * (LOG2_E * (MLA_NOPE + MLA_ROPE) ** -0.5)).astype(BF16)
    k_plain = _dot(jnp.concatenate([ckv, kr2], axis=1), wk_ref[...])
    ka_ref[...] = (k_plain * cos + _dot(kr2, es_ref[...]) * sin).astype(BF16)
    vt = _dot_nt(wvt_ref[...], ckv)
    row = lax.broadcasted_iota(jnp.int32, vt.shape, 0)
    vt_ref[...] = jnp.where(row % MLA_VROWS == MLA_V, 1.0, vt).astype(BF16)


def _attn_in(x, w, cos_t, sin_t, tm):
    bsz, seq, _ = x.shape
    tok = lambda width: pl.BlockSpec((None, tm, width), lambda b, i: (b, i, 0))
    table = pl.BlockSpec((tm, MLA_SLOT), lambda b, i: (i, 0))
    group = GROUP_COLS
    slot_w = MLA_HEADS * MLA_SLOT
    out_specs = [tok(slot_w), tok(slot_w),
                 pl.BlockSpec((None, MLA_HEADS * MLA_VROWS, tm), lambda b, i: (b, 0, i)), tok(group)]
    out_shape = [jax.ShapeDtypeStruct((bsz, seq, slot_w), BF16), jax.ShapeDtypeStruct((bsz, seq, slot_w), BF16),
                 jax.ShapeDtypeStruct((bsz, MLA_HEADS * MLA_VROWS, seq), BF16),
                 jax.ShapeDtypeStruct((bsz, seq, group), BF16)]
    scratch = []
    for _, dil in DIL_PAIRS[1:]:
        out_specs.append(pl.BlockSpec((None, dil, tm // dil, group), lambda b, i: (b, 0, i, 0)))
        out_shape.append(jax.ShapeDtypeStruct((bsz, dil, seq // dil, group), BF16))
        scratch.append(pltpu.VMEM((group // LANES, tm, LANES), F32))
    out_specs += [tok(C_COLS), tok(D_COLS)]
    out_shape += [jax.ShapeDtypeStruct((bsz, seq, C_COLS), BF16), jax.ShapeDtypeStruct((bsz, seq, D_COLS), BF16)]
    return pl.pallas_call(
        _attn_in_kernel,
        grid=(bsz, seq // tm),
        in_specs=[tok(D_MODEL), _resident((1, D_MODEL)), _resident((D_MODEL, MAIN_COLS)),
                  _resident((D_MODEL, A_PAD)), _resident((1, MLA_Q_RANK)), _resident((1, MLA_KV_RANK)),
                  _resident((MLA_Q_RANK, slot_w)), _resident((MLA_Q_RANK, slot_w)),
                  _resident((MLA_KV_RANK + MLA_SLOT, slot_w)), _resident((MLA_HEADS * MLA_VROWS, MLA_KV_RANK)),
                  _resident((MLA_SLOT, slot_w)), table, table],
        out_specs=out_specs, out_shape=out_shape, scratch_shapes=scratch,
        compiler_params=_params("parallel", "parallel"),
        name="attn_in",
    )(x, w["ln_attn"], w["w_main"], w["w_a"], w["q_norm"], w["kv_norm"], w["wq"], w["wq_s"],
      w["wk"], w["wv_t"], w["e_s"], cos_t, sin_t)


def _mla_kernel(q_ref, k_ref, vt_ref, o_ref, m_scr, acc_scr):
    kv = pl.program_id(2)

    @pl.when(kv == 0)
    def _():
        m_scr[...] = jnp.full(m_scr.shape, NEG_INF, F32)
        acc_scr[...] = jnp.zeros(acc_scr.shape, F32)

    def scores(h):
        return _dot_nt(k_ref[:, h * MLA_SLOT:(h + 1) * MLA_SLOT], q_ref[:, h * MLA_SLOT:(h + 1) * MLA_SLOT])

    s_next = scores(0)
    for h in range(MLA_HEADS):
        s = s_next
        if h + 1 < MLA_HEADS:
            s_next = scores(h + 1)
        m_prev = m_scr[h]
        m_new = jnp.maximum(m_prev, jnp.max(s, axis=0, keepdims=True))
        p = jnp.exp2(s - m_new).astype(BF16)
        acc_scr[h] = jnp.exp2(m_prev - m_new) * acc_scr[h] + _dot(vt_ref[h * MLA_VROWS:(h + 1) * MLA_VROWS, :], p)
        m_scr[h] = m_new

    @pl.when(kv == pl.num_programs(2) - 1)
    def _():
        o_t = jnp.concatenate([acc_scr[h, :MLA_V, :] / acc_scr[h, MLA_V:MLA_V + 1, :] for h in range(MLA_HEADS)],
                              axis=0)
        o_ref[...] = o_t.T.astype(BF16)


def _mla(qa, ka, vt, tq, tk):
    bsz, seq, _ = qa.shape
    tq, tk = min(tq, seq), min(tk, seq)
    return pl.pallas_call(
        _mla_kernel,
        grid=(bsz, seq // tq, seq // tk),
        in_specs=[pl.BlockSpec((None, tq, MLA_HEADS * MLA_SLOT), lambda b, i, j: (b, i, 0)),
                  pl.BlockSpec((None, tk, MLA_HEADS * MLA_SLOT), lambda b, i, j: (b, j, 0)),
                  pl.BlockSpec((None, MLA_HEADS * MLA_VROWS, tk), lambda b, i, j: (b, 0, j))],
        out_specs=pl.BlockSpec((None, tq, MLA_HEADS * MLA_V), lambda b, i, j: (b, i, 0)),
        out_shape=jax.ShapeDtypeStruct((bsz, seq, MLA_HEADS * MLA_V), BF16),
        scratch_shapes=[pltpu.VMEM((MLA_HEADS, 1, tq), F32), pltpu.VMEM((MLA_HEADS, MLA_VROWS, tq), F32)],
        compiler_params=_params("parallel", "parallel", "arbitrary"),
        name="mla_attention",
    )(qa, ka, vt)


def _head_rows(x, nheads):
    head = lax.broadcasted_iota(jnp.int32, x.shape, 1) // HEAD_DIM
    xf = x.astype(F32)
    return jnp.concatenate([jnp.where(head == h, xf, 0.0) for h in range(nheads)], axis=0).astype(x.dtype)


def _head_cols(x, nheads):
    n = x.shape[0] // nheads
    head = lax.broadcasted_iota(jnp.int32, (n, x.shape[1]), 1) // HEAD_DIM
    out = x[:n]
    for h in range(1, nheads):
        out = jnp.where(head == h, x[h * n:(h + 1) * n], out)
    return out


BAND_UNIT = 128
ONES_ROWS = 16
LN_2 = math.log(2.0)


def _query_rows(q, kv_heads):
    if kv_heads == DIL_SLOTS:
        return _head_rows(q, DIL_SLOTS)
    kvw = kv_heads * HEAD_DIM
    slot = lax.broadcasted_iota(jnp.int32, (q.shape[0], kvw), 1) // HEAD_DIM
    tiles = [q[:, t * kvw:(t + 1) * kvw].astype(F32) for t in range(DIL_SLOTS // kv_heads)]
    rows = [jnp.where(slot == h // (DIL_SLOTS // kv_heads), tiles[h % (DIL_SLOTS // kv_heads)], 0.0)
            for h in range(DIL_SLOTS)]
    return jnp.concatenate(rows, axis=0).astype(q.dtype)


def _banded_kernel(*refs, halo, tm, seq_len, tile_axis, kv_heads, with_sink, with_lse):
    q_ref, kp_ref, km_ref, kn_ref, vp_ref, vm_ref, vn_ref, bias_ref = refs[:8]
    refs = refs[8:]
    sink_ref = None
    if with_sink:
        sink_ref, refs = refs[0], refs[1:]
    o_ref = refs[0]
    lse_ref = refs[1] if with_lse else None

    tile = pl.program_id(tile_axis)
    kvw = kv_heads * HEAD_DIM
    kcat = jnp.concatenate([kp_ref[...], km_ref[...], kn_ref[...]], axis=0)
    vcat = jnp.concatenate([vp_ref[...], vm_ref[...], vn_ref[...]], axis=0)
    vt = vcat.astype(F32).T
    ones = jnp.where(lax.broadcasted_iota(jnp.int32, (ONES_ROWS, vt.shape[1]), 0) == 0, 1.0, 0.0)
    vt = jnp.concatenate([vt, ones], axis=0).astype(BF16)
    nk = BAND_UNIT + 2 * halo
    nunit = tm // BAND_UNIT
    lanes = lambda x, h: x[:, h * BAND_UNIT:(h + 1) * BAND_UNIT]

    def scores(u):
        qbd = _query_rows(q_ref[u * BAND_UNIT:(u + 1) * BAND_UNIT, :], kv_heads)
        return _dot_nt(kcat[u * BAND_UNIT:u * BAND_UNIT + nk], qbd)

    raw = [scores(u) for u in range(nunit)]
    outs = []
    for u in range(nunit):
        s = raw[u] * (LOG2_E * HEAD_DIM ** -0.5) + bias_ref[...]
        if tm >= seq_len:
            start = (u * BAND_UNIT) % seq_len
            first, last = start == 0, start == seq_len - BAND_UNIT
        else:
            start = (tile % (seq_len // tm)) * tm + u * BAND_UNIT
            first, last = u == 0, u == nunit - 1
        if first or last:
            kpos = start - halo + lax.broadcasted_iota(jnp.int32, s.shape, 0)
            if first:
                s = jnp.where(kpos >= 0, s, NEG_INF)
            if last:
                s = jnp.where(kpos < seq_len, s, NEG_INF)
        m = jnp.max(s, axis=0, keepdims=True)
        if with_sink:
            m = jnp.maximum(m, sink_ref[...])
        p = jnp.exp2(s - m).astype(BF16)
        acc = _dot(vt[:, u * BAND_UNIT:u * BAND_UNIT + nk], p)
        l = acc[kvw:kvw + 1]
        if with_sink:
            l = l + jnp.exp2(sink_ref[...] - m)
        inv = 1.0 / l
        kv_row = lambda h: (h * kv_heads // DIL_SLOTS) * HEAD_DIM
        out_t = [lanes(acc[kv_row(h):kv_row(h) + HEAD_DIM], h) * lanes(inv, h) for h in range(DIL_SLOTS)]
        if with_lse:
            lse = m * LN_2 + jnp.log(l)
            out_t += [jnp.broadcast_to(lanes(lse, h), (HEAD_DIM, BAND_UNIT)) for h in range(DIL_SLOTS)]
        outs.append(jnp.concatenate(out_t, axis=0))
    for u in range(nunit):
        rows = slice(u * BAND_UNIT, (u + 1) * BAND_UNIT)
        out = outs[u].T
        o_ref[rows, :] = out[:, :BRANCH_W].astype(BF16)
        if with_lse:
            lse_ref[rows, :] = out[:, BRANCH_W:]


def _banded(z, bias, sink, *, halo, tm, kv_heads, with_lse):
    bsz, dil, sub, ncol = z.shape
    fold = max(1, min(dil, tm // sub))
    assert dil % fold == 0
    z = z.reshape(bsz, dil // fold, fold * sub, ncol)
    nres, rows = z.shape[1], z.shape[2]
    tm = min(tm, rows)
    assert tm % BAND_UNIT == 0 and BAND_UNIT % halo == 0 and (tm % sub == 0 or sub % tm == 0)
    blk = halo
    per_tile = tm // blk
    last_blk = rows // blk - 1
    nk = BAND_UNIT + 2 * halo
    kvw = kv_heads * HEAD_DIM
    first_kv = BRANCH_W // kvw

    def main(j):
        return pl.BlockSpec((None, None, tm, kvw), lambda b, r, i: (b, r, i, first_kv + j))

    def prev(j):
        return pl.BlockSpec((None, None, blk, kvw),
                            lambda b, r, i: (b, r, jnp.maximum(i * per_tile - 1, 0), first_kv + j))

    def nxt(j):
        return pl.BlockSpec((None, None, blk, kvw),
                            lambda b, r, i: (b, r, jnp.minimum((i + 1) * per_tile, last_blk), first_kv + j))

    in_specs = [pl.BlockSpec((None, None, tm, BRANCH_W), lambda b, r, i: (b, r, i, 0)),
                prev(0), main(0), nxt(0), prev(1), main(1), nxt(1),
                _resident((nk, DIL_SLOTS * BAND_UNIT))]
    args = [z, z, z, z, z, z, z, bias]
    if sink is not None:
        in_specs.append(_resident((1, DIL_SLOTS * BAND_UNIT)))
        args.append(sink)
    out_spec = pl.BlockSpec((None, None, tm, BRANCH_W), lambda b, r, i: (b, r, i, 0))
    out_specs = [out_spec]
    out_shape = [jax.ShapeDtypeStruct((bsz, nres, rows, BRANCH_W), BF16)]
    if with_lse:
        out_specs.append(out_spec)
        out_shape.append(jax.ShapeDtypeStruct((bsz, nres, rows, BRANCH_W), F32))
    outs = pl.pallas_call(
        functools.partial(_banded_kernel, halo=halo, tm=tm, seq_len=sub, tile_axis=2, kv_heads=kv_heads,
                          with_sink=sink is not None, with_lse=with_lse),
        grid=(bsz, nres, rows // tm),
        in_specs=in_specs, out_specs=out_specs, out_shape=out_shape,
        compiler_params=_params("parallel", "parallel", "parallel"),
        name=f"banded_d{dil}_h{halo}_kv{kv_heads}",
    )(*args)
    return [o.reshape(bsz, dil, sub, BRANCH_W) for o in outs]


def _na_kernel(q_ref, k_ref, v_ref, bias_ref, o_ref, *, rows, rows_per_step):
    step = pl.program_id(1)
    nkeys = NA_ROWS * GRID_W

    def window(rr):
        r = step * rows_per_step + rr
        sr = jnp.clip(r - NA_ROWS // 2, 0, rows - NA_ROWS)
        return sr - r + NA_ROWS - 1, pl.multiple_of(sr * GRID_W, GRID_W)

    def scores(rr):
        qbd = _head_rows(q_ref[rr * GRID_W:(rr + 1) * GRID_W, :], NA_HEADS)
        return _dot_nt(qbd, k_ref[pl.ds(wins[rr][1], nkeys), :])

    wins = [window(rr) for rr in range(rows_per_step)]
    raw = [scores(rr) for rr in range(rows_per_step)]
    for rr in range(rows_per_step):
        variant, start = wins[rr]
        s = raw[rr] * (HEAD_DIM ** -0.5) + bias_ref[variant]
        p = jnp.exp(s - jnp.max(s, axis=1, keepdims=True))
        obd = _dot(p.astype(BF16), v_ref[pl.ds(start, nkeys), :]) / jnp.sum(p, axis=1, keepdims=True)
        o_ref[rr * GRID_W:(rr + 1) * GRID_W, :] = _head_cols(obd, NA_HEADS).astype(BF16)


def _na(zd, bias, rows_per_step):
    bsz, seq, _ = zd.shape
    rows = seq // GRID_W
    assert rows >= NA_ROWS and rows % rows_per_step == 0
    tq = rows_per_step * GRID_W
    whole = lambda j: pl.BlockSpec((None, seq, BRANCH_W), lambda b, i: (b, 0, j))
    return pl.pallas_call(
        functools.partial(_na_kernel, rows=rows, rows_per_step=rows_per_step),
        grid=(bsz, rows // rows_per_step),
        in_specs=[pl.BlockSpec((None, tq, BRANCH_W), lambda b, i: (b, i, 0)), whole(1), whole(2),
                  _resident((NA_ROWS, NA_HEADS * GRID_W, NA_ROWS * GRID_W))],
        out_specs=pl.BlockSpec((None, tq, BRANCH_W), lambda b, i: (b, i, 0)),
        out_shape=jax.ShapeDtypeStruct((bsz, seq, BRANCH_W), BF16),
        compiler_params=_params("parallel", "arbitrary"),
        name="neighborhood_attention",
    )(zd, zd, zd, bias)


def _join_residues(ref, scr):
    dil, rows, width = ref.shape
    for j in range(width // LANES):
        for r in range(dil):
            scr[j, pl.ds(r, rows, stride=dil), :] = ref[r, :, j * LANES:(j + 1) * LANES].astype(F32)
    return jnp.concatenate([scr[j] for j in range(width // LANES)], axis=1)


def _merge_kernel(x_ref, g_ref, oa_ref, ob0_ref, ob1_ref, ob2_ref, l0_ref, l1_ref, l2_ref,
                  oc_ref, od_ref, wg_ref, wb_ref, wo_ref, out_ref, *scratch):
    x = x_ref[...]
    hb = _rms(x, g_ref[...]).astype(BF16)
    outs = (ob0_ref[0].astype(F32), _join_residues(ob1_ref, scratch[0]), _join_residues(ob2_ref, scratch[1]))
    lses = (l0_ref[0], _join_residues(l1_ref, scratch[2]), _join_residues(l2_ref, scratch[3]))
    top = jnp.maximum(jnp.maximum(lses[0], lses[1]), lses[2])
    ws = [jnp.exp(l - top) for l in lses]
    den = ws[0] + ws[1] + ws[2]
    ob = sum((wgt / den) * o for wgt, o in zip(ws, outs))
    branches = (oa_ref[...], ob.astype(BF16), oc_ref[...], od_ref[...])
    merged = None
    for j, o in enumerate(branches):
        term = jax.nn.sigmoid(_dot(hb, wg_ref[j])) * _dot(o, wb_ref[j])
        merged = term if merged is None else merged + term
    out_ref[...] = x + _dot(merged.astype(BF16), wo_ref[...])


def _merge(x, w, oa, obs, lses, oc, od, tm):
    bsz, seq, _ = x.shape
    tok = lambda width: pl.BlockSpec((None, tm, width), lambda b, i: (b, i, 0))
    residues = [pl.BlockSpec((None, dil, tm // dil, BRANCH_W), lambda b, i: (b, 0, i, 0)) for _, dil in DIL_PAIRS]
    return pl.pallas_call(
        _merge_kernel,
        grid=(bsz, seq // tm),
        in_specs=[tok(D_MODEL), _resident((1, D_MODEL)), tok(BRANCH_W)] + residues + residues
                 + [tok(BRANCH_W), tok(BRANCH_W),
                    _resident((N_BRANCH, D_MODEL, D_MODEL)), _resident((N_BRANCH, BRANCH_W, D_MODEL)),
                    _resident((D_MODEL, D_MODEL))],
        out_specs=tok(D_MODEL),
        out_shape=jax.ShapeDtypeStruct(x.shape, F32),
        scratch_shapes=[pltpu.VMEM((BRANCH_W // LANES, tm, LANES), F32)] * 4,
        compiler_params=_params("parallel", "parallel"),
        name="merge",
    )(x, w["ln_attn"], oa, *obs, *lses, oc, od, w["w_gate"], w["w_branch"], w["w_out"])


SUBLANES = 8
GELU_C0 = math.sqrt(2.0 / math.pi)
GELU_C1 = GELU_C0 * 0.044715


def _ffn_kernel(x_ref, xp_ref, xn_ref, g_ref, wu_ref, cw_ref, cb_ref, wd_ref, out_ref, *, tm):
    tile = pl.program_id(1)
    g = g_ref[...]
    x = x_ref[...]
    hn = _rms(x, g).astype(BF16)
    keep_prev = jnp.where(tile > 0, 1.0, 0.0)
    keep_next = jnp.where(tile < pl.num_programs(1) - 1, 1.0, 0.0)
    halo = jnp.concatenate([_rms(xp_ref[...], g) * keep_prev, _rms(xn_ref[...], g) * keep_next], axis=0)
    wu = wu_ref[...]
    u = _dot(hn, wu)
    uh = _dot(halo.astype(BF16), wu)
    row = lax.broadcasted_iota(jnp.int32, (SUBLANES, u.shape[1]), 0)
    below = pltpu.roll(u, 1, 0)
    below = jnp.concatenate([jnp.where(row == 0, uh[SUBLANES - 1:SUBLANES], below[:SUBLANES]), below[SUBLANES:]], axis=0)
    above = pltpu.roll(u, tm - 1, 0)
    above = jnp.concatenate([above[:tm - SUBLANES],
                             jnp.where(row == SUBLANES - 1, uh[SUBLANES:SUBLANES + 1], above[tm - SUBLANES:])], axis=0)
    cw = cw_ref[...]
    y = below * cw[0:1] + u * cw[1:2] + above * cw[2:3] + cb_ref[...]
    a, half_b = y[:, :D_FF], y[:, D_FF:]
    act = a * (1.0 + jnp.tanh(a * (GELU_C0 + GELU_C1 * (a * a)))) * half_b
    out_ref[...] = x + _dot(act.astype(BF16), wd_ref[...])


def _ffn(x, w, tm):
    bsz, seq, _ = x.shape
    per_tile = tm // SUBLANES
    last = seq // SUBLANES - 1
    tok = pl.BlockSpec((None, tm, D_MODEL), lambda b, i: (b, i, 0))
    return pl.pallas_call(
        functools.partial(_ffn_kernel, tm=tm),
        grid=(bsz, seq // tm),
        in_specs=[tok,
                  pl.BlockSpec((None, SUBLANES, D_MODEL), lambda b, i: (b, jnp.maximum(i * per_tile - 1, 0), 0)),
                  pl.BlockSpec((None, SUBLANES, D_MODEL), lambda b, i: (b, jnp.minimum((i + 1) * per_tile, last), 0)),
                  _resident((1, D_MODEL)), _resident((D_MODEL, 2 * D_FF)), _resident((CONV_W, 2 * D_FF)),
                  _resident((1, 2 * D_FF)), _resident((D_FF, D_MODEL))],
        out_specs=tok,
        out_shape=jax.ShapeDtypeStruct(x.shape, F32),
        compiler_params=_params("parallel", "parallel"),
        name="conv_ffn",
    )(x, x, x, w["ln_ffn"], w["w_ffn_up"], w["ffn_conv_w"], w["ffn_conv_b"], w["w_ffn_down"])


def _ple_kernel(x_ref, p_ref, g_ref, wg_ref, wp_ref, gf_ref, out_ref, *, final):
    x = x_ref[...]
    gate = jax.nn.sigmoid(_dot(_rms(x, g_ref[...]).astype(BF16), wg_ref[...]))
    y = x + gate * _dot(p_ref[...].astype(BF16), wp_ref[...])
    if final:
        y = _rms(y, gf_ref[...])
    out_ref[...] = y


def _ple(x, p, layer, w, ln_final, tm, final):
    bsz, seq, _ = x.shape
    tok = lambda width: pl.BlockSpec((None, tm, width), lambda b, i: (b, i, 0))
    return pl.pallas_call(
        functools.partial(_ple_kernel, final=final),
        grid=(bsz, seq // tm),
        in_specs=[tok(D_MODEL), pl.BlockSpec((None, None, tm, PLE_DIM), lambda b, i: (layer, b, i, 0)),
                  _resident((1, D_MODEL)), _resident((D_MODEL, D_MODEL)),
                  _resident((PLE_DIM, D_MODEL)), _resident((1, D_MODEL))],
        out_specs=tok(D_MODEL),
        out_shape=jax.ShapeDtypeStruct(x.shape, F32),
        compiler_params=_params("parallel", "parallel"),
        name="ple_final" if final else "ple",
    )(x, p, w["ln_ple"], w["w_ple_gate"], w["w_ple_proj"], ln_final)


def _gather_cols(wmat, cols, sign=None):
    cols = np.asarray(cols)
    picked = jnp.take(wmat, jnp.asarray(np.maximum(cols, 0)), axis=-1)
    scale = (cols >= 0).astype(np.float32) * (1.0 if sign is None else np.asarray(sign, np.float32))
    return picked * jnp.asarray(scale)


def _prep_weights(ln_attn, w_in, q_norm, kv_norm, w_q_up, w_kv_up, w_gate, w_branch, w_out, ln_ffn,
                  w_ffn_up, ffn_conv_w, ffn_conv_b, w_ffn_down, ln_ple, w_ple_gate, w_ple_proj):
    half = MLA_ROPE // 2
    qk_dim = MLA_NOPE + MLA_ROPE
    slot_w = MLA_HEADS * MLA_SLOT

    c0 = A_COLS + B_COLS
    group = SWA_Q_HEADS // SWA_KV_HEADS
    q_order = [g + group * kv for g in range(group) for kv in range(SWA_KV_HEADS)]
    q_heads = [w_in[..., c0 + h * HEAD_DIM:c0 + (h + 1) * HEAD_DIM] for h in q_order]
    w_main = jnp.concatenate([w_in[..., A_COLS:c0]] + q_heads + [w_in[..., c0 + SWA_Q_HEADS * HEAD_DIM:]], axis=-1)
    w_a = jnp.pad(w_in[..., :A_COLS], ((0, 0), (0, 0), (0, A_PAD - A_COLS)))

    q_cols = np.full((slot_w,), -1)
    qs_cols = np.full((slot_w,), -1)
    qs_sign = np.ones((slot_w,), np.float32)
    k_cols = np.full((slot_w,), -1)
    e_rows = np.full((slot_w,), -1)
    es_rows = np.full((slot_w,), -1)
    es_sign = np.ones((slot_w,), np.float32)
    for h in range(MLA_HEADS):
        base = h * MLA_SLOT
        q_cols[base:base + qk_dim] = h * qk_dim + np.arange(qk_dim)
        k_cols[base:base + MLA_NOPE] = h * (MLA_NOPE + MLA_V) + np.arange(MLA_NOPE)
        rope0 = base + MLA_NOPE
        qs_cols[rope0:rope0 + half] = h * qk_dim + MLA_NOPE + half + np.arange(half)
        qs_sign[rope0:rope0 + half] = -1.0
        qs_cols[rope0 + half:rope0 + MLA_ROPE] = h * qk_dim + MLA_NOPE + np.arange(half)
        e_rows[rope0:rope0 + MLA_ROPE] = np.arange(MLA_ROPE)
        es_rows[rope0:rope0 + half] = half + np.arange(half)
        es_sign[rope0:rope0 + half] = -1.0
        es_rows[rope0 + half:rope0 + MLA_ROPE] = np.arange(half)
    v_cols = np.full((MLA_HEADS, MLA_VROWS), -1)
    v_cols[:, :MLA_V] = np.arange(MLA_HEADS)[:, None] * (MLA_NOPE + MLA_V) + MLA_NOPE + np.arange(MLA_V)
    v_cols = v_cols.reshape(-1)
    eye = jnp.eye(MLA_SLOT, dtype=F32)

    def place(rows, sign=None):
        m = _gather_cols(eye, rows, sign)
        return m.at[MLA_ROPE:2 * MLA_ROPE].set(m[:MLA_ROPE])

    wk = jnp.concatenate([_gather_cols(w_kv_up, k_cols),
                          jnp.broadcast_to(place(e_rows), (DEPTH, MLA_SLOT, slot_w))], axis=1)

    row = lambda a: a[:, None, :]
    gate_half = jnp.asarray(np.concatenate([np.ones(D_FF, np.float32), np.full(D_FF, 0.5, np.float32)]))
    stacked = {
        "ln_attn": row(ln_attn), "q_norm": row(q_norm), "kv_norm": row(kv_norm),
        "w_main": w_main.astype(BF16),
        "w_a": w_a.astype(BF16),
        "wq": _gather_cols(w_q_up, q_cols).astype(BF16),
        "wq_s": _gather_cols(w_q_up, qs_cols, qs_sign).astype(BF16),
        "wk": wk.astype(BF16),
        "wv_t": jnp.swapaxes(_gather_cols(w_kv_up, v_cols), -1, -2).astype(BF16),
        "w_gate": w_gate.astype(BF16), "w_branch": w_branch.astype(BF16), "w_out": w_out.astype(BF16),
        "ln_ffn": row(ln_ffn), "w_ffn_up": w_ffn_up.astype(BF16), "ffn_conv_w": ffn_conv_w * gate_half,
        "ffn_conv_b": row(ffn_conv_b * gate_half), "w_ffn_down": w_ffn_down.astype(BF16),
        "ln_ple": row(ln_ple), "w_ple_gate": w_ple_gate.astype(BF16), "w_ple_proj": w_ple_proj.astype(BF16),
    }
    shared = {"e_s": place(es_rows, es_sign).astype(BF16)}
    return [dict({k: v[i] for k, v in stacked.items()}, **shared) for i in range(DEPTH)]


def _rope_tables(seq):
    inv = ROPE_THETA ** (-jnp.arange(0, MLA_ROPE, 2, dtype=F32) / MLA_ROPE)
    ang = jnp.arange(seq, dtype=F32)[:, None] * inv[None]
    cos, sin = jnp.cos(ang), jnp.sin(ang)
    pad = MLA_SLOT - MLA_NOPE - MLA_ROPE
    cos_t = jnp.concatenate([jnp.ones((seq, MLA_NOPE), F32), cos, cos, jnp.zeros((seq, pad), F32)], axis=1)
    sin_t = jnp.concatenate([jnp.zeros((seq, MLA_NOPE), F32), sin, sin, jnp.zeros((seq, pad), F32)], axis=1)
    return cos_t, sin_t


def _t5_bucket(rel):
    nb = REL_BUCKETS // 2
    max_exact = nb // 2
    ret = jnp.where(rel > 0, nb, 0)
    n = jnp.abs(rel)
    nf = jnp.maximum(n, 1).astype(F32)
    large = max_exact + (jnp.log(nf / max_exact) / math.log(REL_MAX_DIST / max_exact) * (nb - max_exact)).astype(jnp.int32)
    large = jnp.minimum(large, nb - 1)
    return ret + jnp.where(n < max_exact, n, large)


def _band_bias(rel_table, window, dil, head0):
    nk = BAND_UNIT + 2 * window
    off = np.arange(-(window + BAND_UNIT - 1), window + BAND_UNIT)
    picked = jax.nn.one_hot(_t5_bucket(jnp.asarray(off * dil)), REL_BUCKETS, dtype=F32)
    per_off = jnp.dot(picked, rel_table[:, head0:head0 + DIL_SLOTS].astype(F32), precision=lax.Precision.HIGHEST)
    per_off = jnp.where(jnp.asarray(np.abs(off) <= window)[:, None], per_off * LOG2_E, NEG_INF)
    cols = [per_off[BAND_UNIT - 1 - q:BAND_UNIT - 1 - q + nk] for q in range(BAND_UNIT)]
    return jnp.transpose(jnp.stack(cols, axis=0), (1, 2, 0)).reshape(nk, DIL_SLOTS * BAND_UNIT)


def _na_bias(rpb):
    qc = np.arange(GRID_W)[:, None]
    kc = np.arange(GRID_W)[None, :]
    sc = np.clip(qc - NA_COLS // 2, 0, GRID_W - NA_COLS)
    valid = (kc >= sc) & (kc < sc + NA_COLS)
    edge = GRID_W - NA_COLS
    ext = jnp.concatenate([jnp.repeat(rpb[..., :1], edge, axis=-1), rpb, jnp.repeat(rpb[..., -1:], edge, axis=-1)],
                          axis=-1).astype(F32)
    table = jnp.stack([ext[..., GRID_W - 1 - q:2 * GRID_W - 1 - q] for q in range(GRID_W)], axis=2)
    table = jnp.where(jnp.asarray(valid)[None, None], table, NEG_INF)
    variants = [jnp.transpose(table[:, v:v + NA_ROWS], (0, 2, 1, 3)).reshape(NA_HEADS * GRID_W, NA_ROWS * GRID_W)
                for v in range(NA_ROWS)]
    return jnp.stack(variants)


def _tables(seq, attn_sink, na_rpb, rel_table):
    cos_t, sin_t = _rope_tables(seq)
    return {
        "cos": cos_t, "sin": sin_t,
        "dil_bias": [_band_bias(rel_table, window // (2 * dil), dil, gi * DIL_SLOTS)
                     for gi, (window, dil) in enumerate(DIL_PAIRS)],
        "swa_bias": _band_bias(rel_table, SWA_WINDOW, 1, DIL_HEADS),
        "sink": [jnp.repeat(attn_sink[i].astype(F32) * LOG2_E, BAND_UNIT)[None, :] for i in range(DEPTH)],
        "na_bias": [_na_bias(na_rpb[i]) for i in range(DEPTH)],
    }


TILE_PROJ = 1024
TILE_WIDE = 512
TILE_MLA_Q, TILE_MLA_K = 1024, 2048
TILE_BAND = 2048
NA_ROWS_PER_STEP = 32


def _trunk(x, p, weights, tables, ln_final):
    bsz, seq, _ = x.shape
    tile_proj = min(TILE_PROJ, seq)
    for i in range(DEPTH):
        w = weights[i]
        qa, ka, vt, zb0, zb1, zb2, zc, zd = _attn_in(x, w, tables["cos"], tables["sin"], tile_proj)
        oa = _mla(qa, ka, vt, TILE_MLA_Q, TILE_MLA_K)
        obs, lses = [], []
        for gi, zb in enumerate((zb0[:, None], zb1, zb2)):
            halo = DIL_PAIRS[gi][0] // (2 * DIL_PAIRS[gi][1])
            o, lse = _banded(zb, tables["dil_bias"][gi], None, halo=halo, tm=TILE_BAND, kv_heads=DIL_SLOTS,
                             with_lse=True)
            obs.append(o)
            lses.append(lse)
        (oc,) = _banded(zc[:, None], tables["swa_bias"], tables["sink"][i], halo=SWA_WINDOW, tm=TILE_BAND,
                        kv_heads=SWA_KV_HEADS, with_lse=False)
        oc = oc.reshape(bsz, seq, BRANCH_W)
        od = _na(zd, tables["na_bias"][i], rows_per_step=NA_ROWS_PER_STEP)
        x = _merge(x, w, oa, obs, lses, oc, od, TILE_WIDE)
        x = _ffn(x, w, TILE_WIDE)
        x = _ple(x, p, i, w, ln_final, tile_proj, final=(i == DEPTH - 1))
    return x


def kernel(x_prompt, x_sample, p_prompt, p_sample, ln_attn, w_in, q_norm, kv_norm, w_q_up, w_kv_up,
           attn_sink, na_rpb, rel_table, w_gate, w_branch, w_out, ln_ffn, w_ffn_up, ffn_conv_w,
           ffn_conv_b, w_ffn_down, ln_ple, w_ple_gate, w_ple_proj, ln_final):
    weights = _prep_weights(ln_attn, w_in, q_norm, kv_norm, w_q_up, w_kv_up, w_gate, w_branch, w_out,
                            ln_ffn, w_ffn_up, ffn_conv_w, ffn_conv_b, w_ffn_down, ln_ple, w_ple_gate,
                            w_ple_proj)
    assert x_prompt.shape[1] == x_sample.shape[1]
    tables = _tables(x_prompt.shape[1], attn_sink, na_rpb, rel_table)
    ln_final = ln_final[None, :]
    return (_trunk(x_prompt, p_prompt, weights, tables, ln_final),
            _trunk(x_sample, p_sample, weights, tables, ln_final))
```

```python
import functools
import math

import jax
import jax.numpy as jnp
import numpy as np
from jax import lax
from jax.experimental import pallas as pl
from jax.experimental.pallas import tpu as pltpu

D_MODEL = 1024
DEPTH = 4
PLE_DIM = 256
GRID_W = 64
HEAD_DIM = 64
BRANCH_W = 256
N_BRANCH = 4
EPS = 1e-6
NEG_INF = -1e30

MLA_HEADS = 4
MLA_Q_RANK = 256
MLA_KV_RANK = 128
MLA_NOPE = 64
MLA_ROPE = 32
MLA_V = 64
ROPE_THETA = 10000.0
MLA_SLOT = 128
MLA_VROWS = 96
LANES = 128
LOG2_E = math.log2(math.e)

DIL_PAIRS = ((128, 1), (512, 4), (2048, 16))
DIL_SLOTS = 4
DIL_HEADS = DIL_SLOTS * len(DIL_PAIRS)

SWA_Q_HEADS = 4
SWA_KV_HEADS = 2
SWA_WINDOW = 128

NA_HEADS = 4
NA_ROWS = 8
NA_COLS = 16

REL_BUCKETS = 32
REL_MAX_DIST = 1024

D_FF = 2816
CONV_W = 3

A_COLS = MLA_Q_RANK + MLA_KV_RANK + MLA_ROPE
B_COLS = 3 * DIL_HEADS * HEAD_DIM
C_COLS = (SWA_Q_HEADS + 2 * SWA_KV_HEADS) * HEAD_DIM
D_COLS = 3 * NA_HEADS * HEAD_DIM
A_PAD = 512
MAIN_COLS = B_COLS + C_COLS + D_COLS
GROUP_COLS = 3 * BRANCH_W
MAIN_SPLITS = (0, GROUP_COLS, 2 * GROUP_COLS, B_COLS, B_COLS + C_COLS, MAIN_COLS)

BF16 = jnp.bfloat16
F32 = jnp.float32

VMEM_LIMIT_BYTES = 56 * 1024 * 1024


def _params(*semantics):
    return pltpu.CompilerParams(dimension_semantics=semantics, vmem_limit_bytes=VMEM_LIMIT_BYTES)


def _resident(shape):
    return pl.BlockSpec(shape, lambda *_: (0,) * len(shape), pipeline_mode=pl.Buffered(1))


def _dot(a, b):
    return jnp.dot(a, b, preferred_element_type=F32)


def _dot_nt(a, b):
    return lax.dot_general(a, b, (((1,), (1,)), ((), ())), preferred_element_type=F32)


def _rms(x, g):
    return x * lax.rsqrt(jnp.mean(x * x, axis=-1, keepdims=True) + EPS) * g


def _split_residues(val, scr, out_ref, dil):
    tm = val.shape[0]
    nslab = val.shape[1] // LANES
    for j in range(nslab):
        scr[j] = val[:, j * LANES:(j + 1) * LANES]
    for r in range(dil):
        for j in range(nslab):
            out_ref[r, :, j * LANES:(j + 1) * LANES] = scr[j, pl.ds(r, tm // dil, stride=dil), :].astype(BF16)


def _attn_in_kernel(x_ref, g_ref, wmain_ref, wa_ref, qn_ref, kvn_ref, wq_ref, wqs_ref, wk_ref,
                    wvt_ref, es_ref, cos_ref, sin_ref,
                    qa_ref, ka_ref, vt_ref, zb0_ref, zb1_ref, zb2_ref, zc_ref, zd_ref, scr1, scr2):
    hb = _rms(x_ref[...], g_ref[...]).astype(BF16)
    za = _dot(hb, wa_ref[...])
    chunk = lambda c: _dot(hb, wmain_ref[:, MAIN_SPLITS[c]:MAIN_SPLITS[c + 1]])
    zb0_ref[...] = chunk(0).astype(BF16)
    _split_residues(chunk(1), scr1, zb1_ref, DIL_PAIRS[1][1])
    _split_residues(chunk(2), scr2, zb2_ref, DIL_PAIRS[2][1])
    zc_ref[...] = chunk(3).astype(BF16)
    zd_ref[...] = chunk(4).astype(BF16)

    cq = _rms(za[:, :MLA_Q_RANK], qn_ref[...]).astype(BF16)
    ckv = _rms(za[:, MLA_Q_RANK:MLA_Q_RANK + MLA_KV_RANK], kvn_ref[...]).astype(BF16)
    kr = za[:, MLA_Q_RANK + MLA_KV_RANK:]
    kr_hi = kr.astype(BF16).astype(F32)
    kr2 = (kr_hi + pltpu.roll(kr - kr_hi, MLA_ROPE, 1)).astype(BF16)
    cos = jnp.concatenate([cos_ref[...]] * MLA_HEADS, axis=1)
    sin = jnp.concatenate([sin_ref[...]] * MLA_HEADS, axis=1)
    q = _dot(cq, wq_ref[...]) * cos + _dot(cq, wqs_ref[...]) * sin
    qa_ref[...] = (q * (LOG2_E * (MLA_NOPE + MLA_ROPE) ** -0.5)).astype(BF16)
    k_plain = _dot(jnp.concatenate([ckv, kr2], axis=1), wk_ref[...])
    ka_ref[...] = (k_plain * cos + _dot(kr2, es_ref[...]) * sin).astype(BF16)
    vt = _dot_nt(wvt_ref[...], ckv)
    row = lax.broadcasted_iota(jnp.int32, vt.shape, 0)
    vt_ref[...] = jnp.where(row % MLA_VROWS == MLA_V, 1.0, vt).astype(BF16)


def _attn_in(x, w, cos_t, sin_t, tm):
    bsz, seq, _ = x.shape
    tok = lambda width: pl.BlockSpec((None, tm, width), lambda b, i: (b, i, 0))
    table = pl.BlockSpec((tm, MLA_SLOT), lambda b, i: (i, 0))
    group = GROUP_COLS
    slot_w = MLA_HEADS * MLA_SLOT
    out_specs = [tok(slot_w), tok(slot_w),
                 pl.BlockSpec((None, MLA_HEADS * MLA_VROWS, tm), lambda b, i: (b, 0, i)), tok(group)]
    out_shape = [jax.ShapeDtypeStruct((bsz, seq, slot_w), BF16), jax.ShapeDtypeStruct((bsz, seq, slot_w), BF16),
                 jax.ShapeDtypeStruct((bsz, MLA_HEADS * MLA_VROWS, seq), BF16),
                 jax.ShapeDtypeStruct((bsz, seq, group), BF16)]
    scratch = []
    for _, dil in DIL_PAIRS[1:]:
        out_specs.append(pl.BlockSpec((None, dil, tm // dil, group), lambda b, i: (b, 0, i, 0)))
        out_shape.append(jax.ShapeDtypeStruct((bsz, dil, seq // dil, group), BF16))
        scratch.append(pltpu.VMEM((group // LANES, tm, LANES), F32))
    out_specs += [tok(C_COLS), tok(D_COLS)]
    out_shape += [jax.ShapeDtypeStruct((bsz, seq, C_COLS), BF16), jax.ShapeDtypeStruct((bsz, seq, D_COLS), BF16)]
    return pl.pallas_call(
        _attn_in_kernel,
        grid=(bsz, seq // tm),
        in_specs=[tok(D_MODEL), _resident((1, D_MODEL)), _resident((D_MODEL, MAIN_COLS)),
                  _resident((D_MODEL, A_PAD)), _resident((1, MLA_Q_RANK)), _resident((1, MLA_KV_RANK)),
                  _resident((MLA_Q_RANK, slot_w)), _resident((MLA_Q_RANK, slot_w)),
                  _resident((MLA_KV_RANK + MLA_SLOT, slot_w)), _resident((MLA_HEADS * MLA_VROWS, MLA_KV_RANK)),
                  _resident((MLA_SLOT, slot_w)), table, table],
        out_specs=out_specs, out_shape=out_shape, scratch_shapes=scratch,
        compiler_params=_params("parallel", "parallel"),
        name="attn_in",
    )(x, w["ln_attn"], w["w_main"], w["w_a"], w["q_norm"], w["kv_norm"], w["wq"], w["wq_s"],
      w["wk"], w["wv_t"], w["e_s"], cos_t, sin_t)


def _mla_kernel(q_ref, k_ref, vt_ref, o_ref, m_scr, acc_scr):
    kv = pl.program_id(2)

    @pl.when(kv == 0)
    def _():
        m_scr[...] = jnp.full(m_scr.shape, NEG_INF, F32)
        acc_scr[...] = jnp.zeros(acc_scr.shape, F32)

    def scores(h):
        return _dot_nt(k_ref[:, h * MLA_SLOT:(h + 1) * MLA_SLOT], q_ref[:, h * MLA_SLOT:(h + 1) * MLA_SLOT])

    s_next = scores(0)
    for h in range(MLA_HEADS):
        s = s_next
        if h + 1 < MLA_HEADS:
            s_next = scores(h + 1)
        m_prev = m_scr[h]
        m_new = jnp.maximum(m_prev, jnp.max(s, axis=0, keepdims=True))
        p = jnp.exp2(s - m_new).astype(BF16)
        acc_scr[h] = jnp.exp2(m_prev - m_new) * acc_scr[h] + _dot(vt_ref[h * MLA_VROWS:(h + 1) * MLA_VROWS, :], p)
        m_scr[h] = m_new

    @pl.when(kv == pl.num_programs(2) - 1)
    def _():
        o_t = jnp.concatenate([acc_scr[h, :MLA_V, :] / acc_scr[h, MLA_V:MLA_V + 1, :] for h in range(MLA_HEADS)],
                              axis=0)
        o_ref[...] = o_t.T.astype(BF16)


def _mla(qa, ka, vt, tq, tk):
    bsz, seq, _ = qa.shape
    tq, tk = min(tq, seq), min(tk, seq)
    return pl.pallas_call(
        _mla_kernel,
        grid=(bsz, seq // tq, seq // tk),
        in_specs=[pl.BlockSpec((None, tq, MLA_HEADS * MLA_SLOT), lambda b, i, j: (b, i, 0)),
                  pl.BlockSpec((None, tk, MLA_HEADS * MLA_SLOT), lambda b, i, j: (b, j, 0)),
                  pl.BlockSpec((None, MLA_HEADS * MLA_VROWS, tk), lambda b, i, j: (b, 0, j))],
        out_specs=pl.BlockSpec((None, tq, MLA_HEADS * MLA_V), lambda b, i, j: (b, i, 0)),
        out_shape=jax.ShapeDtypeStruct((bsz, seq, MLA_HEADS * MLA_V), BF16),
        scratch_shapes=[pltpu.VMEM((MLA_HEADS, 1, tq), F32), pltpu.VMEM((MLA_HEADS, MLA_VROWS, tq), F32)],
        compiler_params=_params("parallel", "parallel", "arbitrary"),
        name="mla_attention",
    )(qa, ka, vt)


def _head_rows(x, nheads):
    head = lax.broadcasted_iota(jnp.int32, x.shape, 1) // HEAD_DIM
    xf = x.astype(F32)
    return jnp.concatenate([jnp.where(head == h, xf, 0.0) for h in range(nheads)], axis=0).astype(x.dtype)


def _head_cols(x, nheads):
    n = x.shape[0] // nheads
    head = lax.broadcasted_iota(jnp.int32, (n, x.shape[1]), 1) // HEAD_DIM
    out = x[:n]
    for h in range(1, nheads):
        out = jnp.where(head == h, x[h * n:(h + 1) * n], out)
    return out


BAND_UNIT = 128
ONES_ROWS = 16
LN_2 = math.log(2.0)


def _query_rows(q, kv_heads):
    if kv_heads == DIL_SLOTS:
        return _head_rows(q, DIL_SLOTS)
    kvw = kv_heads * HEAD_DIM
    slot = lax.broadcasted_iota(jnp.int32, (q.shape[0], kvw), 1) // HEAD_DIM
    tiles = [q[:, t * kvw:(t + 1) * kvw].astype(F32) for t in range(DIL_SLOTS // kv_heads)]
    rows = [jnp.where(slot == h // (DIL_SLOTS // kv_heads), tiles[h % (DIL_SLOTS // kv_heads)], 0.0)
            for h in range(DIL_SLOTS)]
    return jnp.concatenate(rows, axis=0).astype(q.dtype)


def _banded_kernel(*refs, halo, tm, seq_len, tile_axis, kv_heads, with_sink, with_lse):
    q_ref, kp_ref, km_ref, kn_ref, vp_ref, vm_ref, vn_ref, bias_ref = refs[:8]
    refs = refs[8:]
    sink_ref = None
    if with_sink:
        sink_ref, refs = refs[0], refs[1:]
    o_ref = refs[0]
    lse_ref = refs[1] if with_lse else None

    tile = pl.program_id(tile_axis)
    kvw = kv_heads * HEAD_DIM
    kcat = jnp.concatenate([kp_ref[...], km_ref[...], kn_ref[...]], axis=0)
    vcat = jnp.concatenate([vp_ref[...], vm_ref[...], vn_ref[...]], axis=0)
    vt = vcat.astype(F32).T
    ones = jnp.where(lax.broadcasted_iota(jnp.int32, (ONES_ROWS, vt.shape[1]), 0) == 0, 1.0, 0.0)
    vt = jnp.concatenate([vt, ones], axis=0).astype(BF16)
    nk = BAND_UNIT + 2 * halo
    nunit = tm // BAND_UNIT
    lanes = lambda x, h: x[:, h * BAND_UNIT:(h + 1) * BAND_UNIT]

    def scores(u):
        qbd = _query_rows(q_ref[u * BAND_UNIT:(u + 1) * BAND_UNIT, :], kv_heads)
        return _dot_nt(kcat[u * BAND_UNIT:u * BAND_UNIT + nk], qbd)

    raw = [scores(u) for u in range(nunit)]
    outs = []
    for u in range(nunit):
        s = raw[u] * (LOG2_E * HEAD_DIM ** -0.5) + bias_ref[...]
        if tm >= seq_len:
            start = (u * BAND_UNIT) % seq_len
            first, last = start == 0, start == seq_len - BAND_UNIT
        else:
            start = (tile % (seq_len // tm)) * tm + u * BAND_UNIT
            first, last = u == 0, u == nunit - 1
        if first or last:
            kpos = start - halo + lax.broadcasted_iota(jnp.int32, s.shape, 0)
            if first:
                s = jnp.where(kpos >= 0, s, NEG_INF)
            if last:
                s = jnp.where(kpos < seq_len, s, NEG_INF)
        m = jnp.max(s, axis=0, keepdims=True)
        if with_sink:
            m = jnp.maximum(m, sink_ref[...])
        p = jnp.exp2(s - m).astype(BF16)
        acc = _dot(vt[:, u * BAND_UNIT:u * BAND_UNIT + nk], p)
        l = acc[kvw:kvw + 1]
        if with_sink:
            l = l + jnp.exp2(sink_ref[...] - m)
        inv = 1.0 / l
        kv_row = lambda h: (h * kv_heads // DIL_SLOTS) * HEAD_DIM
        out_t = [lanes(acc[kv_row(h):kv_row(h) + HEAD_DIM], h) * lanes(inv, h) for h in range(DIL_SLOTS)]
        if with_lse:
            lse = m * LN_2 + jnp.log(l)
            out_t += [jnp.broadcast_to(lanes(lse, h), (HEAD_DIM, BAND_UNIT)) for h in range(DIL_SLOTS)]
        outs.append(jnp.concatenate(out_t, axis=0))
    for u in range(nunit):
        rows = slice(u * BAND_UNIT, (u + 1) * BAND_UNIT)
        out = outs[u].T
        o_ref[rows, :] = out[:, :BRANCH_W].astype(BF16)
        if with_lse:
            lse_ref[rows, :] = out[:, BRANCH_W:]


def _banded(z, bias, sink, *, halo, tm, kv_heads, with_lse):
    bsz, dil, sub, ncol = z.shape
    fold = max(1, min(dil, tm // sub))
    assert dil % fold == 0
    z = z.reshape(bsz, dil // fold, fold * sub, ncol)
    nres, rows = z.shape[1], z.shape[2]
    tm = min(tm, rows)
    assert tm % BAND_UNIT == 0 and BAND_UNIT % halo == 0 and (tm % sub == 0 or sub % tm == 0)
    blk = halo
    per_tile = tm // blk
    last_blk = rows // blk - 1
    nk = BAND_UNIT + 2 * halo
    kvw = kv_heads * HEAD_DIM
    first_kv = BRANCH_W // kvw

    def main(j):
        return pl.BlockSpec((None, None, tm, kvw), lambda b, r, i: (b, r, i, first_kv + j))

    def prev(j):
        return pl.BlockSpec((None, None, blk, kvw),
                            lambda b, r, i: (b, r, jnp.maximum(i * per_tile - 1, 0), first_kv + j))

    def nxt(j):
        return pl.BlockSpec((None, None, blk, kvw),
                            lambda b, r, i: (b, r, jnp.minimum((i + 1) * per_tile, last_blk), first_kv + j))

    in_specs = [pl.BlockSpec((None, None, tm, BRANCH_W), lambda b, r, i: (b, r, i, 0)),
                prev(0), main(0), nxt(0), prev(1), main(1), nxt(1),
                _resident((nk, DIL_SLOTS * BAND_UNIT))]
    args = [z, z, z, z, z, z, z, bias]
    if sink is not None:
        in_specs.append(_resident((1, DIL_SLOTS * BAND_UNIT)))
        args.append(sink)
    out_spec = pl.BlockSpec((None, None, tm, BRANCH_W), lambda b, r, i: (b, r, i, 0))
    out_specs = [out_spec]
    out_shape = [jax.ShapeDtypeStruct((bsz, nres, rows, BRANCH_W), BF16)]
    if with_lse:
        out_specs.append(out_spec)
        out_shape.append(jax.ShapeDtypeStruct((bsz, nres, rows, BRANCH_W), F32))
    outs = pl.pallas_call(
        functools.partial(_banded_kernel, halo=halo, tm=tm, seq_len=sub, tile_axis=2, kv_heads=kv_heads,
                          with_sink=sink is not None, with_lse=with_lse),
        grid=(bsz, nres, rows // tm),
        in_specs=in_specs, out_specs=out_specs, out_shape=out_shape,
        compiler_params=_params("parallel", "parallel", "parallel"),
        name=f"banded_d{dil}_h{halo}_kv{kv_heads}",
    )(*args)
    return [o.reshape(bsz, dil, sub, BRANCH_W) for o in outs]


def _na_kernel(q_ref, k_ref, v_ref, bias_ref, o_ref, *, rows, rows_per_step):
    step = pl.program_id(1)
    nkeys = NA_ROWS * GRID_W

    def window(rr):
        r = step * rows_per_step + rr
        sr = jnp.clip(r - NA_ROWS // 2, 0, rows - NA_ROWS)
        return sr - r + NA_ROWS - 1, pl.multiple_of(sr * GRID_W, GRID_W)

    def scores(rr):
        qbd = _head_rows(q_ref[rr * GRID_W:(rr + 1) * GRID_W, :], NA_HEADS)
        return _dot_nt(qbd, k_ref[pl.ds(wins[rr][1], nkeys), :])

    wins = [window(rr) for rr in range(rows_per_step)]
    raw = [scores(rr) for rr in range(rows_per_step)]
    for rr in range(rows_per_step):
        variant, start = wins[rr]
        s = raw[rr] * (HEAD_DIM ** -0.5) + bias_ref[variant]
        p = jnp.exp(s - jnp.max(s, axis=1, keepdims=True))
        obd = _dot(p.astype(BF16), v_ref[pl.ds(start, nkeys), :]) / jnp.sum(p, axis=1, keepdims=True)
        o_ref[rr * GRID_W:(rr + 1) * GRID_W, :] = _head_cols(obd, NA_HEADS).astype(BF16)


def _na(zd, bias, rows_per_step):
    bsz, seq, _ = zd.shape
    rows = seq // GRID_W
    assert rows >= NA_ROWS and rows % rows_per_step == 0
    tq = rows_per_step * GRID_W
    whole = lambda j: pl.BlockSpec((None, seq, BRANCH_W), lambda b, i: (b, 0, j))
    return pl.pallas_call(
        functools.partial(_na_kernel, rows=rows, rows_per_step=rows_per_step),
        grid=(bsz, rows // rows_per_step),
        in_specs=[pl.BlockSpec((None, tq, BRANCH_W), lambda b, i: (b, i, 0)), whole(1), whole(2),
                  _resident((NA_ROWS, NA_HEADS * GRID_W, NA_ROWS * GRID_W))],
        out_specs=pl.BlockSpec((None, tq, BRANCH_W), lambda b, i: (b, i, 0)),
        out_shape=jax.ShapeDtypeStruct((bsz, seq, BRANCH_W), BF16),
        compiler_params=_params("parallel", "arbitrary"),
        name="neighborhood_attention",
    )(zd, zd, zd, bias)


def _join_residues(ref, scr):
    dil, rows, width = ref.shape
    for j in range(width // LANES):
        for r in range(dil):
            scr[j, pl.ds(r, rows, stride=dil), :] = ref[r, :, j * LANES:(j + 1) * LANES].astype(F32)
    return jnp.concatenate([scr[j] for j in range(width // LANES)], axis=1)


def _merge_kernel(x_ref, g_ref, oa_ref, ob0_ref, ob1_ref, ob2_ref, l0_ref, l1_ref, l2_ref,
                  oc_ref, od_ref, wg_ref, wb_ref, wo_ref, out_ref, *scratch):
    x = x_ref[...]
    hb = _rms(x, g_ref[...]).astype(BF16)
    gates = [_dot(hb, wg_ref[j]) for j in range(N_BRANCH)]
    outs = (ob0_ref[0].astype(F32), _join_residues(ob1_ref, scratch[0]), _join_residues(ob2_ref, scratch[1]))
    lses = (l0_ref[0], _join_residues(l1_ref, scratch[2]), _join_residues(l2_ref, scratch[3]))
    top = jnp.maximum(jnp.maximum(lses[0], lses[1]), lses[2])
    ws = [jnp.exp(l - top) for l in lses]
    den = ws[0] + ws[1] + ws[2]
    ob = sum((wgt / den) * o for wgt, o in zip(ws, outs))
    branches = (oa_ref[...], ob.astype(BF16), oc_ref[...], od_ref[...])
    merged = None
    for j, o in enumerate(branches):
        term = jax.nn.sigmoid(gates[j]) * _dot(o, wb_ref[j])
        merged = term if merged is None else merged + term
    out_ref[...] = x + _dot(merged.astype(BF16), wo_ref[...])


def _merge(x, w, oa, obs, lses, oc, od, tm):
    bsz, seq, _ = x.shape
    tok = lambda width: pl.BlockSpec((None, tm, width), lambda b, i: (b, i, 0))
    residues = [pl.BlockSpec((None, dil, tm // dil, BRANCH_W), lambda b, i: (b, 0, i, 0)) for _, dil in DIL_PAIRS]
    return pl.pallas_call(
        _merge_kernel,
        grid=(bsz, seq // tm),
        in_specs=[tok(D_MODEL), _resident((1, D_MODEL)), tok(BRANCH_W)] + residues + residues
                 + [tok(BRANCH_W), tok(BRANCH_W),
                    _resident((N_BRANCH, D_MODEL, D_MODEL)), _resident((N_BRANCH, BRANCH_W, D_MODEL)),
                    _resident((D_MODEL, D_MODEL))],
        out_specs=tok(D_MODEL),
        out_shape=jax.ShapeDtypeStruct(x.shape, F32),
        scratch_shapes=[pltpu.VMEM((BRANCH_W // LANES, tm, LANES), F32)] * 4,
        compiler_params=_params("parallel", "parallel"),
        name="merge",
    )(x, w["ln_attn"], oa, *obs, *lses, oc, od, w["w_gate"], w["w_branch"], w["w_out"])


SUBLANES = 8
GELU_C0 = math.sqrt(2.0 / math.pi)
GELU_C1 = GELU_C0 * 0.044715


def _ffn_kernel(x_ref, xp_ref, xn_ref, g_ref, wu_ref, cw_ref, cb_ref, wd_ref, out_ref, *, tm):
    tile = pl.program_id(1)
    g = g_ref[...]
    x = x_ref[...]
    hn = _rms(x, g).astype(BF16)
    keep_prev = jnp.where(tile > 0, 1.0, 0.0)
    keep_next = jnp.where(tile < pl.num_programs(1) - 1, 1.0, 0.0)
    halo = jnp.concatenate([_rms(xp_ref[...], g) * keep_prev, _rms(xn_ref[...], g) * keep_next], axis=0)
    halo = halo.astype(BF16)
    halves = ((0, D_FF), (D_FF, 2 * D_FF))
    ups = [(_dot(hn, wu_ref[:, c0:c1]), _dot(halo, wu_ref[:, c0:c1])) for c0, c1 in halves]
    row = lax.broadcasted_iota(jnp.int32, (SUBLANES, D_FF), 0)

    def conv(idx):
        (c0, c1), (u, uh) = halves[idx], ups[idx]
        below = pltpu.roll(u, 1, 0)
        below = jnp.concatenate([jnp.where(row == 0, uh[SUBLANES - 1:SUBLANES], below[:SUBLANES]),
                                 below[SUBLANES:]], axis=0)
        above = pltpu.roll(u, tm - 1, 0)
        above = jnp.concatenate([above[:tm - SUBLANES],
                                 jnp.where(row == SUBLANES - 1, uh[SUBLANES:SUBLANES + 1], above[tm - SUBLANES:])], axis=0)
        return below * cw_ref[0:1, c0:c1] + u * cw_ref[1:2, c0:c1] + above * cw_ref[2:3, c0:c1] + cb_ref[:, c0:c1]

    a, half_b = conv(0), conv(1)
    act = a * (1.0 + jnp.tanh(a * (GELU_C0 + GELU_C1 * (a * a)))) * half_b
    out_ref[...] = x + _dot(act.astype(BF16), wd_ref[...])


def _ffn(x, w, tm):
    bsz, seq, _ = x.shape
    per_tile = tm // SUBLANES
    last = seq // SUBLANES - 1
    tok = pl.BlockSpec((None, tm, D_MODEL), lambda b, i: (b, i, 0))
    return pl.pallas_call(
        functools.partial(_ffn_kernel, tm=tm),
        grid=(bsz, seq // tm),
        in_specs=[tok,
                  pl.BlockSpec((None, SUBLANES, D_MODEL), lambda b, i: (b, jnp.maximum(i * per_tile - 1, 0), 0)),
                  pl.BlockSpec((None, SUBLANES, D_MODEL), lambda b, i: (b, jnp.minimum((i + 1) * per_tile, last), 0)),
                  _resident((1, D_MODEL)), _resident((D_MODEL, 2 * D_FF)), _resident((CONV_W, 2 * D_FF)),
                  _resident((1, 2 * D_FF)), _resident((D_FF, D_MODEL))],
        out_specs=tok,
        out_shape=jax.ShapeDtypeStruct(x.shape, F32),
        compiler_params=_params("parallel", "parallel"),
        name="conv_ffn",
    )(x, x, x, w["ln_ffn"], w["w_ffn_up"], w["ffn_conv_w"], w["ffn_conv_b"], w["w_ffn_down"])


def _ple_kernel(x_ref, p_ref, g_ref, wg_ref, wp_ref, gf_ref, out_ref, *, final):
    x = x_ref[...]
    emb = _dot(p_ref[...].astype(BF16), wp_ref[...])
    gate = jax.nn.sigmoid(_dot(_rms(x, g_ref[...]).astype(BF16), wg_ref[...]))
    y = x + gate * emb
    if final:
        y = _rms(y, gf_ref[...])
    out_ref[...] = y


def _ple(x, p, layer, w, ln_final, tm, final):
    bsz, seq, _ = x.shape
    tok = lambda width: pl.BlockSpec((None, tm, width), lambda b, i: (b, i, 0))
    return pl.pallas_call(
        functools.partial(_ple_kernel, final=final),
        grid=(bsz, seq // tm),
        in_specs=[tok(D_MODEL), pl.BlockSpec((None, None, tm, PLE_DIM), lambda b, i: (layer, b, i, 0)),
                  _resident((1, D_MODEL)), _resident((D_MODEL, D_MODEL)),
                  _resident((PLE_DIM, D_MODEL)), _resident((1, D_MODEL))],
        out_specs=tok(D_MODEL),
        out_shape=jax.ShapeDtypeStruct(x.shape, F32),
        compiler_params=_params("parallel", "parallel"),
        name="ple_final" if final else "ple",
    )(x, p, w["ln_ple"], w["w_ple_gate"], w["w_ple_proj"], ln_final)


def _gather_cols(wmat, cols, sign=None):
    cols = np.asarray(cols)
    picked = jnp.take(wmat, jnp.asarray(np.maximum(cols, 0)), axis=-1)
    scale = (cols >= 0).astype(np.float32) * (1.0 if sign is None else np.asarray(sign, np.float32))
    return picked * jnp.asarray(scale)


def _prep_weights(ln_attn, w_in, q_norm, kv_norm, w_q_up, w_kv_up, w_gate, w_branch, w_out, ln_ffn,
                  w_ffn_up, ffn_conv_w, ffn_conv_b, w_ffn_down, ln_ple, w_ple_gate, w_ple_proj):
    half = MLA_ROPE // 2
    qk_dim = MLA_NOPE + MLA_ROPE
    slot_w = MLA_HEADS * MLA_SLOT

    c0 = A_COLS + B_COLS
    group = SWA_Q_HEADS // SWA_KV_HEADS
    q_order = [g + group * kv for g in range(group) for kv in range(SWA_KV_HEADS)]
    q_heads = [w_in[..., c0 + h * HEAD_DIM:c0 + (h + 1) * HEAD_DIM] for h in q_order]
    w_main = jnp.concatenate([w_in[..., A_COLS:c0]] + q_heads + [w_in[..., c0 + SWA_Q_HEADS * HEAD_DIM:]], axis=-1)
    w_a = jnp.pad(w_in[..., :A_COLS], ((0, 0), (0, 0), (0, A_PAD - A_COLS)))

    q_cols = np.full((slot_w,), -1)
    qs_cols = np.full((slot_w,), -1)
    qs_sign = np.ones((slot_w,), np.float32)
    k_cols = np.full((slot_w,), -1)
    e_rows = np.full((slot_w,), -1)
    es_rows = np.full((slot_w,), -1)
    es_sign = np.ones((slot_w,), np.float32)
    for h in range(MLA_HEADS):
        base = h * MLA_SLOT
        q_cols[base:base + qk_dim] = h * qk_dim + np.arange(qk_dim)
        k_cols[base:base + MLA_NOPE] = h * (MLA_NOPE + MLA_V) + np.arange(MLA_NOPE)
        rope0 = base + MLA_NOPE
        qs_cols[rope0:rope0 + half] = h * qk_dim + MLA_NOPE + half + np.arange(half)
        qs_sign[rope0:rope0 + half] = -1.0
        qs_cols[rope0 + half:rope0 + MLA_ROPE] = h * qk_dim + MLA_NOPE + np.arange(half)
        e_rows[rope0:rope0 + MLA_ROPE] = np.arange(MLA_ROPE)
        es_rows[rope0:rope0 + half] = half + np.arange(half)
        es_sign[rope0:rope0 + half] = -1.0
        es_rows[rope0 + half:rope0 + MLA_ROPE] = np.arange(half)
    v_cols = np.full((MLA_HEADS, MLA_VROWS), -1)
    v_cols[:, :MLA_V] = np.arange(MLA_HEADS)[:, None] * (MLA_NOPE + MLA_V) + MLA_NOPE + np.arange(MLA_V)
    v_cols = v_cols.reshape(-1)
    eye = jnp.eye(MLA_SLOT, dtype=F32)

    def place(rows, sign=None):
        m = _gather_cols(eye, rows, sign)
        return m.at[MLA_ROPE:2 * MLA_ROPE].set(m[:MLA_ROPE])

    wk = jnp.concatenate([_gather_cols(w_kv_up, k_cols),
                          jnp.broadcast_to(place(e_rows), (DEPTH, MLA_SLOT, slot_w))], axis=1)

    row = lambda a: a[:, None, :]
    gate_half = jnp.asarray(np.concatenate([np.ones(D_FF, np.float32), np.full(D_FF, 0.5, np.float32)]))
    stacked = {
        "ln_attn": row(ln_attn), "q_norm": row(q_norm), "kv_norm": row(kv_norm),
        "w_main": w_main.astype(BF16),
        "w_a": w_a.astype(BF16),
        "wq": _gather_cols(w_q_up, q_cols).astype(BF16),
        "wq_s": _gather_cols(w_q_up, qs_cols, qs_sign).astype(BF16),
        "wk": wk.astype(BF16),
        "wv_t": jnp.swapaxes(_gather_cols(w_kv_up, v_cols), -1, -2).astype(BF16),
        "w_gate": w_gate.astype(BF16), "w_branch": w_branch.astype(BF16), "w_out": w_out.astype(BF16),
        "ln_ffn": row(ln_ffn), "w_ffn_up": w_ffn_up.astype(BF16), "ffn_conv_w": ffn_conv_w * gate_half,
        "ffn_conv_b": row(ffn_conv_b * gate_half), "w_ffn_down": w_ffn_down.astype(BF16),
        "ln_ple": row(ln_ple), "w_ple_gate": w_ple_gate.astype(BF16), "w_ple_proj": w_ple_proj.astype(BF16),
    }
    shared = {"e_s": place(es_rows, es_sign).astype(BF16)}
    return [dict({k: v[i] for k, v in stacked.items()}, **shared) for i in range(DEPTH)]


def _rope_tables(seq):
    inv = ROPE_THETA ** (-jnp.arange(0, MLA_ROPE, 2, dtype=F32) / MLA_ROPE)
    ang = jnp.arange(seq, dtype=F32)[:, None] * inv[None]
    cos, sin = jnp.cos(ang), jnp.sin(ang)
    pad = MLA_SLOT - MLA_NOPE - MLA_ROPE
    cos_t = jnp.concatenate([jnp.ones((seq, MLA_NOPE), F32), cos, cos, jnp.zeros((seq, pad), F32)], axis=1)
    sin_t = jnp.concatenate([jnp.zeros((seq, MLA_NOPE), F32), sin, sin, jnp.zeros((seq, pad), F32)], axis=1)
    return cos_t, sin_t


def _t5_bucket(rel):
    nb = REL_BUCKETS // 2
    max_exact = nb // 2
    ret = jnp.where(rel > 0, nb, 0)
    n = jnp.abs(rel)
    nf = jnp.maximum(n, 1).astype(F32)
    large = max_exact + (jnp.log(nf / max_exact) / math.log(REL_MAX_DIST / max_exact) * (nb - max_exact)).astype(jnp.int32)
    large = jnp.minimum(large, nb - 1)
    return ret + jnp.where(n < max_exact, n, large)


def _band_bias(rel_table, window, dil, head0):
    nk = BAND_UNIT + 2 * window
    off = np.arange(-(window + BAND_UNIT - 1), window + BAND_UNIT)
    picked = jax.nn.one_hot(_t5_bucket(jnp.asarray(off * dil)), REL_BUCKETS, dtype=F32)
    per_off = jnp.dot(picked, rel_table[:, head0:head0 + DIL_SLOTS].astype(F32), precision=lax.Precision.HIGHEST)
    per_off = jnp.where(jnp.asarray(np.abs(off) <= window)[:, None], per_off * LOG2_E, NEG_INF)
    cols = [per_off[BAND_UNIT - 1 - q:BAND_UNIT - 1 - q + nk] for q in range(BAND_UNIT)]
    return jnp.transpose(jnp.stack(cols, axis=0), (1, 2, 0)).reshape(nk, DIL_SLOTS * BAND_UNIT)


def _na_bias(rpb):
    qc = np.arange(GRID_W)[:, None]
    kc = np.arange(GRID_W)[None, :]
    sc = np.clip(qc - NA_COLS // 2, 0, GRID_W - NA_COLS)
    valid = (kc >= sc) & (kc < sc + NA_COLS)
    edge = GRID_W - NA_COLS
    ext = jnp.concatenate([jnp.repeat(rpb[..., :1], edge, axis=-1), rpb, jnp.repeat(rpb[..., -1:], edge, axis=-1)],
                          axis=-1).astype(F32)
    table = jnp.stack([ext[..., GRID_W - 1 - q:2 * GRID_W - 1 - q] for q in range(GRID_W)], axis=2)
    table = jnp.where(jnp.asarray(valid)[None, None], table, NEG_INF)
    variants = [jnp.transpose(table[:, v:v + NA_ROWS], (0, 2, 1, 3)).reshape(NA_HEADS * GRID_W, NA_ROWS * GRID_W)
                for v in range(NA_ROWS)]
    return jnp.stack(variants)


def _tables(seq, attn_sink, na_rpb, rel_table):
    cos_t, sin_t = _rope_tables(seq)
    return {
        "cos": cos_t, "sin": sin_t,
        "dil_bias": [_band_bias(rel_table, window // (2 * dil), dil, gi * DIL_SLOTS)
                     for gi, (window, dil) in enumerate(DIL_PAIRS)],
        "swa_bias": _band_bias(rel_table, SWA_WINDOW, 1, DIL_HEADS),
        "sink": [jnp.repeat(attn_sink[i].astype(F32) * LOG2_E, BAND_UNIT)[None, :] for i in range(DEPTH)],
        "na_bias": [_na_bias(na_rpb[i]) for i in range(DEPTH)],
    }


TILE_PROJ = 1024
TILE_WIDE = 512
TILE_MLA_Q, TILE_MLA_K = 1024, 2048
TILE_BAND = 2048
NA_ROWS_PER_STEP = 32


def _trunk(x, p, weights, tables, ln_final):
    bsz, seq, _ = x.shape
    tile_proj = min(TILE_PROJ, seq)
    for i in range(DEPTH):
        w = weights[i]
        qa, ka, vt, zb0, zb1, zb2, zc, zd = _attn_in(x, w, tables["cos"], tables["sin"], tile_proj)
        oa = _mla(qa, ka, vt, TILE_MLA_Q, TILE_MLA_K)
        obs, lses = [], []
        for gi, zb in enumerate((zb0[:, None], zb1, zb2)):
            halo = DIL_PAIRS[gi][0] // (2 * DIL_PAIRS[gi][1])
            o, lse = _banded(zb, tables["dil_bias"][gi], None, halo=halo, tm=TILE_BAND, kv_heads=DIL_SLOTS,
                             with_lse=True)
            obs.append(o)
            lses.append(lse)
        (oc,) = _banded(zc[:, None], tables["swa_bias"], tables["sink"][i], halo=SWA_WINDOW, tm=TILE_BAND,
                        kv_heads=SWA_KV_HEADS, with_lse=False)
        oc = oc.reshape(bsz, seq, BRANCH_W)
        od = _na(zd, tables["na_bias"][i], rows_per_step=NA_ROWS_PER_STEP)
        x = _merge(x, w, oa, obs, lses, oc, od, TILE_WIDE)
        x = _ffn(x, w, TILE_WIDE)
        x = _ple(x, p, i, w, ln_final, tile_proj, final=(i == DEPTH - 1))
    return x


def kernel(x_prompt, x_sample, p_prompt, p_sample, ln_attn, w_in, q_norm, kv_norm, w_q_up, w_kv_up,
           attn_sink, na_rpb, rel_table, w_gate, w_branch, w_out, ln_ffn, w_ffn_up, ffn_conv_w,
           ffn_conv_b, w_ffn_down, ln_ple, w_ple_gate, w_ple_proj, ln_final):
    weights = _prep_weights(ln_attn, w_in, q_norm, kv_norm, w_q_up, w_kv_up, w_gate, w_branch, w_out,
                            ln_ffn, w_ffn_up, ffn_conv_w, ffn_conv_b, w_ffn_down, ln_ple, w_ple_gate,
                            w_ple_proj)
    assert x_prompt.shape[1] == x_sample.shape[1]
    tables = _tables(x_prompt.shape[1], attn_sink, na_rpb, rel_table)
    ln_final = ln_final[None, :]
    return (_trunk(x_prompt, p_prompt, weights, tables, ln_final),
            _trunk(x_sample, p_sample, weights, tables, ln_final))
```

```python
import functools
import math

import jax
import jax.numpy as jnp
import numpy as np
from jax import lax
from jax.experimental import pallas as pl
from jax.experimental.pallas import tpu as pltpu

D_MODEL = 1024
DEPTH = 4
PLE_DIM = 256
GRID_W = 64
HEAD_DIM = 64
BRANCH_W = 256
N_BRANCH = 4
EPS = 1e-6
NEG_INF = -1e30

MLA_HEADS = 4
MLA_Q_RANK = 256
MLA_KV_RANK = 128
MLA_NOPE = 64
MLA_ROPE = 32
MLA_V = 64
ROPE_THETA = 10000.0
MLA_SLOT = 128
MLA_VROWS = 96
LANES = 128
LOG2_E = math.log2(math.e)

DIL_PAIRS = ((128, 1), (512, 4), (2048, 16))
DIL_SLOTS = 4
DIL_HEADS = DIL_SLOTS * len(DIL_PAIRS)

SWA_Q_HEADS = 4
SWA_KV_HEADS = 2
SWA_WINDOW = 128

NA_HEADS = 4
NA_ROWS = 8
NA_COLS = 16

REL_BUCKETS = 32
REL_MAX_DIST = 1024

D_FF = 2816
CONV_W = 3

A_COLS = MLA_Q_RANK + MLA_KV_RANK + MLA_ROPE
B_COLS = 3 * DIL_HEADS * HEAD_DIM
C_COLS = (SWA_Q_HEADS + 2 * SWA_KV_HEADS) * HEAD_DIM
D_COLS = 3 * NA_HEADS * HEAD_DIM
A_PAD = 512
MAIN_COLS = B_COLS + C_COLS + D_COLS
GROUP_COLS = 3 * BRANCH_W
MAIN_SPLITS = (0, GROUP_COLS, 2 * GROUP_COLS, B_COLS, B_COLS + C_COLS, MAIN_COLS)

BF16 = jnp.bfloat16
F32 = jnp.float32

VMEM_LIMIT_BYTES = 56 * 1024 * 1024


def _params(*semantics):
    return pltpu.CompilerParams(dimension_semantics=semantics, vmem_limit_bytes=VMEM_LIMIT_BYTES)


def _resident(shape):
    return pl.BlockSpec(shape, lambda *_: (0,) * len(shape), pipeline_mode=pl.Buffered(1))


def _dot(a, b):
    return jnp.dot(a, b, preferred_element_type=F32)


def _dot_nt(a, b):
    return lax.dot_general(a, b, (((1,), (1,)), ((), ())), preferred_element_type=F32)


def _rms(x, g):
    return x * lax.rsqrt(jnp.mean(x * x, axis=-1, keepdims=True) + EPS) * g


def _split_residues(val, scr, out_ref, dil):
    tm = val.shape[0]
    nslab = val.shape[1] // LANES
    for j in range(nslab):
        scr[j] = val[:, j * LANES:(j + 1) * LANES]
    for r in range(dil):
        for j in range(nslab):
            out_ref[r, :, j * LANES:(j + 1) * LANES] = scr[j, pl.ds(r, tm // dil, stride=dil), :].astype(BF16)


def _attn_in_kernel(x_ref, g_ref, wmain_ref, wa_ref, qn_ref, kvn_ref, wq_ref, wqs_ref, wk_ref,
                    wvt_ref, es_ref, cos_ref, sin_ref,
                    qa_ref, ka_ref, vt_ref, zb0_ref, zb1_ref, zb2_ref, zc_ref, zd_ref, scr1, scr2):
    hb = _rms(x_ref[...], g_ref[...]).astype(BF16)
    za = _dot(hb, wa_ref[...])
    chunk = lambda c: _dot(hb, wmain_ref[:, MAIN_SPLITS[c]:MAIN_SPLITS[c + 1]])
    zb0_ref[...] = chunk(0).astype(BF16)
    _split_residues(chunk(1), scr1, zb1_ref, DIL_PAIRS[1][1])
    _split_residues(chunk(2), scr2, zb2_ref, DIL_PAIRS[2][1])
    zc_ref[...] = chunk(3).astype(BF16)
    zd_ref[...] = chunk(4).astype(BF16)

    cq = _rms(za[:, :MLA_Q_RANK], qn_ref[...]).astype(BF16)
    ckv = _rms(za[:, MLA_Q_RANK:MLA_Q_RANK + MLA_KV_RANK], kvn_ref[...]).astype(BF16)
    kr = za[:, MLA_Q_RANK + MLA_KV_RANK:]
    kr_hi = kr.astype(BF16).astype(F32)
    kr2 = (kr_hi + pltpu.roll(kr - kr_hi, MLA_ROPE, 1)).astype(BF16)
    cos = jnp.concatenate([cos_ref[...]] * MLA_HEADS, axis=1)
    sin = jnp.concatenate([sin_ref[...]] * MLA_HEADS, axis=1)
    q = _dot(cq, wq_ref[...]) * cos + _dot(cq, wqs_ref[...]) * sin
    qa_ref[...] = (q * (LOG2_E * (MLA_NOPE + MLA_ROPE) ** -0.5)).astype(BF16)
    k_plain = _dot(jnp.concatenate([ckv, kr2], axis=1), wk_ref[...])
    ka_ref[...] = (k_plain * cos + _dot(kr2, es_ref[...]) * sin).astype(BF16)
    vt = _dot_nt(wvt_ref[...], ckv)
    row = lax.broadcasted_iota(jnp.int32, vt.shape, 0)
    vt_ref[...] = jnp.where(row % MLA_VROWS == MLA_V, 1.0, vt).astype(BF16)


def _attn_in(x, w, cos_t, sin_t, tm):
    bsz, seq, _ = x.shape
    tok = lambda width: pl.BlockSpec((None, tm, width), lambda b, i: (b, i, 0))
    table = pl.BlockSpec((tm, MLA_SLOT), lambda b, i: (i, 0))
    group = GROUP_COLS
    slot_w = MLA_HEADS * MLA_SLOT
    out_specs = [tok(slot_w), tok(slot_w),
                 pl.BlockSpec((None, MLA_HEADS * MLA_VROWS, tm), lambda b, i: (b, 0, i)), tok(group)]
    out_shape = [jax.ShapeDtypeStruct((bsz, seq, slot_w), BF16), jax.ShapeDtypeStruct((bsz, seq, slot_w), BF16),
                 jax.ShapeDtypeStruct((bsz, MLA_HEADS * MLA_VROWS, seq), BF16),
                 jax.ShapeDtypeStruct((bsz, seq, group), BF16)]
    scratch = []
    for _, dil in DIL_PAIRS[1:]:
        out_specs.append(pl.BlockSpec((None, dil, tm // dil, group), lambda b, i: (b, 0, i, 0)))
        out_shape.append(jax.ShapeDtypeStruct((bsz, dil, seq // dil, group), BF16))
        scratch.append(pltpu.VMEM((group // LANES, tm, LANES), F32))
    out_specs += [tok(C_COLS), tok(D_COLS)]
    out_shape += [jax.ShapeDtypeStruct((bsz, seq, C_COLS), BF16), jax.ShapeDtypeStruct((bsz, seq, D_COLS), BF16)]
    return pl.pallas_call(
        _attn_in_kernel,
        grid=(bsz, seq // tm),
        in_specs=[tok(D_MODEL), _resident((1, D_MODEL)), _resident((D_MODEL, MAIN_COLS)),
                  _resident((D_MODEL, A_PAD)), _resident((1, MLA_Q_RANK)), _resident((1, MLA_KV_RANK)),
                  _resident((MLA_Q_RANK, slot_w)), _resident((MLA_Q_RANK, slot_w)),
                  _resident((MLA_KV_RANK + MLA_SLOT, slot_w)), _resident((MLA_HEADS * MLA_VROWS, MLA_KV_RANK)),
                  _resident((MLA_SLOT, slot_w)), table, table],
        out_specs=out_specs, out_shape=out_shape, scratch_shapes=scratch,
        compiler_params=_params("parallel", "parallel"),
        name="attn_in",
    )(x, w["ln_attn"], w["w_main"], w["w_a"], w["q_norm"], w["kv_norm"], w["wq"], w["wq_s"],
      w["wk"], w["wv_t"], w["e_s"], cos_t, sin_t)


def _mla_kernel(q_ref, k_ref, vt_ref, o_ref, m_scr, acc_scr):
    kv = pl.program_id(2)

    @pl.when(kv == 0)
    def _():
        m_scr[...] = jnp.full(m_scr.shape, NEG_INF, F32)
        acc_scr[...] = jnp.zeros(acc_scr.shape, F32)

    def scores(h):
        return _dot_nt(k_ref[:, h * MLA_SLOT:(h + 1) * MLA_SLOT], q_ref[:, h * MLA_SLOT:(h + 1) * MLA_SLOT])

    s_next = scores(0)
    for h in range(MLA_HEADS):
        s = s_next
        if h + 1 < MLA_HEADS:
            s_next = scores(h + 1)
        m_prev = m_scr[h]
        m_new = jnp.maximum(m_prev, jnp.max(s, axis=0, keepdims=True))
        p = jnp.exp2(s - m_new).astype(BF16)
        acc_scr[h] = jnp.exp2(m_prev - m_new) * acc_scr[h] + _dot(vt_ref[h * MLA_VROWS:(h + 1) * MLA_VROWS, :], p)
        m_scr[h] = m_new

    @pl.when(kv == pl.num_programs(2) - 1)
    def _():
        o_t = jnp.concatenate([acc_scr[h, :MLA_V, :] / acc_scr[h, MLA_V:MLA_V + 1, :] for h in range(MLA_HEADS)],
                              axis=0)
        o_ref[...] = o_t.T.astype(BF16)


def _mla(qa, ka, vt, tq, tk):
    bsz, seq, _ = qa.shape
    tq, tk = min(tq, seq), min(tk, seq)
    return pl.pallas_call(
        _mla_kernel,
        grid=(bsz, seq // tq, seq // tk),
        in_specs=[pl.BlockSpec((None, tq, MLA_HEADS * MLA_SLOT), lambda b, i, j: (b, i, 0)),
                  pl.BlockSpec((None, tk, MLA_HEADS * MLA_SLOT), lambda b, i, j: (b, j, 0)),
                  pl.BlockSpec((None, MLA_HEADS * MLA_VROWS, tk), lambda b, i, j: (b, 0, j))],
        out_specs=pl.BlockSpec((None, tq, MLA_HEADS * MLA_V), lambda b, i, j: (b, i, 0)),
        out_shape=jax.ShapeDtypeStruct((bsz, seq, MLA_HEADS * MLA_V), BF16),
        scratch_shapes=[pltpu.VMEM((MLA_HEADS, 1, tq), F32), pltpu.VMEM((MLA_HEADS, MLA_VROWS, tq), F32)],
        compiler_params=_params("parallel", "parallel", "arbitrary"),
        name="mla_attention",
    )(qa, ka, vt)


def _head_rows(x, nheads):
    head = lax.broadcasted_iota(jnp.int32, x.shape, 1) // HEAD_DIM
    xf = x.astype(F32)
    return jnp.concatenate([jnp.where(head == h, xf, 0.0) for h in range(nheads)], axis=0).astype(x.dtype)


def _head_cols(x, nheads):
    n = x.shape[0] // nheads
    head = lax.broadcasted_iota(jnp.int32, (n, x.shape[1]), 1) // HEAD_DIM
    out = x[:n]
    for h in range(1, nheads):
        out = jnp.where(head == h, x[h * n:(h + 1) * n], out)
    return out


BAND_UNIT = 128
ONES_ROWS = 16
LN_2 = math.log(2.0)


def _query_rows(q, kv_heads):
    if kv_heads == DIL_SLOTS:
        return _head_rows(q, DIL_SLOTS)
    kvw = kv_heads * HEAD_DIM
    slot = lax.broadcasted_iota(jnp.int32, (q.shape[0], kvw), 1) // HEAD_DIM
    tiles = [q[:, t * kvw:(t + 1) * kvw].astype(F32) for t in range(DIL_SLOTS // kv_heads)]
    rows = [jnp.where(slot == h // (DIL_SLOTS // kv_heads), tiles[h % (DIL_SLOTS // kv_heads)], 0.0)
            for h in range(DIL_SLOTS)]
    return jnp.concatenate(rows, axis=0).astype(q.dtype)


def _banded_kernel(*refs, halo, tm, seq_len, tile_axis, kv_heads, with_sink, with_lse):
    q_ref, kp_ref, km_ref, kn_ref, vp_ref, vm_ref, vn_ref, bias_ref = refs[:8]
    refs = refs[8:]
    sink_ref = None
    if with_sink:
        sink_ref, refs = refs[0], refs[1:]
    o_ref = refs[0]
    lse_ref = refs[1] if with_lse else None

    tile = pl.program_id(tile_axis)
    kvw = kv_heads * HEAD_DIM
    kcat = jnp.concatenate([kp_ref[...], km_ref[...], kn_ref[...]], axis=0)
    vcat = jnp.concatenate([vp_ref[...], vm_ref[...], vn_ref[...]], axis=0)
    vt = vcat.astype(F32).T
    ones = jnp.where(lax.broadcasted_iota(jnp.int32, (ONES_ROWS, vt.shape[1]), 0) == 0, 1.0, 0.0)
    vt = jnp.concatenate([vt, ones], axis=0).astype(BF16)
    nk = BAND_UNIT + 2 * halo
    nunit = tm // BAND_UNIT
    lanes = lambda x, h: x[:, h * BAND_UNIT:(h + 1) * BAND_UNIT]

    def scores(u):
        qbd = _query_rows(q_ref[u * BAND_UNIT:(u + 1) * BAND_UNIT, :], kv_heads)
        return _dot_nt(kcat[u * BAND_UNIT:u * BAND_UNIT + nk], qbd)

    raw = [scores(u) for u in range(nunit)]
    outs = []
    for u in range(nunit):
        s = raw[u] * (LOG2_E * HEAD_DIM ** -0.5) + bias_ref[...]
        if tm >= seq_len:
            start = (u * BAND_UNIT) % seq_len
            first, last = start == 0, start == seq_len - BAND_UNIT
        else:
            start = (tile % (seq_len // tm)) * tm + u * BAND_UNIT
            first, last = u == 0, u == nunit - 1
        if first or last:
            kpos = start - halo + lax.broadcasted_iota(jnp.int32, s.shape, 0)
            if first:
                s = jnp.where(kpos >= 0, s, NEG_INF)
            if last:
                s = jnp.where(kpos < seq_len, s, NEG_INF)
        m = jnp.max(s, axis=0, keepdims=True)
        if with_sink:
            m = jnp.maximum(m, sink_ref[...])
        p = jnp.exp2(s - m).astype(BF16)
        acc = _dot(vt[:, u * BAND_UNIT:u * BAND_UNIT + nk], p)
        l = acc[kvw:kvw + 1]
        if with_sink:
            l = l + jnp.exp2(sink_ref[...] - m)
        inv = 1.0 / l
        kv_row = lambda h: (h * kv_heads // DIL_SLOTS) * HEAD_DIM
        out_t = [lanes(acc[kv_row(h):kv_row(h) + HEAD_DIM], h) * lanes(inv, h) for h in range(DIL_SLOTS)]
        if with_lse:
            lse = m * LN_2 + jnp.log(l)
            out_t += [jnp.broadcast_to(lanes(lse, h), (HEAD_DIM, BAND_UNIT)) for h in range(DIL_SLOTS)]
        outs.append(jnp.concatenate(out_t, axis=0))
    for u in range(nunit):
        rows = slice(u * BAND_UNIT, (u + 1) * BAND_UNIT)
        out = outs[u].T
        o_ref[rows, :] = out[:, :BRANCH_W].astype(BF16)
        if with_lse:
            lse_ref[rows, :] = out[:, BRANCH_W:]


def _banded(z, bias, sink, *, halo, tm, kv_heads, with_lse):
    bsz, dil, sub, ncol = z.shape
    fold = max(1, min(dil, tm // sub))
    assert dil % fold == 0
    z = z.reshape(bsz, dil // fold, fold * sub, ncol)
    nres, rows = z.shape[1], z.shape[2]
    tm = min(tm, rows)
    assert tm % BAND_UNIT == 0 and BAND_UNIT % halo == 0 and (tm % sub == 0 or sub % tm == 0)
    blk = halo
    per_tile = tm // blk
    last_blk = rows // blk - 1
    nk = BAND_UNIT + 2 * halo
    kvw = kv_heads * HEAD_DIM
    first_kv = BRANCH_W // kvw

    def main(j):
        return pl.BlockSpec((None, None, tm, kvw), lambda b, r, i: (b, r, i, first_kv + j))

    def prev(j):
        return pl.BlockSpec((None, None, blk, kvw),
                            lambda b, r, i: (b, r, jnp.maximum(i * per_tile - 1, 0), first_kv + j))

    def nxt(j):
        return pl.BlockSpec((None, None, blk, kvw),
                            lambda b, r, i: (b, r, jnp.minimum((i + 1) * per_tile, last_blk), first_kv + j))

    in_specs = [pl.BlockSpec((None, None, tm, BRANCH_W), lambda b, r, i: (b, r, i, 0)),
                prev(0), main(0), nxt(0), prev(1), main(1), nxt(1),
                _resident((nk, DIL_SLOTS * BAND_UNIT))]
    args = [z, z, z, z, z, z, z, bias]
    if sink is not None:
        in_specs.append(_resident((1, DIL_SLOTS * BAND_UNIT)))
        args.append(sink)
    out_spec = pl.BlockSpec((None, None, tm, BRANCH_W), lambda b, r, i: (b, r, i, 0))
    out_specs = [out_spec]
    out_shape = [jax.ShapeDtypeStruct((bsz, nres, rows, BRANCH_W), BF16)]
    if with_lse:
        out_specs.append(out_spec)
        out_shape.append(jax.ShapeDtypeStruct((bsz, nres, rows, BRANCH_W), F32))
    outs = pl.pallas_call(
        functools.partial(_banded_kernel, halo=halo, tm=tm, seq_len=sub, tile_axis=2, kv_heads=kv_heads,
                          with_sink=sink is not None, with_lse=with_lse),
        grid=(bsz, nres, rows // tm),
        in_specs=in_specs, out_specs=out_specs, out_shape=out_shape,
        compiler_params=_params("parallel", "parallel", "parallel"),
        name=f"banded_d{dil}_h{halo}_kv{kv_heads}",
    )(*args)
    return [o.reshape(bsz, dil, sub, BRANCH_W) for o in outs]


def _na_kernel(q_ref, k_ref, v_ref, bias_ref, o_ref, *, rows, rows_per_step):
    step = pl.program_id(1)
    nkeys = NA_ROWS * GRID_W

    def window(rr):
        r = step * rows_per_step + rr
        sr = jnp.clip(r - NA_ROWS // 2, 0, rows - NA_ROWS)
        return sr - r + NA_ROWS - 1, pl.multiple_of(sr * GRID_W, GRID_W)

    def scores(rr):
        qbd = _head_rows(q_ref[rr * GRID_W:(rr + 1) * GRID_W, :], NA_HEADS)
        return _dot_nt(qbd, k_ref[pl.ds(wins[rr][1], nkeys), :])

    wins = [window(rr) for rr in range(rows_per_step)]
    raw = [scores(rr) for rr in range(rows_per_step)]
    for rr in range(rows_per_step):
        variant, start = wins[rr]
        s = raw[rr] * (HEAD_DIM ** -0.5) + bias_ref[variant]
        p = jnp.exp(s - jnp.max(s, axis=1, keepdims=True))
        obd = _dot(p.astype(BF16), v_ref[pl.ds(start, nkeys), :]) / jnp.sum(p, axis=1, keepdims=True)
        o_ref[rr * GRID_W:(rr + 1) * GRID_W, :] = _head_cols(obd, NA_HEADS).astype(BF16)


def _na(zd, bias, rows_per_step):
    bsz, seq, _ = zd.shape
    rows = seq // GRID_W
    assert rows >= NA_ROWS and rows % rows_per_step == 0
    tq = rows_per_step * GRID_W
    whole = lambda j: pl.BlockSpec((None, seq, BRANCH_W), lambda b, i: (b, 0, j))
    return pl.pallas_call(
        functools.partial(_na_kernel, rows=rows, rows_per_step=rows_per_step),
        grid=(bsz, rows // rows_per_step),
        in_specs=[pl.BlockSpec((None, tq, BRANCH_W), lambda b, i: (b, i, 0)), whole(1), whole(2),
                  _resident((NA_ROWS, NA_HEADS * GRID_W, NA_ROWS * GRID_W))],
        out_specs=pl.BlockSpec((None, tq, BRANCH_W), lambda b, i: (b, i, 0)),
        out_shape=jax.ShapeDtypeStruct((bsz, seq, BRANCH_W), BF16),
        compiler_params=_params("parallel", "arbitrary"),
        name="neighborhood_attention",
    )(zd, zd, zd, bias)


def _join_residues(ref, scr):
    dil, rows, width = ref.shape
    for j in range(width // LANES):
        for r in range(dil):
            scr[j, pl.ds(r, rows, stride=dil), :] = ref[r, :, j * LANES:(j + 1) * LANES].astype(F32)
    return jnp.concatenate([scr[j] for j in range(width // LANES)], axis=1)


def _merge_kernel(x_ref, g_ref, oa_ref, ob0_ref, ob1_ref, ob2_ref, l0_ref, l1_ref, l2_ref,
                  oc_ref, od_ref, wg_ref, wb_ref, wo_ref, out_ref, *scratch):
    x = x_ref[...]
    hb = _rms(x, g_ref[...]).astype(BF16)
    gates = [_dot(hb, wg_ref[j]) for j in range(N_BRANCH)]
    outs = (ob0_ref[0].astype(F32), _join_residues(ob1_ref, scratch[0]), _join_residues(ob2_ref, scratch[1]))
    lses = (l0_ref[0], _join_residues(l1_ref, scratch[2]), _join_residues(l2_ref, scratch[3]))
    top = jnp.maximum(jnp.maximum(lses[0], lses[1]), lses[2])
    ws = [jnp.exp(l - top) for l in lses]
    den = ws[0] + ws[1] + ws[2]
    ob = sum((wgt / den) * o for wgt, o in zip(ws, outs))
    branches = (oa_ref[...], ob.astype(BF16), oc_ref[...], od_ref[...])
    merged = None
    for j, o in enumerate(branches):
        term = jax.nn.sigmoid(gates[j]) * _dot(o, wb_ref[j])
        merged = term if merged is None else merged + term
    out_ref[...] = x + _dot(merged.astype(BF16), wo_ref[...])


def _merge(x, w, oa, obs, lses, oc, od, tm):
    bsz, seq, _ = x.shape
    tok = lambda width: pl.BlockSpec((None, tm, width), lambda b, i: (b, i, 0))
    residues = [pl.BlockSpec((None, dil, tm // dil, BRANCH_W), lambda b, i: (b, 0, i, 0)) for _, dil in DIL_PAIRS]
    return pl.pallas_call(
        _merge_kernel,
        grid=(bsz, seq // tm),
        in_specs=[tok(D_MODEL), _resident((1, D_MODEL)), tok(BRANCH_W)] + residues + residues
                 + [tok(BRANCH_W), tok(BRANCH_W),
                    _resident((N_BRANCH, D_MODEL, D_MODEL)), _resident((N_BRANCH, BRANCH_W, D_MODEL)),
                    _resident((D_MODEL, D_MODEL))],
        out_specs=tok(D_MODEL),
        out_shape=jax.ShapeDtypeStruct(x.shape, F32),
        scratch_shapes=[pltpu.VMEM((BRANCH_W // LANES, tm, LANES), F32)] * 4,
        compiler_params=_params("parallel", "parallel"),
        name="merge",
    )(x, w["ln_attn"], oa, *obs, *lses, oc, od, w["w_gate"], w["w_branch"], w["w_out"])


SUBLANES = 8
GELU_C0 = math.sqrt(2.0 / math.pi)
GELU_C1 = GELU_C0 * 0.044715


def _ffn_kernel(x_ref, xp_ref, xn_ref, g_ref, wu_ref, cw_ref, cb_ref, wd_ref, out_ref, *, tm):
    tile = pl.program_id(1)
    g = g_ref[...]
    x = x_ref[...]
    hn = _rms(x, g).astype(BF16)
    keep_prev = jnp.where(tile > 0, 1.0, 0.0)
    keep_next = jnp.where(tile < pl.num_programs(1) - 1, 1.0, 0.0)
    halo = jnp.concatenate([_rms(xp_ref[...], g) * keep_prev, _rms(xn_ref[...], g) * keep_next], axis=0)
    lhs = jnp.concatenate([hn, halo.astype(BF16)], axis=0)
    halves = ((0, D_FF), (D_FF, 2 * D_FF))
    ups = [_dot(lhs, wu_ref[:, c0:c1]) for c0, c1 in halves]
    row = lax.broadcasted_iota(jnp.int32, (SUBLANES, D_FF), 0)

    def conv(idx):
        (c0, c1), u, uh = halves[idx], ups[idx][:tm], ups[idx][tm:]
        below = pltpu.roll(u, 1, 0)
        below = jnp.concatenate([jnp.where(row == 0, uh[SUBLANES - 1:SUBLANES], below[:SUBLANES]),
                                 below[SUBLANES:]], axis=0)
        above = pltpu.roll(u, tm - 1, 0)
        above = jnp.concatenate([above[:tm - SUBLANES],
                                 jnp.where(row == SUBLANES - 1, uh[SUBLANES:SUBLANES + 1], above[tm - SUBLANES:])], axis=0)
        return below * cw_ref[0:1, c0:c1] + u * cw_ref[1:2, c0:c1] + above * cw_ref[2:3, c0:c1] + cb_ref[:, c0:c1]

    a, half_b = conv(0), conv(1)
    act = a * (1.0 + jnp.tanh(a * (GELU_C0 + GELU_C1 * (a * a)))) * half_b
    out_ref[...] = x + _dot(act.astype(BF16), wd_ref[...])


def _ffn(x, w, tm):
    bsz, seq, _ = x.shape
    per_tile = tm // SUBLANES
    last = seq // SUBLANES - 1
    tok = pl.BlockSpec((None, tm, D_MODEL), lambda b, i: (b, i, 0))
    return pl.pallas_call(
        functools.partial(_ffn_kernel, tm=tm),
        grid=(bsz, seq // tm),
        in_specs=[tok,
                  pl.BlockSpec((None, SUBLANES, D_MODEL), lambda b, i: (b, jnp.maximum(i * per_tile - 1, 0), 0)),
                  pl.BlockSpec((None, SUBLANES, D_MODEL), lambda b, i: (b, jnp.minimum((i + 1) * per_tile, last), 0)),
                  _resident((1, D_MODEL)), _resident((D_MODEL, 2 * D_FF)), _resident((CONV_W, 2 * D_FF)),
                  _resident((1, 2 * D_FF)), _resident((D_FF, D_MODEL))],
        out_specs=tok,
        out_shape=jax.ShapeDtypeStruct(x.shape, F32),
        compiler_params=_params("parallel", "parallel"),
        name="conv_ffn",
    )(x, x, x, w["ln_ffn"], w["w_ffn_up"], w["ffn_conv_w"], w["ffn_conv_b"], w["w_ffn_down"])


def _ple_kernel(x_ref, p_ref, g_ref, wg_ref, wp_ref, gf_ref, out_ref, *, final):
    x = x_ref[...]
    emb = _dot(p_ref[...].astype(BF16), wp_ref[...])
    gate = jax.nn.sigmoid(_dot(_rms(x, g_ref[...]).astype(BF16), wg_ref[...]))
    y = x + gate * emb
    if final:
        y = _rms(y, gf_ref[...])
    out_ref[...] = y


def _ple(x, p, layer, w, ln_final, tm, final):
    bsz, seq, _ = x.shape
    tok = lambda width: pl.BlockSpec((None, tm, width), lambda b, i: (b, i, 0))
    return pl.pallas_call(
        functools.partial(_ple_kernel, final=final),
        grid=(bsz, seq // tm),
        in_specs=[tok(D_MODEL), pl.BlockSpec((None, None, tm, PLE_DIM), lambda b, i: (layer, b, i, 0)),
                  _resident((1, D_MODEL)), _resident((D_MODEL, D_MODEL)),
                  _resident((PLE_DIM, D_MODEL)), _resident((1, D_MODEL))],
        out_specs=tok(D_MODEL),
        out_shape=jax.ShapeDtypeStruct(x.shape, F32),
        compiler_params=_params("parallel", "parallel"),
        name="ple_final" if final else "ple",
    )(x, p, w["ln_ple"], w["w_ple_gate"], w["w_ple_proj"], ln_final)


def _gather_cols(wmat, cols, sign=None):
    cols = np.asarray(cols)
    picked = jnp.take(wmat, jnp.asarray(np.maximum(cols, 0)), axis=-1)
    scale = (cols >= 0).astype(np.float32) * (1.0 if sign is None else np.asarray(sign, np.float32))
    return picked * jnp.asarray(scale)


def _prep_weights(ln_attn, w_in, q_norm, kv_norm, w_q_up, w_kv_up, w_gate, w_branch, w_out, ln_ffn,
                  w_ffn_up, ffn_conv_w, ffn_conv_b, w_ffn_down, ln_ple, w_ple_gate, w_ple_proj):
    half = MLA_ROPE // 2
    qk_dim = MLA_NOPE + MLA_ROPE
    slot_w = MLA_HEADS * MLA_SLOT

    c0 = A_COLS + B_COLS
    group = SWA_Q_HEADS // SWA_KV_HEADS
    q_order = [g + group * kv for g in range(group) for kv in range(SWA_KV_HEADS)]
    q_heads = [w_in[..., c0 + h * HEAD_DIM:c0 + (h + 1) * HEAD_DIM] for h in q_order]
    w_main = jnp.concatenate([w_in[..., A_COLS:c0]] + q_heads + [w_in[..., c0 + SWA_Q_HEADS * HEAD_DIM:]], axis=-1)
    w_a = jnp.pad(w_in[..., :A_COLS], ((0, 0), (0, 0), (0, A_PAD - A_COLS)))

    q_cols = np.full((slot_w,), -1)
    qs_cols = np.full((slot_w,), -1)
    qs_sign = np.ones((slot_w,), np.float32)
    k_cols = np.full((slot_w,), -1)
    e_rows = np.full((slot_w,), -1)
    es_rows = np.full((slot_w,), -1)
    es_sign = np.ones((slot_w,), np.float32)
    for h in range(MLA_HEADS):
        base = h * MLA_SLOT
        q_cols[base:base + qk_dim] = h * qk_dim + np.arange(qk_dim)
        k_cols[base:base + MLA_NOPE] = h * (MLA_NOPE + MLA_V) + np.arange(MLA_NOPE)
        rope0 = base + MLA_NOPE
        qs_cols[rope0:rope0 + half] = h * qk_dim + MLA_NOPE + half + np.arange(half)
        qs_sign[rope0:rope0 + half] = -1.0
        qs_cols[rope0 + half:rope0 + MLA_ROPE] = h * qk_dim + MLA_NOPE + np.arange(half)
        e_rows[rope0:rope0 + MLA_ROPE] = np.arange(MLA_ROPE)
        es_rows[rope0:rope0 + half] = half + np.arange(half)
        es_sign[rope0:rope0 + half] = -1.0
        es_rows[rope0 + half:rope0 + MLA_ROPE] = np.arange(half)
    v_cols = np.full((MLA_HEADS, MLA_VROWS), -1)
    v_cols[:, :MLA_V] = np.arange(MLA_HEADS)[:, None] * (MLA_NOPE + MLA_V) + MLA_NOPE + np.arange(MLA_V)
    v_cols = v_cols.reshape(-1)
    eye = jnp.eye(MLA_SLOT, dtype=F32)

    def place(rows, sign=None):
        m = _gather_cols(eye, rows, sign)
        return m.at[MLA_ROPE:2 * MLA_ROPE].set(m[:MLA_ROPE])

    wk = jnp.concatenate([_gather_cols(w_kv_up, k_cols),
                          jnp.broadcast_to(place(e_rows), (DEPTH, MLA_SLOT, slot_w))], axis=1)

    row = lambda a: a[:, None, :]
    gate_half = jnp.asarray(np.concatenate([np.ones(D_FF, np.float32), np.full(D_FF, 0.5, np.float32)]))
    stacked = {
        "ln_attn": row(ln_attn), "q_norm": row(q_norm), "kv_norm": row(kv_norm),
        "w_main": w_main.astype(BF16),
        "w_a": w_a.astype(BF16),
        "wq": _gather_cols(w_q_up, q_cols).astype(BF16),
        "wq_s": _gather_cols(w_q_up, qs_cols, qs_sign).astype(BF16),
        "wk": wk.astype(BF16),
        "wv_t": jnp.swapaxes(_gather_cols(w_kv_up, v_cols), -1, -2).astype(BF16),
        "w_gate": w_gate.astype(BF16), "w_branch": w_branch.astype(BF16), "w_out": w_out.astype(BF16),
        "ln_ffn": row(ln_ffn), "w_ffn_up": w_ffn_up.astype(BF16), "ffn_conv_w": ffn_conv_w * gate_half,
        "ffn_conv_b": row(ffn_conv_b * gate_half), "w_ffn_down": w_ffn_down.astype(BF16),
        "ln_ple": row(ln_ple), "w_ple_gate": w_ple_gate.astype(BF16), "w_ple_proj": w_ple_proj.astype(BF16),
    }
    shared = {"e_s": place(es_rows, es_sign).astype(BF16)}
    return [dict({k: v[i] for k, v in stacked.items()}, **shared) for i in range(DEPTH)]


def _rope_tables(seq):
    inv = ROPE_THETA ** (-jnp.arange(0, MLA_ROPE, 2, dtype=F32) / MLA_ROPE)
    ang = jnp.arange(seq, dtype=F32)[:, None] * inv[None]
    cos, sin = jnp.cos(ang), jnp.sin(ang)
    pad = MLA_SLOT - MLA_NOPE - MLA_ROPE
    cos_t = jnp.concatenate([jnp.ones((seq, MLA_NOPE), F32), cos, cos, jnp.zeros((seq, pad), F32)], axis=1)
    sin_t = jnp.concatenate([jnp.zeros((seq, MLA_NOPE), F32), sin, sin, jnp.zeros((seq, pad), F32)], axis=1)
    return cos_t, sin_t


def _t5_bucket(rel):
    nb = REL_BUCKETS // 2
    max_exact = nb // 2
    ret = jnp.where(rel > 0, nb, 0)
    n = jnp.abs(rel)
    nf = jnp.maximum(n, 1).astype(F32)
    large = max_exact + (jnp.log(nf / max_exact) / math.log(REL_MAX_DIST / max_exact) * (nb - max_exact)).astype(jnp.int32)
    large = jnp.minimum(large, nb - 1)
    return ret + jnp.where(n < max_exact, n, large)


def _band_bias(rel_table, window, dil, head0):
    nk = BAND_UNIT + 2 * window
    off = np.arange(-(window + BAND_UNIT - 1), window + BAND_UNIT)
    picked = jax.nn.one_hot(_t5_bucket(jnp.asarray(off * dil)), REL_BUCKETS, dtype=F32)
    per_off = jnp.dot(picked, rel_table[:, head0:head0 + DIL_SLOTS].astype(F32), precision=lax.Precision.HIGHEST)
    per_off = jnp.where(jnp.asarray(np.abs(off) <= window)[:, None], per_off * LOG2_E, NEG_INF)
    cols = [per_off[BAND_UNIT - 1 - q:BAND_UNIT - 1 - q + nk] for q in range(BAND_UNIT)]
    return jnp.transpose(jnp.stack(cols, axis=0), (1, 2, 0)).reshape(nk, DIL_SLOTS * BAND_UNIT)


def _na_bias(rpb):
    qc = np.arange(GRID_W)[:, None]
    kc = np.arange(GRID_W)[None, :]
    sc = np.clip(qc - NA_COLS // 2, 0, GRID_W - NA_COLS)
    valid = (kc >= sc) & (kc < sc + NA_COLS)
    edge = GRID_W - NA_COLS
    ext = jnp.concatenate([jnp.repeat(rpb[..., :1], edge, axis=-1), rpb, jnp.repeat(rpb[..., -1:], edge, axis=-1)],
                          axis=-1).astype(F32)
    table = jnp.stack([ext[..., GRID_W - 1 - q:2 * GRID_W - 1 - q] for q in range(GRID_W)], axis=2)
    table = jnp.where(jnp.asarray(valid)[None, None], table, NEG_INF)
    variants = [jnp.transpose(table[:, v:v + NA_ROWS], (0, 2, 1, 3)).reshape(NA_HEADS * GRID_W, NA_ROWS * GRID_W)
                for v in range(NA_ROWS)]
    return jnp.stack(variants)


def _tables(seq, attn_sink, na_rpb, rel_table):
    cos_t, sin_t = _rope_tables(seq)
    return {
        "cos": cos_t, "sin": sin_t,
        "dil_bias": [_band_bias(rel_table, window // (2 * dil), dil, gi * DIL_SLOTS)
                     for gi, (window, dil) in enumerate(DIL_PAIRS)],
        "swa_bias": _band_bias(rel_table, SWA_WINDOW, 1, DIL_HEADS),
        "sink": [jnp.repeat(attn_sink[i].astype(F32) * LOG2_E, BAND_UNIT)[None, :] for i in range(DEPTH)],
        "na_bias": [_na_bias(na_rpb[i]) for i in range(DEPTH)],
    }


TILE_PROJ = 1024
TILE_WIDE = 512
TILE_MLA_Q, TILE_MLA_K = 1024, 2048
TILE_BAND = 2048
NA_ROWS_PER_STEP = 32


def _trunk(x, p, weights, tables, ln_final):
    bsz, seq, _ = x.shape
    tile_proj = min(TILE_PROJ, seq)
    for i in range(DEPTH):
        w = weights[i]
        qa, ka, vt, zb0, zb1, zb2, zc, zd = _attn_in(x, w, tables["cos"], tables["sin"], tile_proj)
        oa = _mla(qa, ka, vt, TILE_MLA_Q, TILE_MLA_K)
        obs, lses = [], []
        for gi, zb in enumerate((zb0[:, None], zb1, zb2)):
            halo = DIL_PAIRS[gi][0] // (2 * DIL_PAIRS[gi][1])
            o, lse = _banded(zb, tables["dil_bias"][gi], None, halo=halo, tm=TILE_BAND, kv_heads=DIL_SLOTS,
                             with_lse=True)
            obs.append(o)
            lses.append(lse)
        (oc,) = _banded(zc[:, None], tables["swa_bias"], tables["sink"][i], halo=SWA_WINDOW, tm=TILE_BAND,
                        kv_heads=SWA_KV_HEADS, with_lse=False)
        oc = oc.reshape(bsz, seq, BRANCH_W)
        od = _na(zd, tables["na_bias"][i], rows_per_step=NA_ROWS_PER_STEP)
        x = _merge(x, w, oa, obs, lses, oc, od, TILE_WIDE)
        x = _ffn(x, w, TILE_WIDE)
        x = _ple(x, p, i, w, ln_final, tile_proj, final=(i == DEPTH - 1))
    return x


def kernel(x_prompt, x_sample, p_prompt, p_sample, ln_attn, w_in, q_norm, kv_norm, w_q_up, w_kv_up,
           attn_sink, na_rpb, rel_table, w_gate, w_branch, w_out, ln_ffn, w_ffn_up, ffn_conv_w,
           ffn_conv_b, w_ffn_down, ln_ple, w_ple_gate, w_ple_proj, ln_final):
    weights = _prep_weights(ln_attn, w_in, q_norm, kv_norm, w_q_up, w_kv_up, w_gate, w_branch, w_out,
                            ln_ffn, w_ffn_up, ffn_conv_w, ffn_conv_b, w_ffn_down, ln_ple, w_ple_gate,
                            w_ple_proj)
    assert x_prompt.shape[1] == x_sample.shape[1]
    tables = _tables(x_prompt.shape[1], attn_sink, na_rpb, rel_table)
    ln_final = ln_final[None, :]
    return (_trunk(x_prompt, p_prompt, weights, tables, ln_final),
            _trunk(x_sample, p_sample, weights, tables, ln_final))
```

```python
import functools
import math

import jax
import jax.numpy as jnp
import numpy as np
from jax import lax
from jax.experimental import pallas as pl
from jax.experimental.pallas import tpu as pltpu

D_MODEL = 1024
DEPTH = 4
PLE_DIM = 256
GRID_W = 64
HEAD_DIM = 64
BRANCH_W = 256
N_BRANCH = 4
EPS = 1e-6
NEG_INF = -1e30

MLA_HEADS = 4
MLA_Q_RANK = 256
MLA_KV_RANK = 128
MLA_NOPE = 64
MLA_ROPE = 32
MLA_V = 64
ROPE_THETA = 10000.0
MLA_SLOT = 128
MLA_VROWS = 96
LANES = 128
LOG2_E = math.log2(math.e)

DIL_PAIRS = ((128, 1), (512, 4), (2048, 16))
DIL_SLOTS = 4
DIL_HEADS = DIL_SLOTS * len(DIL_PAIRS)

SWA_Q_HEADS = 4
SWA_KV_HEADS = 2
SWA_WINDOW = 128

NA_HEADS = 4
NA_ROWS = 8
NA_COLS = 16

REL_BUCKETS = 32
REL_MAX_DIST = 1024

D_FF = 2816
CONV_W = 3

A_COLS = MLA_Q_RANK + MLA_KV_RANK + MLA_ROPE
B_COLS = 3 * DIL_HEADS * HEAD_DIM
C_COLS = (SWA_Q_HEADS + 2 * SWA_KV_HEADS) * HEAD_DIM
D_COLS = 3 * NA_HEADS * HEAD_DIM
A_PAD = 512
MAIN_COLS = B_COLS + C_COLS + D_COLS
GROUP_COLS = 3 * BRANCH_W
MAIN_SPLITS = (0, GROUP_COLS, 2 * GROUP_COLS, B_COLS, B_COLS + C_COLS, MAIN_COLS)

BF16 = jnp.bfloat16
F32 = jnp.float32

VMEM_LIMIT_BYTES = 56 * 1024 * 1024


def _params(*semantics):
    return pltpu.CompilerParams(dimension_semantics=semantics, vmem_limit_bytes=VMEM_LIMIT_BYTES)


def _resident(shape):
    return pl.BlockSpec(shape, lambda *_: (0,) * len(shape), pipeline_mode=pl.Buffered(1))


def _dot(a, b):
    return jnp.dot(a, b, preferred_element_type=F32)


def _dot_nt(a, b):
    return lax.dot_general(a, b, (((1,), (1,)), ((), ())), preferred_element_type=F32)


def _rms(x, g):
    return x * lax.rsqrt(jnp.mean(x * x, axis=-1, keepdims=True) + EPS) * g


def _split_residues(val, scr, out_ref, dil):
    tm = val.shape[0]
    nslab = val.shape[1] // LANES
    for j in range(nslab):
        scr[j] = val[:, j * LANES:(j + 1) * LANES]
    for r in range(dil):
        for j in range(nslab):
            out_ref[r, :, j * LANES:(j + 1) * LANES] = scr[j, pl.ds(r, tm // dil, stride=dil), :].astype(BF16)


def _attn_in_kernel(x_ref, g_ref, wmain_ref, wa_ref, qn_ref, kvn_ref, wq_ref, wqs_ref, wk_ref,
                    wvt_ref, es_ref, cos_ref, sin_ref,
                    qa_ref, ka_ref, vt_ref, zb0_ref, zb1_ref, zb2_ref, zc_ref, zd_ref, scr1, scr2):
    hb = _rms(x_ref[...], g_ref[...]).astype(BF16)
    za = _dot(hb, wa_ref[...])
    chunk = lambda c: _dot(hb, wmain_ref[:, MAIN_SPLITS[c]:MAIN_SPLITS[c + 1]])
    zb0_ref[...] = chunk(0).astype(BF16)
    _split_residues(chunk(1), scr1, zb1_ref, DIL_PAIRS[1][1])
    _split_residues(chunk(2), scr2, zb2_ref, DIL_PAIRS[2][1])
    zc_ref[...] = chunk(3).astype(BF16)
    zd_ref[...] = chunk(4).astype(BF16)

    cq = _rms(za[:, :MLA_Q_RANK], qn_ref[...]).astype(BF16)
    ckv = _rms(za[:, MLA_Q_RANK:MLA_Q_RANK + MLA_KV_RANK], kvn_ref[...]).astype(BF16)
    kr = za[:, MLA_Q_RANK + MLA_KV_RANK:]
    kr_hi = kr.astype(BF16).astype(F32)
    kr2 = (kr_hi + pltpu.roll(kr - kr_hi, MLA_ROPE, 1)).astype(BF16)
    cos = jnp.concatenate([cos_ref[...]] * MLA_HEADS, axis=1)
    sin = jnp.concatenate([sin_ref[...]] * MLA_HEADS, axis=1)
    q = _dot(cq, wq_ref[...]) * cos + _dot(cq, wqs_ref[...]) * sin
    qa_ref[...] = (q * (LOG2_E * (MLA_NOPE + MLA_ROPE) ** -0.5)).astype(BF16)
    k_plain = _dot(jnp.concatenate([ckv, kr2], axis=1), wk_ref[...])
    ka_ref[...] = (k_plain * cos + _dot(kr2, es_ref[...]) * sin).astype(BF16)
    vt = _dot_nt(wvt_ref[...], ckv)
    row = lax.broadcasted_iota(jnp.int32, vt.shape, 0)
    vt_ref[...] = jnp.where(row % MLA_VROWS == MLA_V, 1.0, vt).astype(BF16)


def _attn_in(x, w, cos_t, sin_t, tm):
    bsz, seq, _ = x.shape
    tok = lambda width: pl.BlockSpec((None, tm, width), lambda b, i: (b, i, 0))
    table = pl.BlockSpec((tm, MLA_SLOT), lambda b, i: (i, 0))
    group = GROUP_COLS
    slot_w = MLA_HEADS * MLA_SLOT
    out_specs = [tok(slot_w), tok(slot_w),
                 pl.BlockSpec((None, MLA_HEADS * MLA_VROWS, tm), lambda b, i: (b, 0, i)), tok(group)]
    out_shape = [jax.ShapeDtypeStruct((bsz, seq, slot_w), BF16), jax.ShapeDtypeStruct((bsz, seq, slot_w), BF16),
                 jax.ShapeDtypeStruct((bsz, MLA_HEADS * MLA_VROWS, seq), BF16),
                 jax.ShapeDtypeStruct((bsz, seq, group), BF16)]
    scratch = []
    for _, dil in DIL_PAIRS[1:]:
        out_specs.append(pl.BlockSpec((None, dil, tm // dil, group), lambda b, i: (b, 0, i, 0)))
        out_shape.append(jax.ShapeDtypeStruct((bsz, dil, seq // dil, group), BF16))
        scratch.append(pltpu.VMEM((group // LANES, tm, LANES), F32))
    out_specs += [tok(C_COLS), tok(D_COLS)]
    out_shape += [jax.ShapeDtypeStruct((bsz, seq, C_COLS), BF16), jax.ShapeDtypeStruct((bsz, seq, D_COLS), BF16)]
    return pl.pallas_call(
        _attn_in_kernel,
        grid=(bsz, seq // tm),
        in_specs=[tok(D_MODEL), _resident((1, D_MODEL)), _resident((D_MODEL, MAIN_COLS)),
                  _resident((D_MODEL, A_PAD)), _resident((1, MLA_Q_RANK)), _resident((1, MLA_KV_RANK)),
                  _resident((MLA_Q_RANK, slot_w)), _resident((MLA_Q_RANK, slot_w)),
                  _resident((MLA_KV_RANK + MLA_SLOT, slot_w)), _resident((MLA_HEADS * MLA_VROWS, MLA_KV_RANK)),
                  _resident((MLA_SLOT, slot_w)), table, table],
        out_specs=out_specs, out_shape=out_shape, scratch_shapes=scratch,
        compiler_params=_params("parallel", "parallel"),
        name="attn_in",
    )(x, w["ln_attn"], w["w_main"], w["w_a"], w["q_norm"], w["kv_norm"], w["wq"], w["wq_s"],
      w["wk"], w["wv_t"], w["e_s"], cos_t, sin_t)


def _mla_kernel(q_ref, k_ref, vt_ref, o_ref, m_scr, acc_scr):
    kv = pl.program_id(2)

    @pl.when(kv == 0)
    def _():
        m_scr[...] = jnp.full(m_scr.shape, NEG_INF, F32)
        acc_scr[...] = jnp.zeros(acc_scr.shape, F32)

    def scores(h):
        return _dot_nt(k_ref[:, h * MLA_SLOT:(h + 1) * MLA_SLOT], q_ref[:, h * MLA_SLOT:(h + 1) * MLA_SLOT])

    s_next = scores(0)
    for h in range(MLA_HEADS):
        s = s_next
        if h + 1 < MLA_HEADS:
            s_next = scores(h + 1)
        m_prev = m_scr[h]
        m_new = jnp.maximum(m_prev, jnp.max(s, axis=0, keepdims=True))
        p = jnp.exp2(s - m_new).astype(BF16)
        acc_scr[h] = jnp.exp2(m_prev - m_new) * acc_scr[h] + _dot(vt_ref[h * MLA_VROWS:(h + 1) * MLA_VROWS, :], p)
        m_scr[h] = m_new

    @pl.when(kv == pl.num_programs(2) - 1)
    def _():
        o_t = jnp.concatenate([acc_scr[h, :MLA_V, :] / acc_scr[h, MLA_V:MLA_V + 1, :] for h in range(MLA_HEADS)],
                              axis=0)
        o_ref[...] = o_t.T.astype(BF16)


def _mla(qa, ka, vt, tq, tk):
    bsz, seq, _ = qa.shape
    tq, tk = min(tq, seq), min(tk, seq)
    return pl.pallas_call(
        _mla_kernel,
        grid=(bsz, seq // tq, seq // tk),
        in_specs=[pl.BlockSpec((None, tq, MLA_HEADS * MLA_SLOT), lambda b, i, j: (b, i, 0)),
                  pl.BlockSpec((None, tk, MLA_HEADS * MLA_SLOT), lambda b, i, j: (b, j, 0)),
                  pl.BlockSpec((None, MLA_HEADS * MLA_VROWS, tk), lambda b, i, j: (b, 0, j))],
        out_specs=pl.BlockSpec((None, tq, MLA_HEADS * MLA_V), lambda b, i, j: (b, i, 0)),
        out_shape=jax.ShapeDtypeStruct((bsz, seq, MLA_HEADS * MLA_V), BF16),
        scratch_shapes=[pltpu.VMEM((MLA_HEADS, 1, tq), F32), pltpu.VMEM((MLA_HEADS, MLA_VROWS, tq), F32)],
        compiler_params=_params("parallel", "parallel", "arbitrary"),
        name="mla_attention",
    )(qa, ka, vt)


def _head_rows(x, nheads):
    head = lax.broadcasted_iota(jnp.int32, x.shape, 1) // HEAD_DIM
    xf = x.astype(F32)
    return jnp.concatenate([jnp.where(head == h, xf, 0.0) for h in range(nheads)], axis=0).astype(x.dtype)


def _head_cols(x, nheads):
    n = x.shape[0] // nheads
    head = lax.broadcasted_iota(jnp.int32, (n, x.shape[1]), 1) // HEAD_DIM
    out = x[:n]
    for h in range(1, nheads):
        out = jnp.where(head == h, x[h * n:(h + 1) * n], out)
    return out


BAND_UNIT = 128
ONES_ROWS = 16
LN_2 = math.log(2.0)


def _query_rows(q, kv_heads):
    if kv_heads == DIL_SLOTS:
        return _head_rows(q, DIL_SLOTS)
    kvw = kv_heads * HEAD_DIM
    slot = lax.broadcasted_iota(jnp.int32, (q.shape[0], kvw), 1) // HEAD_DIM
    tiles = [q[:, t * kvw:(t + 1) * kvw].astype(F32) for t in range(DIL_SLOTS // kv_heads)]
    rows = [jnp.where(slot == h // (DIL_SLOTS // kv_heads), tiles[h % (DIL_SLOTS // kv_heads)], 0.0)
            for h in range(DIL_SLOTS)]
    return jnp.concatenate(rows, axis=0).astype(q.dtype)


def _banded_kernel(*refs, halo, tm, seq_len, tile_axis, kv_heads, with_sink, with_lse):
    q_ref, kp_ref, km_ref, kn_ref, vp_ref, vm_ref, vn_ref, bias_ref = refs[:8]
    refs = refs[8:]
    sink_ref = None
    if with_sink:
        sink_ref, refs = refs[0], refs[1:]
    o_ref = refs[0]
    lse_ref = refs[1] if with_lse else None

    tile = pl.program_id(tile_axis)
    kvw = kv_heads * HEAD_DIM
    kcat = jnp.concatenate([kp_ref[...], km_ref[...], kn_ref[...]], axis=0)
    vcat = jnp.concatenate([vp_ref[...], vm_ref[...], vn_ref[...]], axis=0)
    vt = vcat.astype(F32).T
    ones = jnp.where(lax.broadcasted_iota(jnp.int32, (ONES_ROWS, vt.shape[1]), 0) == 0, 1.0, 0.0)
    vt = jnp.concatenate([vt, ones], axis=0).astype(BF16)
    nk = BAND_UNIT + 2 * halo
    nunit = tm // BAND_UNIT
    lanes = lambda x, h: x[:, h * BAND_UNIT:(h + 1) * BAND_UNIT]

    def scores(u):
        qbd = _query_rows(q_ref[u * BAND_UNIT:(u + 1) * BAND_UNIT, :], kv_heads)
        return _dot_nt(kcat[u * BAND_UNIT:u * BAND_UNIT + nk], qbd)

    raw = [scores(u) for u in range(nunit)]
    outs = []
    for u in range(nunit):
        s = raw[u] * (LOG2_E * HEAD_DIM ** -0.5) + bias_ref[...]
        if tm >= seq_len:
            start = (u * BAND_UNIT) % seq_len
            first, last = start == 0, start == seq_len - BAND_UNIT
        else:
            start = (tile % (seq_len // tm)) * tm + u * BAND_UNIT
            first, last = u == 0, u == nunit - 1
        if first or last:
            kpos = start - halo + lax.broadcasted_iota(jnp.int32, s.shape, 0)
            if first:
                s = jnp.where(kpos >= 0, s, NEG_INF)
            if last:
                s = jnp.where(kpos < seq_len, s, NEG_INF)
        m = jnp.max(s, axis=0, keepdims=True)
        if with_sink:
            m = jnp.maximum(m, sink_ref[...])
        p = jnp.exp2(s - m).astype(BF16)
        acc = _dot(vt[:, u * BAND_UNIT:u * BAND_UNIT + nk], p)
        l = acc[kvw:kvw + 1]
        if with_sink:
            l = l + jnp.exp2(sink_ref[...] - m)
        inv = 1.0 / l
        kv_row = lambda h: (h * kv_heads // DIL_SLOTS) * HEAD_DIM
        out_t = [lanes(acc[kv_row(h):kv_row(h) + HEAD_DIM], h) * lanes(inv, h) for h in range(DIL_SLOTS)]
        if with_lse:
            lse = m * LN_2 + jnp.log(l)
            out_t += [jnp.broadcast_to(lanes(lse, h), (HEAD_DIM, BAND_UNIT)) for h in range(DIL_SLOTS)]
        outs.append(jnp.concatenate(out_t, axis=0))
    for u in range(nunit):
        rows = slice(u * BAND_UNIT, (u + 1) * BAND_UNIT)
        out = outs[u].T
        o_ref[rows, :] = out[:, :BRANCH_W].astype(BF16)
        if with_lse:
            lse_ref[rows, :] = out[:, BRANCH_W:]


def _banded(z, bias, sink, *, halo, tm, kv_heads, with_lse):
    bsz, dil, sub, ncol = z.shape
    fold = max(1, min(dil, tm // sub))
    assert dil % fold == 0
    z = z.reshape(bsz, dil // fold, fold * sub, ncol)
    nres, rows = z.shape[1], z.shape[2]
    tm = min(tm, rows)
    assert tm % BAND_UNIT == 0 and BAND_UNIT % halo == 0 and (tm % sub == 0 or sub % tm == 0)
    blk = halo
    per_tile = tm // blk
    last_blk = rows // blk - 1
    nk = BAND_UNIT + 2 * halo
    kvw = kv_heads * HEAD_DIM
    first_kv = BRANCH_W // kvw

    def main(j):
        return pl.BlockSpec((None, None, tm, kvw), lambda b, r, i: (b, r, i, first_kv + j))

    def prev(j):
        return pl.BlockSpec((None, None, blk, kvw),
                            lambda b, r, i: (b, r, jnp.maximum(i * per_tile - 1, 0), first_kv + j))

    def nxt(j):
        return pl.BlockSpec((None, None, blk, kvw),
                            lambda b, r, i: (b, r, jnp.minimum((i + 1) * per_tile, last_blk), first_kv + j))

    in_specs = [pl.BlockSpec((None, None, tm, BRANCH_W), lambda b, r, i: (b, r, i, 0)),
                prev(0), main(0), nxt(0), prev(1), main(1), nxt(1),
                _resident((nk, DIL_SLOTS * BAND_UNIT))]
    args = [z, z, z, z, z, z, z, bias]
    if sink is not None:
        in_specs.append(_resident((1, DIL_SLOTS * BAND_UNIT)))
        args.append(sink)
    out_spec = pl.BlockSpec((None, None, tm, BRANCH_W), lambda b, r, i: (b, r, i, 0))
    out_specs = [out_spec]
    out_shape = [jax.ShapeDtypeStruct((bsz, nres, rows, BRANCH_W), BF16)]
    if with_lse:
        out_specs.append(out_spec)
        out_shape.append(jax.ShapeDtypeStruct((bsz, nres, rows, BRANCH_W), F32))
    outs = pl.pallas_call(
        functools.partial(_banded_kernel, halo=halo, tm=tm, seq_len=sub, tile_axis=2, kv_heads=kv_heads,
                          with_sink=sink is not None, with_lse=with_lse),
        grid=(bsz, nres, rows // tm),
        in_specs=in_specs, out_specs=out_specs, out_shape=out_shape,
        compiler_params=_params("parallel", "parallel", "parallel"),
        name=f"banded_d{dil}_h{halo}_kv{kv_heads}",
    )(*args)
    return [o.reshape(bsz, dil, sub, BRANCH_W) for o in outs]


def _na_kernel(q_ref, k_ref, v_ref, bias_ref, o_ref, *, rows, rows_per_step):
    step = pl.program_id(1)
    nkeys = NA_ROWS * GRID_W

    def window(rr):
        r = step * rows_per_step + rr
        sr = jnp.clip(r - NA_ROWS // 2, 0, rows - NA_ROWS)
        return sr - r + NA_ROWS - 1, pl.multiple_of(sr * GRID_W, GRID_W)

    def scores(rr):
        qbd = _head_rows(q_ref[rr * GRID_W:(rr + 1) * GRID_W, :], NA_HEADS)
        return _dot_nt(qbd, k_ref[pl.ds(wins[rr][1], nkeys), :])

    wins = [window(rr) for rr in range(rows_per_step)]
    raw = [scores(rr) for rr in range(rows_per_step)]
    for rr in range(rows_per_step):
        variant, start = wins[rr]
        s = raw[rr] * (HEAD_DIM ** -0.5) + bias_ref[variant]
        p = jnp.exp(s - jnp.max(s, axis=1, keepdims=True))
        obd = _dot(p.astype(BF16), v_ref[pl.ds(start, nkeys), :]) / jnp.sum(p, axis=1, keepdims=True)
        o_ref[rr * GRID_W:(rr + 1) * GRID_W, :] = _head_cols(obd, NA_HEADS).astype(BF16)


def _na(zd, bias, rows_per_step):
    bsz, seq, _ = zd.shape
    rows = seq // GRID_W
    assert rows >= NA_ROWS and rows % rows_per_step == 0
    tq = rows_per_step * GRID_W
    whole = lambda j: pl.BlockSpec((None, seq, BRANCH_W), lambda b, i: (b, 0, j))
    return pl.pallas_call(
        functools.partial(_na_kernel, rows=rows, rows_per_step=rows_per_step),
        grid=(bsz, rows // rows_per_step),
        in_specs=[pl.BlockSpec((None, tq, BRANCH_W), lambda b, i: (b, i, 0)), whole(1), whole(2),
                  _resident((NA_ROWS, NA_HEADS * GRID_W, NA_ROWS * GRID_W))],
        out_specs=pl.BlockSpec((None, tq, BRANCH_W), lambda b, i: (b, i, 0)),
        out_shape=jax.ShapeDtypeStruct((bsz, seq, BRANCH_W), BF16),
        compiler_params=_params("parallel", "arbitrary"),
        name="neighborhood_attention",
    )(zd, zd, zd, bias)


def _join_residues(ref, scr):
    dil, rows, width = ref.shape
    for j in range(width // LANES):
        for r in range(dil):
            scr[j, pl.ds(r, rows, stride=dil), :] = ref[r, :, j * LANES:(j + 1) * LANES].astype(F32)
    return jnp.concatenate([scr[j] for j in range(width // LANES)], axis=1)


def _merge_kernel(x_ref, g_ref, oa_ref, ob0_ref, ob1_ref, ob2_ref, l0_ref, l1_ref, l2_ref,
                  oc_ref, od_ref, wg_ref, wb_ref, wo_ref, out_ref, *scratch):
    x = x_ref[...]
    hb = _rms(x, g_ref[...]).astype(BF16)
    gates = [_dot(hb, wg_ref[j]) for j in range(N_BRANCH)]
    outs = (ob0_ref[0].astype(F32), _join_residues(ob1_ref, scratch[0]), _join_residues(ob2_ref, scratch[1]))
    lses = (l0_ref[0], _join_residues(l1_ref, scratch[2]), _join_residues(l2_ref, scratch[3]))
    top = jnp.maximum(jnp.maximum(lses[0], lses[1]), lses[2])
    ws = [jnp.exp(l - top) for l in lses]
    den = ws[0] + ws[1] + ws[2]
    ob = sum((wgt / den) * o for wgt, o in zip(ws, outs))
    branches = (oa_ref[...], ob.astype(BF16), oc_ref[...], od_ref[...])
    merged = None
    for j, o in enumerate(branches):
        term = jax.nn.sigmoid(gates[j]) * _dot(o, wb_ref[j])
        merged = term if merged is None else merged + term
    out_ref[...] = x + _dot(merged.astype(BF16), wo_ref[...])


def _merge(x, w, oa, obs, lses, oc, od, tm):
    bsz, seq, _ = x.shape
    tok = lambda width: pl.BlockSpec((None, tm, width), lambda b, i: (b, i, 0))
    residues = [pl.BlockSpec((None, dil, tm // dil, BRANCH_W), lambda b, i: (b, 0, i, 0)) for _, dil in DIL_PAIRS]
    return pl.pallas_call(
        _merge_kernel,
        grid=(bsz, seq // tm),
        in_specs=[tok(D_MODEL), _resident((1, D_MODEL)), tok(BRANCH_W)] + residues + residues
                 + [tok(BRANCH_W), tok(BRANCH_W),
                    _resident((N_BRANCH, D_MODEL, D_MODEL)), _resident((N_BRANCH, BRANCH_W, D_MODEL)),
                    _resident((D_MODEL, D_MODEL))],
        out_specs=tok(D_MODEL),
        out_shape=jax.ShapeDtypeStruct(x.shape, F32),
        scratch_shapes=[pltpu.VMEM((BRANCH_W // LANES, tm, LANES), F32)] * 4,
        compiler_params=_params("parallel", "parallel"),
        name="merge",
    )(x, w["ln_attn"], oa, *obs, *lses, oc, od, w["w_gate"], w["w_branch"], w["w_out"])


SUBLANES = 8
GELU_C0 = math.sqrt(2.0 / math.pi)
GELU_C1 = GELU_C0 * 0.044715


def _ffn_kernel(x_ref, xp_ref, xn_ref, g_ref, wu_ref, cw_ref, cb_ref, wd_ref, out_ref, *, tm):
    tile = pl.program_id(1)
    g = g_ref[...]
    x = x_ref[...]
    hn = _rms(x, g).astype(BF16)
    keep_prev = jnp.where(tile > 0, 1.0, 0.0)
    keep_next = jnp.where(tile < pl.num_programs(1) - 1, 1.0, 0.0)
    halo = jnp.concatenate([_rms(xp_ref[...], g) * keep_prev, _rms(xn_ref[...], g) * keep_next], axis=0)
    lhs = jnp.concatenate([hn, halo.astype(BF16)], axis=0)
    halves = ((0, D_FF), (D_FF, 2 * D_FF))
    ups = [_dot(lhs, wu_ref[:, c0:c1]) for c0, c1 in halves]
    row = lax.broadcasted_iota(jnp.int32, (SUBLANES, D_FF), 0)

    def conv(idx):
        (c0, c1), u, uh = halves[idx], ups[idx][:tm], ups[idx][tm:]
        below = pltpu.roll(u, 1, 0)
        below = jnp.concatenate([jnp.where(row == 0, uh[SUBLANES - 1:SUBLANES], below[:SUBLANES]),
                                 below[SUBLANES:]], axis=0)
        above = pltpu.roll(u, tm - 1, 0)
        above = jnp.concatenate([above[:tm - SUBLANES],
                                 jnp.where(row == SUBLANES - 1, uh[SUBLANES:SUBLANES + 1], above[tm - SUBLANES:])], axis=0)
        return below * cw_ref[0:1, c0:c1] + u * cw_ref[1:2, c0:c1] + above * cw_ref[2:3, c0:c1] + cb_ref[:, c0:c1]

    a, half_b = conv(0), conv(1)
    act = a * (1.0 + jnp.tanh(a * (GELU_C0 + GELU_C1 * (a * a)))) * half_b
    out_ref[...] = x + _dot(act.astype(BF16), wd_ref[...])


def _ffn(x, w, tm):
    bsz, seq, _ = x.shape
    per_tile = tm // SUBLANES
    last = seq // SUBLANES - 1
    tok = pl.BlockSpec((None, tm, D_MODEL), lambda b, i: (b, i, 0))
    return pl.pallas_call(
        functools.partial(_ffn_kernel, tm=tm),
        grid=(bsz, seq // tm),
        in_specs=[tok,
                  pl.BlockSpec((None, SUBLANES, D_MODEL), lambda b, i: (b, jnp.maximum(i * per_tile - 1, 0), 0)),
                  pl.BlockSpec((None, SUBLANES, D_MODEL), lambda b, i: (b, jnp.minimum((i + 1) * per_tile, last), 0)),
                  _resident((1, D_MODEL)), _resident((D_MODEL, 2 * D_FF)), _resident((CONV_W, 2 * D_FF)),
                  _resident((1, 2 * D_FF)), _resident((D_FF, D_MODEL))],
        out_specs=tok,
        out_shape=jax.ShapeDtypeStruct(x.shape, F32),
        compiler_params=_params("parallel", "parallel"),
        name="conv_ffn",
    )(x, x, x, w["ln_ffn"], w["w_ffn_up"], w["ffn_conv_w"], w["ffn_conv_b"], w["w_ffn_down"])


def _ple_kernel(x_ref, p_ref, g_ref, wg_ref, wp_ref, gf_ref, out_ref, *, final):
    x = x_ref[...]
    emb = _dot(p_ref[...].astype(BF16), wp_ref[...])
    gate = jax.nn.sigmoid(_dot(_rms(x, g_ref[...]).astype(BF16), wg_ref[...]))
    y = x + gate * emb
    if final:
        y = _rms(y, gf_ref[...])
    out_ref[...] = y


def _ple(x, p, layer, w, ln_final, tm, final):
    bsz, seq, _ = x.shape
    tok = lambda width: pl.BlockSpec((None, tm, width), lambda b, i: (b, i, 0))
    return pl.pallas_call(
        functools.partial(_ple_kernel, final=final),
        grid=(bsz, seq // tm),
        in_specs=[tok(D_MODEL), pl.BlockSpec((None, None, tm, PLE_DIM), lambda b, i: (layer, b, i, 0)),
                  _resident((1, D_MODEL)), _resident((D_MODEL, D_MODEL)),
                  _resident((PLE_DIM, D_MODEL)), _resident((1, D_MODEL))],
        out_specs=tok(D_MODEL),
        out_shape=jax.ShapeDtypeStruct(x.shape, F32),
        compiler_params=_params("parallel", "parallel"),
        name="ple_final" if final else "ple",
    )(x, p, w["ln_ple"], w["w_ple_gate"], w["w_ple_proj"], ln_final)


def _gather_cols(wmat, cols, sign=None):
    cols = np.asarray(cols)
    picked = jnp.take(wmat, jnp.asarray(np.maximum(cols, 0)), axis=-1)
    scale = (cols >= 0).astype(np.float32) * (1.0 if sign is None else np.asarray(sign, np.float32))
    return picked * jnp.asarray(scale)


def _prep_weights(ln_attn, w_in, q_norm, kv_norm, w_q_up, w_kv_up, w_gate, w_branch, w_out, ln_ffn,
                  w_ffn_up, ffn_conv_w, ffn_conv_b, w_ffn_down, ln_ple, w_ple_gate, w_ple_proj):
    half = MLA_ROPE // 2
    qk_dim = MLA_NOPE + MLA_ROPE
    slot_w = MLA_HEADS * MLA_SLOT

    c0 = A_COLS + B_COLS
    group = SWA_Q_HEADS // SWA_KV_HEADS
    q_order = [g + group * kv for g in range(group) for kv in range(SWA_KV_HEADS)]
    q_heads = [w_in[..., c0 + h * HEAD_DIM:c0 + (h + 1) * HEAD_DIM] for h in q_order]
    w_main = jnp.concatenate([w_in[..., A_COLS:c0]] + q_heads + [w_in[..., c0 + SWA_Q_HEADS * HEAD_DIM:]], axis=-1)
    w_a = jnp.pad(w_in[..., :A_COLS], ((0, 0), (0, 0), (0, A_PAD - A_COLS)))

    q_cols = np.full((slot_w,), -1)
    qs_cols = np.full((slot_w,), -1)
    qs_sign = np.ones((slot_w,), np.float32)
    k_cols = np.full((slot_w,), -1)
    e_rows = np.full((slot_w,), -1)
    es_rows = np.full((slot_w,), -1)
    es_sign = np.ones((slot_w,), np.float32)
    for h in range(MLA_HEADS):
        base = h * MLA_SLOT
        q_cols[base:base + qk_dim] = h * qk_dim + np.arange(qk_dim)
        k_cols[base:base + MLA_NOPE] = h * (MLA_NOPE + MLA_V) + np.arange(MLA_NOPE)
        rope0 = base + MLA_NOPE
        qs_cols[rope0:rope0 + half] = h * qk_dim + MLA_NOPE + half + np.arange(half)
        qs_sign[rope0:rope0 + half] = -1.0
        qs_cols[rope0 + half:rope0 + MLA_ROPE] = h * qk_dim + MLA_NOPE + np.arange(half)
        e_rows[rope0:rope0 + MLA_ROPE] = np.arange(MLA_ROPE)
        es_rows[rope0:rope0 + half] = half + np.arange(half)
        es_sign[rope0:rope0 + half] = -1.0
        es_rows[rope0 + half:rope0 + MLA_ROPE] = np.arange(half)
    v_cols = np.full((MLA_HEADS, MLA_VROWS), -1)
    v_cols[:, :MLA_V] = np.arange(MLA_HEADS)[:, None] * (MLA_NOPE + MLA_V) + MLA_NOPE + np.arange(MLA_V)
    v_cols = v_cols.reshape(-1)
    eye = jnp.eye(MLA_SLOT, dtype=F32)

    def place(rows, sign=None):
        m = _gather_cols(eye, rows, sign)
        return m.at[MLA_ROPE:2 * MLA_ROPE].set(m[:MLA_ROPE])

    wk = jnp.concatenate([_gather_cols(w_kv_up, k_cols),
                          jnp.broadcast_to(place(e_rows), (DEPTH, MLA_SLOT, slot_w))], axis=1)

    row = lambda a: a[:, None, :]
    gate_half = jnp.asarray(np.concatenate([np.ones(D_FF, np.float32), np.full(D_FF, 0.5, np.float32)]))
    stacked = {
        "ln_attn": row(ln_attn), "q_norm": row(q_norm), "kv_norm": row(kv_norm),
        "w_main": w_main.astype(BF16),
        "w_a": w_a.astype(BF16),
        "wq": _gather_cols(w_q_up, q_cols).astype(BF16),
        "wq_s": _gather_cols(w_q_up, qs_cols, qs_sign).astype(BF16),
        "wk": wk.astype(BF16),
        "wv_t": jnp.swapaxes(_gather_cols(w_kv_up, v_cols), -1, -2).astype(BF16),
        "w_gate": w_gate.astype(BF16), "w_branch": w_branch.astype(BF16), "w_out": w_out.astype(BF16),
        "ln_ffn": row(ln_ffn), "w_ffn_up": w_ffn_up.astype(BF16), "ffn_conv_w": ffn_conv_w * gate_half,
        "ffn_conv_b": row(ffn_conv_b * gate_half), "w_ffn_down": w_ffn_down.astype(BF16),
        "ln_ple": row(ln_ple), "w_ple_gate": w_ple_gate.astype(BF16), "w_ple_proj": w_ple_proj.astype(BF16),
    }
    shared = {"e_s": place(es_rows, es_sign).astype(BF16)}
    return [dict({k: v[i] for k, v in stacked.items()}, **shared) for i in range(DEPTH)]


def _rope_tables(seq):
    inv = ROPE_THETA ** (-jnp.arange(0, MLA_ROPE, 2, dtype=F32) / MLA_ROPE)
    ang = jnp.arange(seq, dtype=F32)[:, None] * inv[None]
    cos, sin = jnp.cos(ang), jnp.sin(ang)
    pad = MLA_SLOT - MLA_NOPE - MLA_ROPE
    cos_t = jnp.concatenate([jnp.ones((seq, MLA_NOPE), F32), cos, cos, jnp.zeros((seq, pad), F32)], axis=1)
    sin_t = jnp.concatenate([jnp.zeros((seq, MLA_NOPE), F32), sin, sin, jnp.zeros((seq, pad), F32)], axis=1)
    return cos_t, sin_t


def _t5_bucket(rel):
    nb = REL_BUCKETS // 2
    max_exact = nb // 2
    ret = jnp.where(rel > 0, nb, 0)
    n = jnp.abs(rel)
    nf = jnp.maximum(n, 1).astype(F32)
    large = max_exact + (jnp.log(nf / max_exact) / math.log(REL_MAX_DIST / max_exact) * (nb - max_exact)).astype(jnp.int32)
    large = jnp.minimum(large, nb - 1)
    return ret + jnp.where(n < max_exact, n, large)


def _band_bias(rel_table, window, dil, head0):
    nk = BAND_UNIT + 2 * window
    off = np.arange(-(window + BAND_UNIT - 1), window + BAND_UNIT)
    picked = jax.nn.one_hot(_t5_bucket(jnp.asarray(off * dil)), REL_BUCKETS, dtype=F32)
    per_off = jnp.dot(picked, rel_table[:, head0:head0 + DIL_SLOTS].astype(F32), precision=lax.Precision.HIGHEST)
    per_off = jnp.where(jnp.asarray(np.abs(off) <= window)[:, None], per_off * LOG2_E, NEG_INF)
    cols = [per_off[BAND_UNIT - 1 - q:BAND_UNIT - 1 - q + nk] for q in range(BAND_UNIT)]
    return jnp.transpose(jnp.stack(cols, axis=0), (1, 2, 0)).reshape(nk, DIL_SLOTS * BAND_UNIT)


def _na_bias(rpb):
    qc = np.arange(GRID_W)[:, None]
    kc = np.arange(GRID_W)[None, :]
    sc = np.clip(qc - NA_COLS // 2, 0, GRID_W - NA_COLS)
    valid = (kc >= sc) & (kc < sc + NA_COLS)
    edge = GRID_W - NA_COLS
    ext = jnp.concatenate([jnp.repeat(rpb[..., :1], edge, axis=-1), rpb, jnp.repeat(rpb[..., -1:], edge, axis=-1)],
                          axis=-1).astype(F32)
    table = jnp.stack([ext[..., GRID_W - 1 - q:2 * GRID_W - 1 - q] for q in range(GRID_W)], axis=2)
    table = jnp.where(jnp.asarray(valid)[None, None], table, NEG_INF)
    variants = [jnp.transpose(table[:, v:v + NA_ROWS], (0, 2, 1, 3)).reshape(NA_HEADS * GRID_W, NA_ROWS * GRID_W)
                for v in range(NA_ROWS)]
    return jnp.stack(variants)


def _tables(seq, attn_sink, na_rpb, rel_table):
    cos_t, sin_t = _rope_tables(seq)
    return {
        "cos": cos_t, "sin": sin_t,
        "dil_bias": [_band_bias(rel_table, window // (2 * dil), dil, gi * DIL_SLOTS)
                     for gi, (window, dil) in enumerate(DIL_PAIRS)],
        "swa_bias": _band_bias(rel_table, SWA_WINDOW, 1, DIL_HEADS),
        "sink": [jnp.repeat(attn_sink[i].astype(F32) * LOG2_E, BAND_UNIT)[None, :] for i in range(DEPTH)],
        "na_bias": [_na_bias(na_rpb[i]) for i in range(DEPTH)],
    }


TILE_PROJ = 1024
TILE_WIDE = 512
TILE_MLA_Q, TILE_MLA_K = 1024, 2048
TILE_BAND = 4096
NA_ROWS_PER_STEP = 32


def _trunk(x, p, weights, tables, ln_final):
    bsz, seq, _ = x.shape
    tile_proj = min(TILE_PROJ, seq)
    for i in range(DEPTH):
        w = weights[i]
        qa, ka, vt, zb0, zb1, zb2, zc, zd = _attn_in(x, w, tables["cos"], tables["sin"], tile_proj)
        oa = _mla(qa, ka, vt, TILE_MLA_Q, TILE_MLA_K)
        obs, lses = [], []
        for gi, zb in enumerate((zb0[:, None], zb1, zb2)):
            halo = DIL_PAIRS[gi][0] // (2 * DIL_PAIRS[gi][1])
            o, lse = _banded(zb, tables["dil_bias"][gi], None, halo=halo, tm=TILE_BAND, kv_heads=DIL_SLOTS,
                             with_lse=True)
            obs.append(o)
            lses.append(lse)
        (oc,) = _banded(zc[:, None], tables["swa_bias"], tables["sink"][i], halo=SWA_WINDOW, tm=TILE_BAND,
                        kv_heads=SWA_KV_HEADS, with_lse=False)
        oc = oc.reshape(bsz, seq, BRANCH_W)
        od = _na(zd, tables["na_bias"][i], rows_per_step=NA_ROWS_PER_STEP)
        x = _merge(x, w, oa, obs, lses, oc, od, TILE_WIDE)
        x = _ffn(x, w, TILE_WIDE)
        x = _ple(x, p, i, w, ln_final, tile_proj, final=(i == DEPTH - 1))
    return x


def kernel(x_prompt, x_sample, p_prompt, p_sample, ln_attn, w_in, q_norm, kv_norm, w_q_up, w_kv_up,
           attn_sink, na_rpb, rel_table, w_gate, w_branch, w_out, ln_ffn, w_ffn_up, ffn_conv_w,
           ffn_conv_b, w_ffn_down, ln_ple, w_ple_gate, w_ple_proj, ln_final):
    weights = _prep_weights(ln_attn, w_in, q_norm, kv_norm, w_q_up, w_kv_up, w_gate, w_branch, w_out,
                            ln_ffn, w_ffn_up, ffn_conv_w, ffn_conv_b, w_ffn_down, ln_ple, w_ple_gate,
                            w_ple_proj)
    assert x_prompt.shape[1] == x_sample.shape[1]
    tables = _tables(x_prompt.shape[1], attn_sink, na_rpb, rel_table)
    ln_final = ln_final[None, :]
    return (_trunk(x_prompt, p_prompt, weights, tables, ln_final),
            _trunk(x_sample, p_sample, weights, tables, ln_final))
```
